```python
import math
import jax, jax.numpy as jnp
from jax import lax
import numpy as np

D_MODEL = 4096
BATCH = 2
SEQ = 4096
DEPTH = 2

F32 = jnp.float32
HEAD_DIM = 128
N_SB_HEADS = 8
N_NSA_HEADS = 8
NSA_KV_HEADS = 2
NSA_Q_PER_KV = N_NSA_HEADS // NSA_KV_HEADS
N_DIFF_HEADS = 8
DIFF_QK_DIM = HEAD_DIM // 2
BRANCH_WIDTH = N_SB_HEADS * HEAD_DIM
N_BRANCHES = 3
Q_BLOCK = 128
CMP_LEN = 32
CMP_STRIDE = 16
SLC_LEN = 64
N_SLC = 16
N_LOCAL_SLC = 2
WINDOW = 512
REL_BUCKETS = 32
REL_MAX_DIST = 128
N_EXPERTS = 16
N_EXPERT_GROUPS = 4
EXPERTS_PER_GROUP = N_EXPERTS // N_EXPERT_GROUPS
TOP_K = 2
D_EXPERT = 1024
NORM_EPS = 1e-6
NEG_INF = -1e30
FORCE_SCORE = 1e9
KV_W = NSA_KV_HEADS * HEAD_DIM
IN_SPLITS = (
    BRANCH_WIDTH, BRANCH_WIDTH, BRANCH_WIDTH,
    BRANCH_WIDTH, KV_W, KV_W, KV_W, KV_W, KV_W, KV_W,
    3 * N_NSA_HEADS,
    BRANCH_WIDTH, BRANCH_WIDTH, BRANCH_WIDTH,
    N_BRANCHES * D_MODEL,
)
D_IN = sum(IN_SPLITS)

kernel_name = "hybrid_sb_nsa_diff_moe_adaln"


def rms_norm(x, gain):
    xf = x.astype(F32)
    y = xf * lax.rsqrt(jnp.mean(xf * xf, axis=-1, keepdims=True) + NORM_EPS)
    return (y * gain.astype(F32)).astype(x.dtype)


def rel_bucket(dist):
    n = jnp.maximum(dist, 0)
    max_exact = REL_BUCKETS // 2
    nf = jnp.maximum(n, max_exact).astype(F32)
    large = max_exact + (jnp.log(nf / max_exact) / math.log(REL_MAX_DIST / max_exact)
                         * (REL_BUCKETS - max_exact)).astype(jnp.int32)
    large = jnp.minimum(large, REL_BUCKETS - 1)
    return jnp.where(n < max_exact, n, large)


def masked_softmax(s, mask):
    s = jnp.where(mask, s, NEG_INF)
    p = jnp.exp(s - jnp.max(s, axis=-1, keepdims=True)) * mask
    return p / jnp.maximum(jnp.sum(p, axis=-1, keepdims=True), 1.0)


def rel_bias_gqa(table_g, bkt):
    out = jax.vmap(lambda t, b: t[b], in_axes=(1, 1), out_axes=1)(table_g, bkt)
    return jnp.moveaxis(out, -1, 2)


def stick_breaking_attention(q, k, v):
    B, S, H, Dh = q.shape
    nb = S // Q_BLOCK
    scale = Dh ** -0.5
    kf, vf = k.astype(F32), v.astype(F32)
    key_pos = jnp.arange(S)
    qb = jnp.moveaxis(q.astype(F32).reshape(B, nb, Q_BLOCK, H, Dh), 1, 0)

    def block(args):
        q_blk, i = args
        qpos = i * Q_BLOCK + jnp.arange(Q_BLOCK)
        causal = key_pos[None, :] < qpos[:, None]
        z = jnp.einsum('bqhd,bshd->bhqs', q_blk, kf) * scale
        log_fail = jnp.where(causal, jax.nn.log_sigmoid(-z), 0.0)
        suffix = lax.cumsum(log_fail, axis=3, reverse=True) - log_fail
        a = jnp.where(causal, jnp.exp(jax.nn.log_sigmoid(z) + suffix), 0.0)
        return jnp.einsum('bhqs,bshd->bqhd', a, vf)

    out = lax.map(block, (qb, jnp.arange(nb)))
    return jnp.moveaxis(out, 0, 1).reshape(B, S, H, Dh).astype(q.dtype)


def compress_blocks(x, pos, w1, w2):
    B, S, G, Dh = x.shape
    n_cmp = (S - CMP_LEN) // CMP_STRIDE + 1
    idx = jnp.arange(n_cmp)[:, None] * CMP_STRIDE + jnp.arange(CMP_LEN)[None, :]
    blocks = x[:, idx] + pos[:, None, :]
    blocks = jnp.moveaxis(blocks, 3, 2).reshape(B, n_cmp, G, CMP_LEN * Dh)
    return jax.nn.silu(blocks @ w1) @ w2


def nsa_attention(q, kc, vc, ks, vs, kw, vw, gate_logits, bias_tab,
                  cmp_pos, cmp_w1, cmp_w2, q_gain, k_gain):
    dtype = q.dtype
    B, S, H, Dh = q.shape
    G, R = NSA_KV_HEADS, NSA_Q_PER_KV
    nb = S // Q_BLOCK
    scale = Dh ** -0.5
    q, kc, vc, ks, vs, kw, vw = [a.astype(F32) for a in (q, kc, vc, ks, vs, kw, vw)]
    qg = rms_norm(q, q_gain).reshape(B, S, G, R, Dh)
    t = jnp.arange(S)
    table_g = bias_tab.reshape(REL_BUCKETS, G, R)

    kcm = rms_norm(compress_blocks(kc, cmp_pos[0], cmp_w1[0], cmp_w2[0]), k_gain[0])
    vcm = compress_blocks(vc, cmp_pos[1], cmp_w1[1], cmp_w2[1])
    n_cmp = kcm.shape[1]
    c_start = jnp.arange(n_cmp) * CMP_STRIDE
    c_end = c_start + CMP_LEN - 1
    c_dist = t[:, None] - c_end[None, :]
    c_bias = bias_tab[rel_bucket(c_dist)].reshape(S, n_cmp, G, R).transpose(2, 3, 0, 1)
    s_cmp = jnp.einsum('bqgrd,bngd->bgrqn', qg, kcm) * scale + c_bias
    p_cmp = masked_softmax(s_cmp, c_dist >= 0)
    o_cmp = jnp.einsum('bgrqn,bngd->bqgrd', p_cmp, vcm).reshape(B, S, H, Dh)

    n_blk = S // SLC_LEN
    b_start = jnp.arange(n_blk) * SLC_LEN
    overlap = ((c_start[:, None] < b_start[None, :] + SLC_LEN)
               & (c_start[:, None] + CMP_LEN > b_start[None, :])).astype(F32)
    imp = jnp.einsum('bgrqn,nj->bgqj', p_cmp, overlap)
    cur = t // SLC_LEN
    blk = jnp.arange(n_blk)
    forced = (blk[None, :] == 0) | ((blk[None, :] <= cur[:, None])
                                    & (blk[None, :] > cur[:, None] - N_LOCAL_SLC))
    future = blk[None, :] > cur[:, None]
    imp = jnp.where(forced, FORCE_SCORE, jnp.where(future, NEG_INF, imp))
    k_sel = min(N_SLC, n_blk)
    _, sel_idx = lax.top_k(imp, k_sel)

    ksn = rms_norm(ks, k_gain[1])
    ks_blocks = ksn.reshape(B, n_blk, SLC_LEN, G, Dh).transpose(0, 3, 1, 2, 4)
    vs_blocks = vs.reshape(B, n_blk, SLC_LEN, G, Dh).transpose(0, 3, 1, 2, 4)
    kw_pad = jnp.pad(rms_norm(kw, k_gain[2]), ((0, 0), (WINDOW, 0), (0, 0), (0, 0)))
    vw_pad = jnp.pad(vw, ((0, 0), (WINDOW, 0), (0, 0), (0, 0)))
    gather = jax.vmap(jax.vmap(lambda kb, ib: kb[ib]))
    n_key_sel = k_sel * SLC_LEN
    q_blocks = jnp.moveaxis(qg.reshape(B, nb, Q_BLOCK, G, R, Dh), 1, 0)
    idx_blocks = jnp.moveaxis(sel_idx.reshape(B, G, nb, Q_BLOCK, k_sel), 2, 0)

    def block(args):
        q_blk, idx_blk, i = args
        q0 = i * Q_BLOCK
        qpos = q0 + jnp.arange(Q_BLOCK)
        k_g = gather(ks_blocks, idx_blk).reshape(B, G, Q_BLOCK, n_key_sel, Dh)
        v_g = gather(vs_blocks, idx_blk).reshape(B, G, Q_BLOCK, n_key_sel, Dh)
        kpos = (idx_blk[..., None] * SLC_LEN + jnp.arange(SLC_LEN)).reshape(
            B, G, Q_BLOCK, n_key_sel)
        dist = qpos[:, None] - kpos
        s = (jnp.einsum('bqgrd,bgqkd->bgrqk', q_blk, k_g) * scale
             + rel_bias_gqa(table_g, rel_bucket(dist)))
        p = masked_softmax(s, (dist >= 0)[:, :, None])
        o_slc = jnp.einsum('bgrqk,bgqkd->bqgrd', p, v_g)
        kw_blk = lax.dynamic_slice_in_dim(kw_pad, q0, WINDOW + Q_BLOCK, axis=1)
        vw_blk = lax.dynamic_slice_in_dim(vw_pad, q0, WINDOW + Q_BLOCK, axis=1)
        kwpos = q0 - WINDOW + jnp.arange(WINDOW + Q_BLOCK)
        wdist = qpos[:, None] - kwpos[None, :]
        wmask = (wdist >= 0) & (wdist < WINDOW) & (kwpos[None, :] >= 0)
        w_bias = bias_tab[rel_bucket(wdist)].reshape(
            Q_BLOCK, WINDOW + Q_BLOCK, G, R).transpose(2, 3, 0, 1)
        s_w = jnp.einsum('bqgrd,bkgd->bgrqk', q_blk, kw_blk) * scale + w_bias
        p_w = masked_softmax(s_w, wmask)
        o_win = jnp.einsum('bgrqk,bkgd->bqgrd', p_w, vw_blk)
        return o_slc, o_win

    o_slc, o_win = lax.map(block, (q_blocks, idx_blocks, jnp.arange(nb)))
    o_slc = jnp.moveaxis(o_slc, 0, 1).reshape(B, S, H, Dh)
    o_win = jnp.moveaxis(o_win, 0, 1).reshape(B, S, H, Dh)
    g = jax.nn.sigmoid(gate_logits.astype(F32).reshape(B, S, 3, H))[..., None]
    out = g[:, :, 0] * o_cmp + g[:, :, 1] * o_slc + g[:, :, 2] * o_win
    return out.astype(dtype)


def diff_attention(q, k, v, bias_tab, q_gain, k_gain, lam_params, out_gain, lam_init):
    dtype = v.dtype
    B, S, H, _, dq = q.shape
    nb = S // Q_BLOCK
    scale = dq ** -0.5
    qn = rms_norm(q.astype(F32), q_gain)
    kn = rms_norm(k.astype(F32), k_gain)
    vf = v.astype(F32)
    lp = lam_params.astype(F32)
    lam = jnp.exp(jnp.sum(lp[0] * lp[1])) - jnp.exp(jnp.sum(lp[2] * lp[3])) + lam_init
    key_pos = jnp.arange(S)
    qb = jnp.moveaxis(qn.reshape(B, nb, Q_BLOCK, H, 2, dq), 1, 0)

    def block(args):
        q_blk, i = args
        qpos = i * Q_BLOCK + jnp.arange(Q_BLOCK)
        dist = qpos[:, None] - key_pos[None, :]
        bias = bias_tab[rel_bucket(dist)].transpose(2, 0, 1)
        s = jnp.einsum('bqhcd,bshcd->bchqs', q_blk, kn) * scale + bias[None, None]
        p = masked_softmax(s, dist >= 0)
        a = p[:, 0] - lam * p[:, 1]
        return jnp.einsum('bhqs,bshd->bqhd', a, vf)

    o = lax.map(block, (qb, jnp.arange(nb)))
    o = jnp.moveaxis(o, 0, 1).reshape(B, S, H, -1)
    return (rms_norm(o, out_gain) * (1.0 - lam_init)).astype(dtype)


def hybrid_mixer(h, w_in, w_branch, w_out, rel_bias, cmp_pos, cmp_w1, cmp_w2,
                 nsa_q_g, nsa_k_g, diff_q_g, diff_k_g, diff_lam, diff_out_g, lam_init):
    B, S, D = h.shape
    cuts = [int(i) for i in np.cumsum(IN_SPLITS)[:-1]]
    (q_sb, k_sb, v_sb, q_ns, kc, vc, ks, vs, kw, vw, g_ns,
     q_df, k_df, v_df, g_merge) = jnp.split(h @ w_in, cuts, axis=-1)
    hd = lambda a, n: a.reshape(B, S, n, HEAD_DIM)
    o_sb = stick_breaking_attention(hd(q_sb, N_SB_HEADS), hd(k_sb, N_SB_HEADS),
                                    hd(v_sb, N_SB_HEADS))
    kvh = lambda a: hd(a, NSA_KV_HEADS)
    o_ns = nsa_attention(hd(q_ns, N_NSA_HEADS), kvh(kc), kvh(vc), kvh(ks), kvh(vs),
                         kvh(kw), kvh(vw), g_ns, rel_bias[:, :N_NSA_HEADS],
                         cmp_pos, cmp_w1, cmp_w2, nsa_q_g, nsa_k_g)
    o_df = diff_attention(q_df.reshape(B, S, N_DIFF_HEADS, 2, DIFF_QK_DIM),
                          k_df.reshape(B, S, N_DIFF_HEADS, 2, DIFF_QK_DIM),
                          hd(v_df, N_DIFF_HEADS), rel_bias[:, N_NSA_HEADS:],
                          diff_q_g, diff_k_g, diff_lam, diff_out_g, lam_init)
    o_stack = jnp.stack([o_sb.reshape(B, S, -1), o_ns.reshape(B, S, -1),
                         o_df.reshape(B, S, -1)], axis=2)
    y = jnp.einsum('bsnk,nkd->bsnd', o_stack, w_branch)
    gates = jax.nn.sigmoid(g_merge.reshape(B, S, N_BRANCHES, D))
    merged = jnp.sum(gates * y, axis=2)
    return merged @ w_out


def grouped_moe(h, w_router, router_bias, w_gate, w_up, w_down):
    B, S, D = h.shape
    tok = h.reshape(-1, D)
    aff = jax.nn.sigmoid((tok @ w_router).astype(F32))
    biased = aff + router_bias.astype(F32)
    grouped = biased.reshape(-1, N_EXPERT_GROUPS, EXPERTS_PER_GROUP)
    group_score = jnp.sum(lax.top_k(grouped, TOP_K)[0], axis=-1)
    _, gidx = lax.top_k(group_score, 1)
    gmask = jnp.repeat(jax.nn.one_hot(gidx[:, 0], N_EXPERT_GROUPS, dtype=F32),
                       EXPERTS_PER_GROUP, axis=-1) > 0
    _, eidx = lax.top_k(jnp.where(gmask, biased, NEG_INF), TOP_K)
    w = jnp.take_along_axis(aff, eidx, axis=-1)
    w = w / jnp.sum(w, axis=-1, keepdims=True)
    combine = jnp.sum(jax.nn.one_hot(eidx, N_EXPERTS, dtype=F32) * w[..., None],
                      axis=1).astype(tok.dtype)
    out = jnp.zeros_like(tok)
    for e in range(N_EXPERTS):
        he = jax.nn.silu(tok @ w_gate[e]) * (tok @ w_up[e])
        out = out + combine[:, e:e + 1] * (he @ w_down[e])
    return out.reshape(B, S, D)


def setup_inputs(seed: int = 0) -> dict:
    key = jax.random.key(seed)
    ks = jax.random.split(key, 26)
    nrm = lambda k, shape, s: jax.random.normal(k, shape, F32) * s
    D, E, F, Dh, L = D_MODEL, N_EXPERTS, D_EXPERT, HEAD_DIM, CMP_LEN
    return {
        "x": nrm(ks[0], (BATCH, SEQ, D), 1.0),
        "c": nrm(ks[1], (BATCH, D), 1.0),
        "rel_bias": nrm(ks[2], (REL_BUCKETS, N_NSA_HEADS + N_DIFF_HEADS), 0.5),
        "w_router": nrm(ks[3], (D, E), D ** -0.5),
        "router_bias": nrm(ks[4], (E,), 0.01),
        "w_ada": nrm(ks[5], (DEPTH, D, 6 * D), 0.5 * D ** -0.5),
        "b_ada": nrm(ks[6], (DEPTH, 6 * D), 0.02),
        "norm_mix": 1.0 + nrm(ks[7], (DEPTH, D), 0.02),
        "norm_ffn": 1.0 + nrm(ks[8], (DEPTH, D), 0.02),
        "w_in": nrm(ks[9], (DEPTH, D, D_IN), D ** -0.5),
        "nsa_cmp_pos": nrm(ks[10], (DEPTH, 2, L, Dh), 0.1),
        "nsa_cmp_w1": nrm(ks[11], (DEPTH, 2, L * Dh, Dh), (L * Dh) ** -0.5),
        "nsa_cmp_w2": nrm(ks[12], (DEPTH, 2, Dh, Dh), Dh ** -0.5),
        "nsa_q_norm": 1.0 + nrm(ks[13], (DEPTH, Dh), 0.02),
        "nsa_k_norm": 1.0 + nrm(ks[14], (DEPTH, 3, Dh), 0.02),
        "diff_q_norm": 1.0 + nrm(ks[15], (DEPTH, DIFF_QK_DIM), 0.02),
        "diff_k_norm": 1.0 + nrm(ks[16], (DEPTH, DIFF_QK_DIM), 0.02),
        "diff_lambda": nrm(ks[17], (DEPTH, 4, DIFF_QK_DIM), 0.1),
        "diff_out_norm": 1.0 + nrm(ks[18], (DEPTH, Dh), 0.02),
        "w_branch": nrm(ks[19], (DEPTH, N_BRANCHES, BRANCH_WIDTH, D), BRANCH_WIDTH ** -0.5),
        "w_out": nrm(ks[20], (DEPTH, D, D), D ** -0.5),
        "w_exp_gate": nrm(ks[21], (DEPTH, E, D, F), D ** -0.5),
        "w_exp_up": nrm(ks[22], (DEPTH, E, D, F), D ** -0.5),
        "w_exp_down": nrm(ks[23], (DEPTH, E, F, D), F ** -0.5),
    }


def reference(x, c, rel_bias, w_router, router_bias, w_ada, b_ada, norm_mix, norm_ffn,
              w_in, nsa_cmp_pos, nsa_cmp_w1, nsa_cmp_w2, nsa_q_norm, nsa_k_norm,
              diff_q_norm, diff_k_norm, diff_lambda, diff_out_norm, w_branch, w_out,
              w_exp_gate, w_exp_up, w_exp_down):
    c_act = jax.nn.silu(c)
    for layer in range(DEPTH):
        lam_init = 0.8 - 0.6 * math.exp(-0.3 * layer)
        mod = c_act @ w_ada[layer] + b_ada[layer]
        sh1, sc1, g1, sh2, sc2, g2 = jnp.split(mod[:, None, :], 6, axis=-1)
        h = rms_norm(x, norm_mix[layer]) * (1 + sc1) + sh1
        x = x + g1 * hybrid_mixer(h, w_in[layer], w_branch[layer], w_out[layer], rel_bias,
                                  nsa_cmp_pos[layer], nsa_cmp_w1[layer], nsa_cmp_w2[layer],
                                  nsa_q_norm[layer], nsa_k_norm[layer], diff_q_norm[layer],
                                  diff_k_norm[layer], diff_lambda[layer],
                                  diff_out_norm[layer], lam_init)
        h = rms_norm(x, norm_ffn[layer]) * (1 + sc2) + sh2
        x = x + g2 * grouped_moe(h, w_router, router_bias, w_exp_gate[layer],
                                 w_exp_up[layer], w_exp_down[layer])
    return x
```

```python
import functools
import math

import numpy as np
import jax
import jax.numpy as jnp
from jax import lax
from jax.experimental import pallas as pl
from jax.experimental.pallas import tpu as pltpu

F32 = jnp.float32
BF16 = jnp.bfloat16

HEAD_DIM = 128
N_HEADS = 8
KV_HEADS = 2
Q_PER_KV = N_HEADS // KV_HEADS
DIFF_QK = HEAD_DIM // 2
BRANCH_W = N_HEADS * HEAD_DIM
KV_W = KV_HEADS * HEAD_DIM
CMP_LEN = 32
CMP_STRIDE = 16
SLC_LEN = 64
N_SLC = 16
N_LOCAL_SLC = 2
WINDOW = 512
REL_BUCKETS = 32
REL_MAX_DIST = 128
N_EXPERTS = 16
N_GROUPS = 4
EXP_PER_GROUP = N_EXPERTS // N_GROUPS
NORM_EPS = 1e-6
NEG_INF = -1e30
MASKED_BELOW = -1e29
FORCE_SCORE = 1e9
LANES = 128
SUBLANES = 8
VMEM_CAP = 56 * 1024 * 1024
SB_EXIT = 100.0

_C_SB = 0
_C_NSQ = 3 * BRANCH_W
_C_NSKV = _C_NSQ + BRANCH_W
_C_GNS = _C_NSKV + 6 * KV_W
_C_DF = _C_GNS + 3 * N_HEADS
_C_MERGE = _C_DF + 3 * BRANCH_W


def _cparams(sem, vmem_bytes):
    lim = int(min(VMEM_CAP, max(vmem_bytes * 5 // 4 + (2 << 20), 16 << 20)))
    return pltpu.CompilerParams(dimension_semantics=sem, vmem_limit_bytes=lim)


def _dot(a, b):
    return jnp.dot(a, b, preferred_element_type=F32)


def _dot_nt(a, b):
    return lax.dot_general(a, b, (((1,), (1,)), ((), ())), preferred_element_type=F32)


def _split_bf16(x):
    hi = x.astype(BF16)
    lo = (x - hi.astype(F32)).astype(BF16)
    return hi, lo


def _ada_kernel(cb_ref, w_ref, b_ref, o_ref, ca_sc, *, nb, rows_per_iter):
    k_dim, tn = w_ref.shape

    @pl.when((pl.program_id(0) == 0) & (pl.program_id(1) == 0))
    def _():
        cb = cb_ref[...]
        ca_sc[...] = cb * jax.nn.sigmoid(cb)

    reps = tn // LANES
    groups = rows_per_iter // SUBLANES

    def body(i, accs):
        r0 = pl.multiple_of(i * rows_per_iter, rows_per_iter)
        w3 = w_ref[pl.ds(r0, rows_per_iter), :].reshape(groups, SUBLANES, tn)
        new = []
        for b in range(nb):
            ca = ca_sc[b, pl.ds(r0, rows_per_iter), :].reshape(groups, SUBLANES, LANES)
            ca = jnp.concatenate([ca] * reps, axis=2)
            new.append(accs[b] + jnp.sum(w3 * ca, axis=0))
        return tuple(new)

    accs = lax.fori_loop(0, k_dim // rows_per_iter, body,
                         tuple(jnp.zeros((SUBLANES, tn), F32) for _ in range(nb)))
    rows = [jnp.sum(a, axis=0, keepdims=True) + b_ref[...] for a in accs]
    rows.append(jnp.zeros((SUBLANES - nb, tn), F32))
    o_ref[...] = jnp.concatenate(rows, axis=0)


def _adaln(c, w_ada, b_ada):
    depth, d, n = w_ada.shape
    nb = c.shape[0]
    tn = 512
    cb = jnp.broadcast_to(c[:, :, None], (nb, d, LANES))
    vmem = 2 * d * tn * 4 + 3 * nb * d * LANES * 4
    return pl.pallas_call(
        functools.partial(_ada_kernel, nb=nb, rows_per_iter=64),
        grid=(depth, n // tn),
        in_specs=[pl.BlockSpec((nb, d, LANES), lambda l, j: (0, 0, 0)),
                  pl.BlockSpec((None, d, tn), lambda l, j: (l, 0, j)),
                  pl.BlockSpec((None, 1, tn), lambda l, j: (l, 0, j))],
        out_specs=pl.BlockSpec((None, SUBLANES, tn), lambda l, j: (l, 0, j)),
        out_shape=jax.ShapeDtypeStruct((depth, SUBLANES, n), F32),
        scratch_shapes=[pltpu.VMEM((nb, d, LANES), F32)],
        compiler_params=_cparams(("arbitrary", "arbitrary"), vmem),
        name="adaln_mod",
    )(cb, w_ada, b_ada.reshape(depth, 1, n))


def _modulated_norm(x_ref, gain_ref, sc_ref, sh_ref, tiles_per_batch):
    b = pl.program_id(0) // tiles_per_batch
    x = x_ref[...]
    ms = jnp.mean(x * x, axis=-1, keepdims=True)
    y = x * lax.rsqrt(ms + NORM_EPS) * gain_ref[...]
    return y * (1.0 + sc_ref[pl.ds(b, 1), :]) + sh_ref[pl.ds(b, 1), :]


def _norm_kernel(x_ref, gain_ref, sc_ref, sh_ref, o_ref, *, tiles_per_batch):
    o_ref[...] = _modulated_norm(x_ref, gain_ref, sc_ref, sh_ref, tiles_per_batch).astype(o_ref.dtype)


def _norm_mod(x2, gain, mod_l, sh_chunk, sc_chunk, seq):
    t, d = x2.shape
    tm = 256
    return pl.pallas_call(
        functools.partial(_norm_kernel, tiles_per_batch=seq // tm),
        grid=(t // tm,),
        in_specs=[pl.BlockSpec((tm, d), lambda i: (i, 0)),
                  pl.BlockSpec((1, d), lambda i: (0, 0)),
                  pl.BlockSpec((SUBLANES, d), lambda i: (0, sc_chunk)),
                  pl.BlockSpec((SUBLANES, d), lambda i: (0, sh_chunk))],
        out_specs=pl.BlockSpec((tm, d), lambda i: (i, 0)),
        out_shape=jax.ShapeDtypeStruct((t, d), BF16),
        compiler_params=_cparams(("arbitrary",), 2 * tm * d * 6 + 4 * SUBLANES * d * 4),
        name="norm_mix",
    )(x2, gain.reshape(1, d), mod_l, mod_l)


def _norm_router_kernel(x_ref, gain_ref, sc_ref, sh_ref, wr_ref, rb_ref, h_ref, route_ref,
                        *, tiles_per_batch):
    h = _modulated_norm(x_ref, gain_ref, sc_ref, sh_ref, tiles_per_batch)
    h_ref[...] = h
    h_hi, h_lo = _split_bf16(h)
    w_hi, w_lo = _split_bf16(wr_ref[...])
    logits = _dot(h_hi, w_hi) + _dot(h_hi, w_lo) + _dot(h_lo, w_hi)
    aff = jax.nn.sigmoid(logits)
    biased = aff + rb_ref[...]
    lane_i = lax.broadcasted_iota(jnp.int32, aff.shape, 1)
    lane = lane_i.astype(F32)
    low = jnp.float32(-3e38)
    big = jnp.float32(1 << 20)
    best = None
    for gi in range(N_GROUPS):
        in_group = (lane_i // EXP_PER_GROUP) == gi
        v = jnp.where(in_group, biased, low)
        m1 = jnp.max(v, axis=-1, keepdims=True)
        i1 = jnp.min(jnp.where(v == m1, lane, big), axis=-1, keepdims=True)
        v2 = jnp.where(lane == i1, low, v)
        m2 = jnp.max(v2, axis=-1, keepdims=True)
        i2 = jnp.min(jnp.where(v2 == m2, lane, big), axis=-1, keepdims=True)
        score = m1 + m2
        if best is None:
            best = (score, i1, i2)
        else:
            better = score > best[0]
            best = (jnp.where(better, score, best[0]), jnp.where(better, i1, best[1]),
                    jnp.where(better, i2, best[2]))
    _, e1, e2 = best
    a1 = jnp.sum(jnp.where(lane == e1, aff, 0.0), axis=-1, keepdims=True)
    a2 = jnp.sum(jnp.where(lane == e2, aff, 0.0), axis=-1, keepdims=True)
    den = a1 + a2
    out = jnp.where(lane_i == 0, e1,
                    jnp.where(lane_i == 1, e2,
                              jnp.where(lane_i == 2, a1 / den, jnp.where(lane_i == 3, a2 / den, 0.0))))
    route_ref[...] = out


def _norm_router(x2, gain, mod_l, sh_chunk, sc_chunk, seq, w_router, router_bias):
    t, d = x2.shape
    tm = 256
    wr = jnp.pad(w_router, ((0, 0), (0, LANES - N_EXPERTS)))
    rb = jnp.pad(router_bias, (0, LANES - N_EXPERTS)).reshape(1, LANES)
    return pl.pallas_call(
        functools.partial(_norm_router_kernel, tiles_per_batch=seq // tm),
        grid=(t // tm,),
        in_specs=[pl.BlockSpec((tm, d), lambda i: (i, 0)),
                  pl.BlockSpec((1, d), lambda i: (0, 0)),
                  pl.BlockSpec((SUBLANES, d), lambda i: (0, sc_chunk)),
                  pl.BlockSpec((SUBLANES, d), lambda i: (0, sh_chunk)),
                  pl.BlockSpec((d, LANES), lambda i: (0, 0)),
                  pl.BlockSpec((1, LANES), lambda i: (0, 0))],
        out_specs=[pl.BlockSpec((tm, d), lambda i: (i, 0)),
                   pl.BlockSpec((tm, LANES), lambda i: (i, 0))],
        out_shape=[jax.ShapeDtypeStruct((t, d), F32),
                   jax.ShapeDtypeStruct((t, LANES), F32)],
        compiler_params=_cparams(("arbitrary",), 2 * tm * d * 8 + 4 * d * LANES * 4 + 8 * tm * d),
        name="norm_ffn_router",
    )(x2, gain.reshape(1, d), mod_l, mod_l, wr, rb)


def _mm_kernel(a_ref, b_ref, *rest, epilogue, n_extra, tiles_per_batch):
    extras = rest[:n_extra]
    o_ref = rest[n_extra]
    acc = _dot(a_ref[...], b_ref[...])
    o_ref[...] = epilogue(acc, extras, pl.program_id(0) // tiles_per_batch).astype(o_ref.dtype)


def _matmul(a, b, extras, epilogue, out_dtype, *, tm, tn, name, seq=None):
    m, k = a.shape
    n = b.shape[1]
    tm = min(tm, m)
    tn = min(tn, n)
    assert m % tm == 0 and n % tn == 0
    in_specs = [pl.BlockSpec((tm, k), lambda i, j: (i, 0)),
                pl.BlockSpec((k, tn), lambda i, j: (0, j))]
    vmem = 2 * (tm * k + k * tn) * 2 + 2 * tm * tn * jnp.dtype(out_dtype).itemsize + 2 * tm * tn * 4
    args = []
    for arr, kind in extras:
        if kind == "col":
            in_specs.append(pl.BlockSpec((1, tn), lambda i, j: (0, j)))
            vmem += 2 * SUBLANES * tn * 4
        elif kind == "tile":
            in_specs.append(pl.BlockSpec((tm, tn), lambda i, j: (i, j)))
            vmem += 2 * tm * tn * arr.dtype.itemsize
        else:
            in_specs.append(pl.BlockSpec((SUBLANES, tn), lambda i, j: (0, j)))
            vmem += 2 * SUBLANES * tn * 4
        args.append(arr)
    tiles_per_batch = (seq // tm) if seq else 1
    return pl.pallas_call(
        functools.partial(_mm_kernel, epilogue=epilogue, n_extra=len(extras),
                          tiles_per_batch=tiles_per_batch),
        grid=(m // tm, n // tn),
        in_specs=in_specs,
        out_specs=pl.BlockSpec((tm, tn), lambda i, j: (i, j)),
        out_shape=jax.ShapeDtypeStruct((m, n), out_dtype),
        compiler_params=_cparams(("arbitrary", "arbitrary"), vmem),
        name=name,
    )(a, b, *args)


def _ep_scale(acc, extras, b):
    return acc * extras[0][...]


def _ep_sigmoid(acc, extras, b):
    return jax.nn.sigmoid(acc)


def _ep_norm(acc, extras, b, *, width):
    nf_ref, cs_ref = extras
    outs = []
    for c in range(acc.shape[1] // LANES):
        sl = slice(c * LANES, (c + 1) * LANES)
        blk = acc[:, sl]
        sq = blk * blk
        if width == LANES:
            ms = jnp.mean(sq, axis=-1, keepdims=True)
        else:
            lane = lax.broadcasted_iota(jnp.int32, blk.shape, 1)
            first = lane < width
            lo = jnp.sum(jnp.where(first, sq, 0.0), axis=-1, keepdims=True) / width
            hi = jnp.sum(jnp.where(first, 0.0, sq), axis=-1, keepdims=True) / width
            ms = jnp.where(first, lo, hi)
        inv = lax.rsqrt(ms + NORM_EPS)
        f = jnp.where(nf_ref[:, sl] > 0.0, inv, 1.0)
        outs.append(blk * f * cs_ref[:, sl])
    return jnp.concatenate(outs, axis=1)


def _ep_residual(acc, extras, b):
    x_ref, g_ref = extras
    return x_ref[...] + g_ref[pl.ds(b, 1), :] * acc


def _bucket_np(dist):
    n = np.maximum(dist, 0)
    max_exact = REL_BUCKETS // 2
    nf = np.maximum(n, max_exact).astype(np.float32)
    large = max_exact + (np.log(nf / np.float32(max_exact))
                         / np.float32(math.log(REL_MAX_DIST / max_exact))
                         * np.float32(REL_BUCKETS - max_exact)).astype(np.int32)
    large = np.minimum(large, REL_BUCKETS - 1)
    return np.where(n < max_exact, n, large).astype(np.int32)


def _bias_tiles(table, dist, valid):
    vals = table[jnp.asarray(_bucket_np(dist))]
    vals = jnp.where(jnp.asarray(valid)[..., None], vals, NEG_INF)
    return jnp.moveaxis(vals, -1, 0)


def _toeplitz(offs, nr, nc):
    r = np.arange(nr)[:, None]
    c = np.arange(nc)[None, :]
    return np.stack([o + r - c for o in offs])


def _stack_group_heads(tiles):
    nh, k, nr, nc = tiles.shape
    t = tiles.reshape(KV_HEADS, Q_PER_KV, k, nr, nc)
    return jnp.transpose(t, (0, 2, 1, 3, 4)).reshape(KV_HEADS, k, Q_PER_KV * nr, nc)


def _sb_kernel(q_ref, k_ref, v_ref, o_ref, *, tq):
    qi = pl.program_id(2)
    q = q_ref[...]
    row = lax.broadcasted_iota(jnp.int32, (tq, tq), 0)
    col = lax.broadcasted_iota(jnp.int32, (tq, tq), 1)
    later = jnp.where(row > col, 1.0, 0.0).astype(BF16)
    causal = col < row

    def step(kj, carry, acc, diag):
        k0 = pl.multiple_of(kj * tq, tq)
        k = k_ref[pl.ds(k0, tq), :]
        v = v_ref[pl.ds(k0, tq), :]
        z = _dot_nt(q, k)
        log_succ = jnp.minimum(z, 0.0) - jnp.log1p(jnp.exp(-jnp.abs(z)))
        log_fail = log_succ - z
        if diag:
            log_fail = jnp.where(causal, log_fail, 0.0)
        hi, lo = _split_bf16(log_fail)
        suffix = _dot(hi, later) + _dot(lo, later)
        a = jnp.exp(log_succ + suffix + carry)
        if diag:
            a = jnp.where(causal, a, 0.0)
        acc = acc + _dot(a.astype(BF16), v)
        carry = carry + jnp.sum(log_fail, axis=-1, keepdims=True)
        return carry, acc

    carry, acc = step(qi, jnp.zeros((tq, 1), F32), jnp.zeros((tq, HEAD_DIM), F32), True)

    def cond(state):
        kj, carry, _ = state
        return (kj >= 0) & (jnp.max(carry) > -SB_EXIT)

    def body(state):
        kj, carry, acc = state
        carry, acc = step(kj, carry, acc, False)
        return kj - 1, carry, acc

    _, _, acc = lax.while_loop(cond, body, (qi - 1, carry, acc))
    o_ref[...] = acc.astype(o_ref.dtype)


def _sb_attention(p_sb, batch, seq):
    tq = min(256, seq)
    nq = seq // tq
    t = batch * seq
    vmem = 4 * seq * HEAD_DIM * 2 + 4 * tq * HEAD_DIM * 2 + 16 * tq * tq * 4
    return pl.pallas_call(
        functools.partial(_sb_kernel, tq=tq),
        grid=(batch, N_HEADS, nq),
        in_specs=[pl.BlockSpec((tq, HEAD_DIM), lambda b, h, i: (b * nq + i, h)),
                  pl.BlockSpec((seq, HEAD_DIM), lambda b, h, i: (b, N_HEADS + h)),
                  pl.BlockSpec((seq, HEAD_DIM), lambda b, h, i: (b, 2 * N_HEADS + h))],
        out_specs=pl.BlockSpec((tq, HEAD_DIM), lambda b, h, i: (b * nq + i, h)),
        out_shape=jax.ShapeDtypeStruct((t, BRANCH_W), BF16),
        compiler_params=_cparams(("arbitrary",) * 3, vmem),
        name="sb_attention",
    )(p_sb, p_sb, p_sb)


def _flash_init(m_sc, l_sc, acc_sc):
    m_sc[...] = jnp.full(m_sc.shape, NEG_INF, F32)
    l_sc[...] = jnp.zeros(l_sc.shape, F32)
    acc_sc[...] = jnp.zeros(acc_sc.shape, F32)


def _flash_update(s, v, m_sc, l_sc, acc_sc):
    valid = s > MASKED_BELOW
    s = jnp.where(valid, s, NEG_INF)
    m_old = m_sc[...]
    m_new = jnp.maximum(m_old, jnp.max(s, axis=-1, keepdims=True))
    p = jnp.where(valid, jnp.exp(s - m_new), 0.0)
    alpha = jnp.exp(m_old - m_new)
    l_sc[...] = alpha * l_sc[...] + jnp.sum(p, axis=-1, keepdims=True)
    acc_sc[...] = alpha * acc_sc[...] + _dot(p.astype(BF16), v)
    m_sc[...] = m_new


def _flash_result(l_sc, acc_sc):
    return acc_sc[...] / jnp.maximum(l_sc[...], 1.0)


def _compress_kernel(x_ref, p_ref, w1a_ref, w1b_ref, w2_ref, gain_ref, o_ref):
    nc = x_ref.shape[0]
    is_key = pl.program_id(1) < KV_HEADS
    x = x_ref[...].astype(F32)
    first = _dot((x + p_ref[0:1, :]).astype(BF16), w1a_ref[...])
    second = _dot((x + p_ref[1:2, :]).astype(BF16), w1b_ref[...])
    pre = first + pltpu.roll(second, nc - 1, 0)
    hid = pre * jax.nn.sigmoid(pre)
    out = _dot(hid.astype(BF16), w2_ref[...])
    ms = jnp.mean(out * out, axis=-1, keepdims=True)
    normed = out * lax.rsqrt(ms + NORM_EPS) * gain_ref[...]
    o_ref[...] = jnp.where(is_key, normed, out).astype(o_ref.dtype)


def _compress(p_nskv, cmp_pos, cmp_w1, cmp_w2, k_gain0, batch, seq):
    nc = seq // CMP_STRIDE
    half = CMP_STRIDE * HEAD_DIM
    x = p_nskv[:, :2 * KV_W].reshape(batch, nc, CMP_STRIDE, 2 * KV_HEADS, HEAD_DIM)
    x = jnp.transpose(x, (0, 3, 1, 2, 4)).reshape(batch, 2 * KV_HEADS, nc, half)
    pos = cmp_pos.reshape(2, 2, half)
    w1 = cmp_w1.astype(BF16).reshape(2, 2, half, HEAD_DIM)
    w2 = cmp_w2.astype(BF16)
    return pl.pallas_call(
        _compress_kernel,
        grid=(batch, 2 * KV_HEADS),
        in_specs=[pl.BlockSpec((None, None, nc, half), lambda b, j: (b, j, 0, 0)),
                  pl.BlockSpec((None, 2, half), lambda b, j: (j // KV_HEADS, 0, 0)),
                  pl.BlockSpec((None, None, half, HEAD_DIM), lambda b, j: (j // KV_HEADS, 0, 0, 0)),
                  pl.BlockSpec((None, None, half, HEAD_DIM), lambda b, j: (j // KV_HEADS, 1, 0, 0)),
                  pl.BlockSpec((None, HEAD_DIM, HEAD_DIM), lambda b, j: (j // KV_HEADS, 0, 0)),
                  pl.BlockSpec((1, HEAD_DIM), lambda b, j: (0, 0))],
        out_specs=pl.BlockSpec((None, None, nc, HEAD_DIM), lambda b, j: (b, j, 0, 0)),
        out_shape=jax.ShapeDtypeStruct((batch, 2 * KV_HEADS, nc, HEAD_DIM), BF16),
        compiler_params=_cparams(("arbitrary", "arbitrary"), 8 * nc * half * 2 + 8 * half * HEAD_DIM * 2),
        name="nsa_compress",
    )(x, pos, w1, w1, w2, k_gain0.reshape(1, HEAD_DIM))


def _nsa_kernel(q_ref, kcm_ref, vcm_ref, ks_ref, vs_ref, kw_ref, vw_ref, gate_ref,
                pt_ref, bt_ref, wt_ref, o_ref, m_sc, l_sc, acc_sc, *, tq, tk, n_blk, k_sel):
    qi = pl.program_id(2)
    q0 = qi * tq
    rows = Q_PER_KV * tq
    nc = kcm_ref.shape[0]
    q4 = jnp.concatenate([q_ref[:, r * HEAD_DIM:(r + 1) * HEAD_DIM] for r in range(Q_PER_KV)], axis=0)

    per_tile = tq // CMP_STRIDE
    lead = 2 * per_tile
    shift = lax.rem(qi * per_tile + (nc - lead), nc)
    cb = pltpu.roll(pt_ref[0], shift, 1)
    rel = lax.broadcasted_iota(jnp.int32, (rows, nc), 1) - qi * per_tile
    cb = jnp.where(rel < -lead, pt_ref[1], jnp.where(rel >= per_tile, NEG_INF, cb))
    s = _dot_nt(q4, kcm_ref[...]) + cb
    valid = s > MASKED_BELOW
    s = jnp.where(valid, s, NEG_INF)
    p = jnp.where(valid, jnp.exp(s - jnp.max(s, axis=-1, keepdims=True)), 0.0)
    p = p / jnp.maximum(jnp.sum(p, axis=-1, keepdims=True), 1.0)
    o_cmp = _dot(p.astype(BF16), vcm_ref[...])

    psum = p[0:tq]
    for r in range(1, Q_PER_KV):
        psum = psum + p[r * tq:(r + 1) * tq]
    n_id = lax.broadcasted_iota(jnp.int32, (nc, LANES), 0)
    j_id = lax.broadcasted_iota(jnp.int32, (nc, LANES), 1)
    overlap = ((CMP_STRIDE * n_id < SLC_LEN * j_id + SLC_LEN)
               & (CMP_STRIDE * n_id + CMP_LEN > SLC_LEN * j_id) & (j_id < n_blk))
    overlap = jnp.where(overlap, 1.0, 0.0).astype(BF16)
    p_hi, p_lo = _split_bf16(psum)
    imp = _dot(p_hi, overlap) + _dot(p_lo, overlap)
    blk = lax.broadcasted_iota(jnp.int32, (tq, LANES), 1)
    cur = (q0 + lax.broadcasted_iota(jnp.int32, (tq, LANES), 0)) // SLC_LEN
    forced = (blk == 0) | ((blk <= cur) & (blk > cur - N_LOCAL_SLC))
    imp = jnp.where(forced, FORCE_SCORE, jnp.where(blk > cur, NEG_INF, imp))
    imp = jnp.where(blk < n_blk, imp, -3e38)
    rank = jnp.zeros((tq, LANES), F32)
    for j in range(n_blk):
        cj = imp[:, j:j + 1]
        ahead = jnp.where(blk > j, jnp.where(cj >= imp, 1.0, 0.0), jnp.where(cj > imp, 1.0, 0.0))
        rank = rank + ahead
    sel = jnp.where((rank < k_sel) & (blk < n_blk), 1.0, 0.0).astype(BF16)
    sel4 = jnp.concatenate([sel] * Q_PER_KV, axis=0)

    _flash_init(m_sc, l_sc, acc_sc)
    e_row = lax.broadcasted_iota(jnp.int32, (LANES, tk), 0)
    e_col = lax.broadcasted_iota(jnp.int32, (LANES, tk), 1)

    def slc_body(kj, _):
        k0 = pl.multiple_of(kj * tk, tk)
        off_idx = jnp.minimum((q0 - k0) // tq, bt_ref.shape[0] - 1)
        bias = bt_ref[off_idx]
        expand = jnp.where((k0 + e_col) // SLC_LEN == e_row, 1.0, 0.0).astype(BF16)
        chosen = _dot(sel4, expand)
        s = _dot_nt(q4, ks_ref[pl.ds(k0, tk), :]) + bias
        s = jnp.where(chosen > 0.5, s, NEG_INF)
        _flash_update(s, vs_ref[pl.ds(k0, tk), :], m_sc, l_sc, acc_sc)
        return 0

    lax.fori_loop(0, q0 // tk + 1, slc_body, 0)
    o_slc = _flash_result(l_sc, acc_sc)

    _flash_init(m_sc, l_sc, acc_sc)

    def win_body(kj, _):
        k0 = pl.multiple_of(kj * tk, tk)
        bias = wt_ref[(q0 - k0) // tq]
        s = _dot_nt(q4, kw_ref[pl.ds(k0, tk), :]) + bias
        _flash_update(s, vw_ref[pl.ds(k0, tk), :], m_sc, l_sc, acc_sc)
        return 0

    lax.fori_loop(jnp.maximum(q0 - (WINDOW - 1), 0) // tk, q0 // tk + 1, win_body, 0)
    o_win = _flash_result(l_sc, acc_sc)

    gate = gate_ref[...]
    for r in range(Q_PER_KV):
        sl = slice(r * tq, (r + 1) * tq)
        o = (gate[:, r:r + 1] * o_cmp[sl] + gate[:, Q_PER_KV + r:Q_PER_KV + r + 1] * o_slc[sl]
             + gate[:, 2 * Q_PER_KV + r:2 * Q_PER_KV + r + 1] * o_win[sl])
        o_ref[:, r * HEAD_DIM:(r + 1) * HEAD_DIM] = o.astype(o_ref.dtype)


def _nsa_attention(p_nsq, kvcm, p_nskv, gates, rel_nsa, batch, seq):
    tq = 128
    tk = min(256, seq)
    nq = seq // tq
    nc = seq // CMP_STRIDE
    n_blk = seq // SLC_LEN
    k_sel = min(N_SLC, n_blk)
    t = batch * seq
    rows = Q_PER_KV * tq
    assert n_blk <= LANES and nc % LANES == 0 and WINDOW % tq == 0

    slc_offs = [i * tq for i in range(tk // tq + 2)]
    d = _toeplitz(slc_offs, tq, tk)
    bt = _stack_group_heads(_bias_tiles(rel_nsa, d, d >= 0))
    win_offs = [i * tq for i in range((WINDOW + tk) // tq)]
    d = _toeplitz(win_offs, tq, tk)
    wt = _stack_group_heads(_bias_tiles(rel_nsa, d, (d >= 0) & (d < WINDOW)))
    per_tile = tq // CMP_STRIDE
    i = np.arange(tq)[:, None]
    c = np.arange(nc)[None, :]
    d = i - CMP_STRIDE * (c - 2 * per_tile) - (CMP_LEN - 1)
    pat = _bias_tiles(rel_nsa, d, d >= 0)
    far = jnp.broadcast_to(rel_nsa[REL_BUCKETS - 1][:, None, None], pat.shape)
    pt = _stack_group_heads(jnp.stack([pat, far], axis=1))

    vmem = (8 * seq * HEAD_DIM * 2 + 2 * (bt.size + wt.size + pt.size) // KV_HEADS * 4
            + 24 * rows * tk * 4 + 8 * rows * nc * 4)
    return pl.pallas_call(
        functools.partial(_nsa_kernel, tq=tq, tk=tk, n_blk=n_blk, k_sel=k_sel),
        grid=(batch, KV_HEADS, nq),
        in_specs=[pl.BlockSpec((tq, Q_PER_KV * HEAD_DIM), lambda b, g, i: (b * nq + i, g)),
                  pl.BlockSpec((None, None, nc, HEAD_DIM), lambda b, g, i: (b, g, 0, 0)),
                  pl.BlockSpec((None, None, nc, HEAD_DIM), lambda b, g, i: (b, KV_HEADS + g, 0, 0)),
                  pl.BlockSpec((seq, HEAD_DIM), lambda b, g, i: (b, 2 * KV_HEADS + g)),
                  pl.BlockSpec((seq, HEAD_DIM), lambda b, g, i: (b, 3 * KV_HEADS + g)),
                  pl.BlockSpec((seq, HEAD_DIM), lambda b, g, i: (b, 4 * KV_HEADS + g)),
                  pl.BlockSpec((seq, HEAD_DIM), lambda b, g, i: (b, 5 * KV_HEADS + g)),
                  pl.BlockSpec((tq, LANES), lambda b, g, i: (b * nq + i, g)),
                  pl.BlockSpec((None,) + pt.shape[1:], lambda b, g, i: (g, 0, 0, 0)),
                  pl.BlockSpec((None,) + bt.shape[1:], lambda b, g, i: (g, 0, 0, 0)),
                  pl.BlockSpec((None,) + wt.shape[1:], lambda b, g, i: (g, 0, 0, 0))],
        out_specs=pl.BlockSpec((tq, Q_PER_KV * HEAD_DIM), lambda b, g, i: (b * nq + i, g)),
        out_shape=jax.ShapeDtypeStruct((t, BRANCH_W), BF16),
        scratch_shapes=[pltpu.VMEM((rows, 1), F32), pltpu.VMEM((rows, 1), F32),
                        pltpu.VMEM((rows, HEAD_DIM), F32)],
        compiler_params=_cparams(("arbitrary",) * 3, vmem),
        name="nsa_attention",
    )(p_nsq, kvcm, kvcm, p_nskv, p_nskv, p_nskv, p_nskv, gates, pt, bt, wt)


def _diff_kernel(q_ref, k_ref, v_ref, dt_ref, lam_ref, og_ref, o_ref, m_sc, l_sc, acc_sc,
                 *, tq, lam_init):
    qi = pl.program_id(2)
    q = q_ref[...]
    lane = lax.broadcasted_iota(jnp.int32, q.shape, 1)
    zero = jnp.zeros_like(q)
    q2 = jnp.concatenate([jnp.where(lane < DIFF_QK, q, zero), jnp.where(lane < DIFF_QK, zero, q)], axis=0)
    _flash_init(m_sc, l_sc, acc_sc)

    def body(kj, _):
        k0 = pl.multiple_of(kj * tq, tq)
        bias = dt_ref[jnp.minimum(qi - kj, dt_ref.shape[0] - 1)]
        s = _dot_nt(q2, k_ref[pl.ds(k0, tq), :]) + bias
        _flash_update(s, v_ref[pl.ds(k0, tq), :], m_sc, l_sc, acc_sc)
        return 0

    lax.fori_loop(0, qi + 1, body, 0)
    o2 = _flash_result(l_sc, acc_sc)
    lp = lam_ref[...]
    lam = (jnp.exp(jnp.sum(lp[0:1] * lp[1:2], axis=-1, keepdims=True))
           - jnp.exp(jnp.sum(lp[2:3] * lp[3:4], axis=-1, keepdims=True)) + lam_init)
    o = o2[:tq] - lam * o2[tq:]
    ms = jnp.mean(o * o, axis=-1, keepdims=True)
    o = o * lax.rsqrt(ms + NORM_EPS) * og_ref[...]
    o_ref[...] = (o * (1.0 - lam_init)).astype(o_ref.dtype)


def _diff_attention(p_df, rel_diff, lam_params, out_gain, lam_init, batch, seq):
    tq = min(256, seq)
    nq = seq // tq
    t = batch * seq
    d = _toeplitz([0, tq, 2 * tq], tq, tq)
    dt = _bias_tiles(rel_diff, d, d >= 0)
    dt = jnp.concatenate([dt, dt], axis=2)
    vmem = 4 * seq * HEAD_DIM * 2 + 2 * 3 * 2 * tq * tq * 4 + 24 * 2 * tq * tq * 4
    return pl.pallas_call(
        functools.partial(_diff_kernel, tq=tq, lam_init=lam_init),
        grid=(batch, N_HEADS, nq),
        in_specs=[pl.BlockSpec((tq, HEAD_DIM), lambda b, h, i: (b * nq + i, h)),
                  pl.BlockSpec((seq, HEAD_DIM), lambda b, h, i: (b, N_HEADS + h)),
                  pl.BlockSpec((seq, HEAD_DIM), lambda b, h, i: (b, 2 * N_HEADS + h)),
                  pl.BlockSpec((None, 3, 2 * tq, tq), lambda b, h, i: (h, 0, 0, 0)),
                  pl.BlockSpec((4, DIFF_QK), lambda b, h, i: (0, 0)),
                  pl.BlockSpec((1, HEAD_DIM), lambda b, h, i: (0, 0))],
        out_specs=pl.BlockSpec((tq, HEAD_DIM), lambda b, h, i: (b * nq + i, h)),
        out_shape=jax.ShapeDtypeStruct((t, BRANCH_W), BF16),
        scratch_shapes=[pltpu.VMEM((2 * tq, 1), F32), pltpu.VMEM((2 * tq, 1), F32),
                        pltpu.VMEM((2 * tq, HEAD_DIM), F32)],
        compiler_params=_cparams(("arbitrary",) * 3, vmem),
        name="diff_attention",
    )(p_df, p_df, p_df, dt, lam_params, out_gain.reshape(1, HEAD_DIM))


def _merge_kernel(o0_ref, o1_ref, o2_ref, w_ref, g0_ref, g1_ref, g2_ref, out_ref):
    acc = g0_ref[...].astype(F32) * _dot(o0_ref[...], w_ref[0])
    acc = acc + g1_ref[...].astype(F32) * _dot(o1_ref[...], w_ref[1])
    acc = acc + g2_ref[...].astype(F32) * _dot(o2_ref[...], w_ref[2])
    out_ref[...] = acc.astype(out_ref.dtype)


def _branch_merge(o_sb, o_ns, o_df, w_branch, gates):
    t = o_sb.shape[0]
    d = w_branch.shape[2]
    tm = min(512, t)
    tn = min(512, d)
    nj = d // tn
    o_spec = pl.BlockSpec((tm, BRANCH_W), lambda i, j: (i, 0))
    vmem = 2 * (3 * tm * BRANCH_W * 2 + 3 * BRANCH_W * tn * 2 + 3 * tm * tn * 2 + tm * tn * 2) + 4 * tm * tn * 4
    return pl.pallas_call(
        _merge_kernel,
        grid=(t // tm, nj),
        in_specs=[o_spec, o_spec, o_spec,
                  pl.BlockSpec((3, BRANCH_W, tn), lambda i, j: (0, 0, j)),
                  pl.BlockSpec((tm, tn), lambda i, j: (i, j)),
                  pl.BlockSpec((tm, tn), lambda i, j: (i, nj + j)),
                  pl.BlockSpec((tm, tn), lambda i, j: (i, 2 * nj + j))],
        out_specs=pl.BlockSpec((tm, tn), lambda i, j: (i, j)),
        out_shape=jax.ShapeDtypeStruct((t, d), BF16),
        compiler_params=_cparams(("arbitrary", "arbitrary"), vmem),
        name="branch_merge",
    )(o_sb, o_ns, o_df, w_branch, gates, gates, gates)


def _row_copy(src_ref, src_row, dst_ref, dst_row, sem):
    return pltpu.make_async_copy(src_ref.at[pl.ds(src_row, 1)], dst_ref.at[pl.ds(dst_row, 1)], sem)


def _gather_kernel(idx_ref, src_ref, o_ref, buf, sem, *, tg):
    base = pl.program_id(0) * tg

    def issue(r, _):
        _row_copy(src_ref, idx_ref[base + r], buf, r, sem).start()
        return 0

    lax.fori_loop(0, tg, issue, 0)

    def drain(r, _):
        _row_copy(src_ref, 0, buf, r, sem).wait()
        return 0

    lax.fori_loop(0, tg, drain, 0)
    o_ref[...] = buf[...].astype(o_ref.dtype)


def _gather_rows(row_token, h2, n_rows):
    d = h2.shape[1]
    tg = 256
    return pl.pallas_call(
        functools.partial(_gather_kernel, tg=tg),
        grid_spec=pltpu.PrefetchScalarGridSpec(
            num_scalar_prefetch=1,
            grid=(n_rows // tg,),
            in_specs=[pl.BlockSpec(memory_space=pl.ANY)],
            out_specs=pl.BlockSpec((tg, d), lambda i, idx: (i, 0)),
            scratch_shapes=[pltpu.VMEM((tg, d), F32), pltpu.SemaphoreType.DMA(())]),
        out_shape=jax.ShapeDtypeStruct((n_rows, d), BF16),
        compiler_params=_cparams(("arbitrary",), tg * d * 4 + 2 * tg * d * 2),
        name="moe_gather",
    )(row_token, h2)


def _expert_up_kernel(te_ref, tv_ref, x_ref, wg_ref, wu_ref, o_ref):
    t = pl.program_id(1)

    @pl.when(tv_ref[t] > 0)
    def _():
        x = x_ref[...]
        g = _dot(x, wg_ref[...])
        u = _dot(x, wu_ref[...])
        o_ref[...] = (g * jax.nn.sigmoid(g) * u).astype(o_ref.dtype)

    @pl.when(tv_ref[t] == 0)
    def _():
        o_ref[...] = jnp.zeros(o_ref.shape, o_ref.dtype)


def _expert_down_kernel(te_ref, tv_ref, h_ref, wd_ref, o_ref):
    t = pl.program_id(0)

    @pl.when(tv_ref[t] > 0)
    def _():
        o_ref[...] = _dot(h_ref[...], wd_ref[...])

    @pl.when(tv_ref[t] == 0)
    def _():
        o_ref[...] = jnp.zeros(o_ref.shape, o_ref.dtype)


def _expert_mlps(tile_expert, tile_valid, xs, w_gate, w_up, w_down, tm):
    n_rows, d = xs.shape
    f = w_gate.shape[2]
    tf = min(512, f)
    nt = n_rows // tm
    hidden = pl.pallas_call(
        _expert_up_kernel,
        grid_spec=pltpu.PrefetchScalarGridSpec(
            num_scalar_prefetch=2,
            grid=(f // tf, nt),
            in_specs=[pl.BlockSpec((tm, d), lambda j, t, te, tv: (t, 0)),
                      pl.BlockSpec((None, d, tf), lambda j, t, te, tv: (te[t], 0, j)),
                      pl.BlockSpec((None, d, tf), lambda j, t, te, tv: (te[t], 0, j))],
            out_specs=pl.BlockSpec((tm, tf), lambda j, t, te, tv: (t, j))),
        out_shape=jax.ShapeDtypeStruct((n_rows, f), BF16),
        compiler_params=_cparams(("arbitrary", "arbitrary"),
                                 2 * (tm * d * 2 + 2 * d * tf * 2 + tm * tf * 2) + 4 * tm * tf * 4),
        name="moe_gate_up",
    )(tile_expert, tile_valid, xs, w_gate, w_up)
    return pl.pallas_call(
        _expert_down_kernel,
        grid_spec=pltpu.PrefetchScalarGridSpec(
            num_scalar_prefetch=2,
            grid=(nt,),
            in_specs=[pl.BlockSpec((tm, f), lambda t, te, tv: (t, 0)),
                      pl.BlockSpec((None, f, d), lambda t, te, tv: (te[t], 0, 0))],
            out_specs=pl.BlockSpec((tm, d), lambda t, te, tv: (t, 0))),
        out_shape=jax.ShapeDtypeStruct((n_rows, d), F32),
        compiler_params=_cparams(("arbitrary",), 2 * (tm * f * 2 + f * d * 2 + tm * d * 4) + tm * d * 4),
        name="moe_down",
    )(tile_expert, tile_valid, hidden, w_down)


def _combine_kernel(pos_ref, y_ref, x_ref, g_ref, route_ref, o_ref, buf0, buf1, sem,
                    *, tc, tiles_per_batch):
    i = pl.program_id(0)
    base = i * tc

    def issue(r, _):
        _row_copy(y_ref, pos_ref[2 * (base + r)], buf0, r, sem).start()
        _row_copy(y_ref, pos_ref[2 * (base + r) + 1], buf1, r, sem).start()
        return 0

    lax.fori_loop(0, tc, issue, 0)

    def drain(r, _):
        _row_copy(y_ref, 0, buf0, r, sem).wait()
        _row_copy(y_ref, 0, buf1, r, sem).wait()
        return 0

    lax.fori_loop(0, tc, drain, 0)
    route = route_ref[...]
    moe = route[:, 2:3] * buf0[...] + route[:, 3:4] * buf1[...]
    o_ref[...] = x_ref[...] + g_ref[pl.ds(i // tiles_per_batch, 1), :] * moe


def _combine(pos, y, x2, mod_l, g_chunk, route, seq):
    t, d = x2.shape
    tc = 128
    return pl.pallas_call(
        functools.partial(_combine_kernel, tc=tc, tiles_per_batch=seq // tc),
        grid_spec=pltpu.PrefetchScalarGridSpec(
            num_scalar_prefetch=1,
            grid=(t // tc,),
            in_specs=[pl.BlockSpec(memory_space=pl.ANY),
                      pl.BlockSpec((tc, d), lambda i, p: (i, 0)),
                      pl.BlockSpec((SUBLANES, d), lambda i, p: (0, g_chunk)),
                      pl.BlockSpec((tc, LANES), lambda i, p: (i, 0))],
            out_specs=pl.BlockSpec((tc, d), lambda i, p: (i, 0)),
            scratch_shapes=[pltpu.VMEM((tc, d), F32), pltpu.VMEM((tc, d), F32),
                            pltpu.SemaphoreType.DMA(())]),
        out_shape=jax.ShapeDtypeStruct((t, d), F32),
        compiler_params=_cparams(("arbitrary",), 2 * tc * d * 4 + 4 * tc * d * 4 + 4 * tc * d * 4),
        name="moe_combine",
    )(pos, y, x2, mod_l, route)


def _routing_tables(route, tm):
    t = route.shape[0]
    experts = route[:, :2].astype(jnp.int32).reshape(-1)
    onehot = (experts[:, None] == jnp.arange(N_EXPERTS)[None, :]).astype(jnp.int32)
    csum = jnp.cumsum(onehot, axis=0)
    rank = jnp.sum((csum - onehot) * onehot, axis=1)
    counts = csum[-1]
    padded = ((counts + tm - 1) // tm) * tm
    ends = jnp.cumsum(padded)
    starts = ends - padded
    dest = starts[experts] + rank
    n_rows = 2 * t + N_EXPERTS * tm
    row_token = jnp.zeros((n_rows,), jnp.int32).at[dest].set(jnp.arange(2 * t, dtype=jnp.int32) // 2)
    tile_start = jnp.arange(n_rows // tm, dtype=jnp.int32) * tm
    tile_expert = jnp.minimum(jnp.sum((tile_start[:, None] >= ends[None, :]).astype(jnp.int32), axis=1),
                              N_EXPERTS - 1)
    tile_valid = (tile_start < ends[-1]).astype(jnp.int32)
    return row_token, dest.astype(jnp.int32), tile_expert, tile_valid, n_rows


def _tiled(vec, reps):
    return jnp.tile(vec.astype(F32), reps)


def _mixer(h, seq, batch, w_in, w_branch, w_out, rel_bias, cmp_pos, cmp_w1, cmp_w2,
           nsa_q_g, nsa_k_g, diff_q_g, diff_k_g, diff_lam, diff_out_g, lam_init, x2, mod_l, d):
    scale = HEAD_DIM ** -0.5
    ones = lambda n: jnp.ones((n,), F32)
    zeros = lambda n: jnp.zeros((n,), F32)
    wb = lambda lo, hi: w_in[:, lo:hi].astype(BF16)

    cs = jnp.concatenate([jnp.full((BRANCH_W,), scale, F32), ones(2 * BRANCH_W)]).reshape(1, -1)
    p_sb = _matmul(h, wb(_C_SB, _C_NSQ), [(cs, "col")], _ep_scale, BF16, tm=1024, tn=512, name="proj_sb")
    nf = ones(BRANCH_W).reshape(1, -1)
    cs = (_tiled(nsa_q_g, N_HEADS) * scale).reshape(1, -1)
    p_nsq = _matmul(h, wb(_C_NSQ, _C_NSKV), [(nf, "col"), (cs, "col")],
                    functools.partial(_ep_norm, width=HEAD_DIM), BF16, tm=1024, tn=512, name="proj_nsq")
    nf = jnp.concatenate([zeros(2 * KV_W), ones(KV_W), zeros(KV_W), ones(KV_W), zeros(KV_W)]).reshape(1, -1)
    cs = jnp.concatenate([ones(2 * KV_W), _tiled(nsa_k_g[1], KV_HEADS), ones(KV_W),
                          _tiled(nsa_k_g[2], KV_HEADS), ones(KV_W)]).reshape(1, -1)
    p_nskv = _matmul(h, wb(_C_NSKV, _C_GNS), [(nf, "col"), (cs, "col")],
                     functools.partial(_ep_norm, width=HEAD_DIM), BF16, tm=1024, tn=512, name="proj_nskv")
    cols = np.array([_C_GNS + c * N_HEADS + Q_PER_KV * g + r
                     for g in range(KV_HEADS) for c in range(3) for r in range(Q_PER_KV)])
    wg = w_in[:, cols].reshape(d, KV_HEADS, 3 * Q_PER_KV)
    wg = jnp.pad(wg, ((0, 0), (0, 0), (0, LANES - 3 * Q_PER_KV))).reshape(d, KV_HEADS * LANES).astype(BF16)
    ns_gates = _matmul(h, wg, [], _ep_sigmoid, F32, tm=1024, tn=256, name="proj_nsgate")
    nf = jnp.concatenate([ones(2 * BRANCH_W), zeros(BRANCH_W)]).reshape(1, -1)
    cs = jnp.concatenate([_tiled(diff_q_g, 2 * N_HEADS) * DIFF_QK ** -0.5, _tiled(diff_k_g, 2 * N_HEADS),
                          ones(BRANCH_W)]).reshape(1, -1)
    p_df = _matmul(h, wb(_C_DF, _C_MERGE), [(nf, "col"), (cs, "col")],
                   functools.partial(_ep_norm, width=DIFF_QK), BF16, tm=1024, tn=512, name="proj_diff")
    m_gates = _matmul(h, wb(_C_MERGE, _C_MERGE + 3 * d), [], _ep_sigmoid, BF16, tm=1024, tn=512,
                      name="proj_merge_gate")

    o_sb = _sb_attention(p_sb, batch, seq)
    kvcm = _compress(p_nskv, cmp_pos, cmp_w1, cmp_w2, nsa_k_g[0], batch, seq)
    o_ns = _nsa_attention(p_nsq, kvcm, p_nskv, ns_gates, rel_bias[:, :N_HEADS], batch, seq)
    o_df = _diff_attention(p_df, rel_bias[:, N_HEADS:], diff_lam, diff_out_g, lam_init, batch, seq)
    merged = _branch_merge(o_sb, o_ns, o_df, w_branch.astype(BF16), m_gates)
    g1 = lax.slice_in_dim(mod_l, 2 * d, 3 * d, axis=1)
    return _matmul(merged, w_out.astype(BF16), [(x2, "tile"), (g1, "batch")], _ep_residual, F32,
                   tm=1024, tn=512, name="proj_out", seq=seq)


def _moe(x2, seq, mod_l, norm_gain, w_router, router_bias, w_gate, w_up, w_down):
    tm = 256
    h2, route = _norm_router(x2, norm_gain, mod_l, 3, 4, seq, w_router, router_bias)
    row_token, pos, tile_expert, tile_valid, n_rows = _routing_tables(route, tm)
    xs = _gather_rows(row_token, h2, n_rows)
    y = _expert_mlps(tile_expert, tile_valid, xs, w_gate.astype(BF16), w_up.astype(BF16),
                     w_down.astype(BF16), tm)
    return _combine(pos, y, x2, mod_l, 5, route, seq)


def kernel(x, c, rel_bias, w_router, router_bias, w_ada, b_ada, norm_mix, norm_ffn, w_in, nsa_cmp_pos,
           nsa_cmp_w1, nsa_cmp_w2, nsa_q_norm, nsa_k_norm, diff_q_norm, diff_k_norm, diff_lambda,
           diff_out_norm, w_branch, w_out, w_exp_gate, w_exp_up, w_exp_down):
    batch, seq, d = x.shape
    depth = w_ada.shape[0]
    mod = _adaln(c, w_ada, b_ada)
    x2 = x.reshape(batch * seq, d)
    for layer in range(depth):
        lam_init = 0.8 - 0.6 * math.exp(-0.3 * layer)
        mod_l = mod[layer]
        h = _norm_mod(x2, norm_mix[layer], mod_l, 0, 1, seq)
        x2 = _mixer(h, seq, batch, w_in[layer], w_branch[layer], w_out[layer], rel_bias,
                    nsa_cmp_pos[layer], nsa_cmp_w1[layer], nsa_cmp_w2[layer], nsa_q_norm[layer],
                    nsa_k_norm[layer], diff_q_norm[layer], diff_k_norm[layer], diff_lambda[layer],
                    diff_out_norm[layer], lam_init, x2, mod_l, d)
        x2 = _moe(x2, seq, mod_l, norm_ffn[layer], w_router, router_bias, w_exp_gate[layer],
                  w_exp_up[layer], w_exp_down[layer])
    return x2.reshape(batch, seq, d)
```

```python
import functools
import math

import numpy as np
import jax
import jax.numpy as jnp
from jax import lax
from jax.experimental import pallas as pl
from jax.experimental.pallas import tpu as pltpu

F32 = jnp.float32
BF16 = jnp.bfloat16

HEAD_DIM = 128
N_HEADS = 8
KV_HEADS = 2
Q_PER_KV = N_HEADS // KV_HEADS
DIFF_QK = HEAD_DIM // 2
BRANCH_W = N_HEADS * HEAD_DIM
KV_W = KV_HEADS * HEAD_DIM
CMP_LEN = 32
CMP_STRIDE = 16
SLC_LEN = 64
N_SLC = 16
N_LOCAL_SLC = 2
WINDOW = 512
REL_BUCKETS = 32
REL_MAX_DIST = 128
N_EXPERTS = 16
N_GROUPS = 4
EXP_PER_GROUP = N_EXPERTS // N_GROUPS
NORM_EPS = 1e-6
NEG_INF = -1e30
MASKED_BELOW = -1e29
FORCE_SCORE = 1e9
LANES = 128
SUBLANES = 8
VMEM_CAP = 56 * 1024 * 1024
SB_EXIT = 100.0

_C_SB = 0
_C_NSQ = 3 * BRANCH_W
_C_NSKV = _C_NSQ + BRANCH_W
_C_GNS = _C_NSKV + 6 * KV_W
_C_DF = _C_GNS + 3 * N_HEADS
_C_MERGE = _C_DF + 3 * BRANCH_W


def _cparams(sem, vmem_bytes):
    lim = int(min(VMEM_CAP, max(vmem_bytes * 5 // 4 + (2 << 20), 16 << 20)))
    return pltpu.CompilerParams(dimension_semantics=sem, vmem_limit_bytes=lim)


def _dot(a, b):
    return jnp.dot(a, b, preferred_element_type=F32)


def _dot_nt(a, b):
    return lax.dot_general(a, b, (((1,), (1,)), ((), ())), preferred_element_type=F32)


def _split_bf16(x):
    hi = x.astype(BF16)
    lo = (x - hi.astype(F32)).astype(BF16)
    return hi, lo


def _ada_kernel(cb_ref, w_ref, b_ref, o_ref, ca_sc, *, nb, rows_per_iter):
    k_dim, tn = w_ref.shape

    @pl.when((pl.program_id(0) == 0) & (pl.program_id(1) == 0))
    def _():
        cb = cb_ref[...]
        ca_sc[...] = cb * jax.nn.sigmoid(cb)

    reps = tn // LANES
    groups = rows_per_iter // SUBLANES

    def body(i, accs):
        r0 = pl.multiple_of(i * rows_per_iter, rows_per_iter)
        w3 = w_ref[pl.ds(r0, rows_per_iter), :].reshape(groups, SUBLANES, tn)
        new = []
        for b in range(nb):
            ca = ca_sc[b, pl.ds(r0, rows_per_iter), :].reshape(groups, SUBLANES, LANES)
            ca = jnp.concatenate([ca] * reps, axis=2)
            new.append(accs[b] + jnp.sum(w3 * ca, axis=0))
        return tuple(new)

    accs = lax.fori_loop(0, k_dim // rows_per_iter, body,
                         tuple(jnp.zeros((SUBLANES, tn), F32) for _ in range(nb)))
    rows = [jnp.sum(a, axis=0, keepdims=True) + b_ref[...] for a in accs]
    rows.append(jnp.zeros((SUBLANES - nb, tn), F32))
    o_ref[...] = jnp.concatenate(rows, axis=0)


def _adaln(c, w_ada, b_ada):
    depth, d, n = w_ada.shape
    nb = c.shape[0]
    tn = 512
    cb = jnp.broadcast_to(c[:, :, None], (nb, d, LANES))
    vmem = 2 * d * tn * 4 + 3 * nb * d * LANES * 4
    return pl.pallas_call(
        functools.partial(_ada_kernel, nb=nb, rows_per_iter=64),
        grid=(depth, n // tn),
        in_specs=[pl.BlockSpec((nb, d, LANES), lambda l, j: (0, 0, 0)),
                  pl.BlockSpec((None, d, tn), lambda l, j: (l, 0, j)),
                  pl.BlockSpec((None, 1, tn), lambda l, j: (l, 0, j))],
        out_specs=pl.BlockSpec((None, SUBLANES, tn), lambda l, j: (l, 0, j)),
        out_shape=jax.ShapeDtypeStruct((depth, SUBLANES, n), F32),
        scratch_shapes=[pltpu.VMEM((nb, d, LANES), F32)],
        compiler_params=_cparams(("arbitrary", "arbitrary"), vmem),
        name="adaln_mod",
    )(cb, w_ada, b_ada.reshape(depth, 1, n))


def _modulated_norm(x_ref, gain_ref, sc_ref, sh_ref, tiles_per_batch):
    b = pl.program_id(0) // tiles_per_batch
    x = x_ref[...]
    ms = jnp.mean(x * x, axis=-1, keepdims=True)
    y = x * lax.rsqrt(ms + NORM_EPS) * gain_ref[...]
    return y * (1.0 + sc_ref[pl.ds(b, 1), :]) + sh_ref[pl.ds(b, 1), :]


def _norm_kernel(x_ref, gain_ref, sc_ref, sh_ref, o_ref, *, tiles_per_batch):
    o_ref[...] = _modulated_norm(x_ref, gain_ref, sc_ref, sh_ref, tiles_per_batch).astype(o_ref.dtype)


def _norm_mod(x2, gain, mod_l, sh_chunk, sc_chunk, seq):
    t, d = x2.shape
    tm = 256
    return pl.pallas_call(
        functools.partial(_norm_kernel, tiles_per_batch=seq // tm),
        grid=(t // tm,),
        in_specs=[pl.BlockSpec((tm, d), lambda i: (i, 0)),
                  pl.BlockSpec((1, d), lambda i: (0, 0)),
                  pl.BlockSpec((SUBLANES, d), lambda i: (0, sc_chunk)),
                  pl.BlockSpec((SUBLANES, d), lambda i: (0, sh_chunk))],
        out_specs=pl.BlockSpec((tm, d), lambda i: (i, 0)),
        out_shape=jax.ShapeDtypeStruct((t, d), BF16),
        compiler_params=_cparams(("arbitrary",), 2 * tm * d * 6 + 4 * SUBLANES * d * 4),
        name="norm_mix",
    )(x2, gain.reshape(1, d), mod_l, mod_l)


def _norm_router_kernel(x_ref, gain_ref, sc_ref, sh_ref, wr_ref, rb_ref, h_ref, route_ref,
                        *, tiles_per_batch):
    h = _modulated_norm(x_ref, gain_ref, sc_ref, sh_ref, tiles_per_batch)
    h_ref[...] = h
    h_hi, h_lo = _split_bf16(h)
    w_hi, w_lo = _split_bf16(wr_ref[...])
    logits = _dot(h_hi, w_hi) + _dot(h_hi, w_lo) + _dot(h_lo, w_hi)
    aff = jax.nn.sigmoid(logits)
    biased = aff + rb_ref[...]
    lane_i = lax.broadcasted_iota(jnp.int32, aff.shape, 1)
    lane = lane_i.astype(F32)
    low = jnp.float32(-3e38)
    big = jnp.float32(1 << 20)
    best = None
    for gi in range(N_GROUPS):
        in_group = (lane_i // EXP_PER_GROUP) == gi
        v = jnp.where(in_group, biased, low)
        m1 = jnp.max(v, axis=-1, keepdims=True)
        i1 = jnp.min(jnp.where(v == m1, lane, big), axis=-1, keepdims=True)
        v2 = jnp.where(lane == i1, low, v)
        m2 = jnp.max(v2, axis=-1, keepdims=True)
        i2 = jnp.min(jnp.where(v2 == m2, lane, big), axis=-1, keepdims=True)
        score = m1 + m2
        if best is None:
            best = (score, i1, i2)
        else:
            better = score > best[0]
            best = (jnp.where(better, score, best[0]), jnp.where(better, i1, best[1]),
                    jnp.where(better, i2, best[2]))
    _, e1, e2 = best
    a1 = jnp.sum(jnp.where(lane == e1, aff, 0.0), axis=-1, keepdims=True)
    a2 = jnp.sum(jnp.where(lane == e2, aff, 0.0), axis=-1, keepdims=True)
    den = a1 + a2
    out = jnp.where(lane_i == 0, e1,
                    jnp.where(lane_i == 1, e2,
                              jnp.where(lane_i == 2, a1 / den, jnp.where(lane_i == 3, a2 / den, 0.0))))
    route_ref[...] = out


def _norm_router(x2, gain, mod_l, sh_chunk, sc_chunk, seq, w_router, router_bias):
    t, d = x2.shape
    tm = 256
    wr = jnp.pad(w_router, ((0, 0), (0, LANES - N_EXPERTS)))
    rb = jnp.pad(router_bias, (0, LANES - N_EXPERTS)).reshape(1, LANES)
    return pl.pallas_call(
        functools.partial(_norm_router_kernel, tiles_per_batch=seq // tm),
        grid=(t // tm,),
        in_specs=[pl.BlockSpec((tm, d), lambda i: (i, 0)),
                  pl.BlockSpec((1, d), lambda i: (0, 0)),
                  pl.BlockSpec((SUBLANES, d), lambda i: (0, sc_chunk)),
                  pl.BlockSpec((SUBLANES, d), lambda i: (0, sh_chunk)),
                  pl.BlockSpec((d, LANES), lambda i: (0, 0)),
                  pl.BlockSpec((1, LANES), lambda i: (0, 0))],
        out_specs=[pl.BlockSpec((tm, d), lambda i: (i, 0)),
                   pl.BlockSpec((tm, LANES), lambda i: (i, 0))],
        out_shape=[jax.ShapeDtypeStruct((t, d), F32),
                   jax.ShapeDtypeStruct((t, LANES), F32)],
        compiler_params=_cparams(("arbitrary",), 2 * tm * d * 8 + 4 * d * LANES * 4 + 8 * tm * d),
        name="norm_ffn_router",
    )(x2, gain.reshape(1, d), mod_l, mod_l, wr, rb)


def _mm_kernel(a_ref, b_ref, *rest, epilogue, n_extra, tiles_per_batch):
    extras = rest[:n_extra]
    o_ref = rest[n_extra]
    acc = _dot(a_ref[...], b_ref[...])
    o_ref[...] = epilogue(acc, extras, pl.program_id(0) // tiles_per_batch).astype(o_ref.dtype)


def _matmul(a, b, extras, epilogue, out_dtype, *, tm, tn, name, seq=None):
    m, k = a.shape
    n = b.shape[1]
    tm = min(tm, m)
    tn = min(tn, n)
    assert m % tm == 0 and n % tn == 0
    in_specs = [pl.BlockSpec((tm, k), lambda i, j: (i, 0)),
                pl.BlockSpec((k, tn), lambda i, j: (0, j))]
    vmem = 2 * (tm * k + k * tn) * 2 + 2 * tm * tn * jnp.dtype(out_dtype).itemsize + 2 * tm * tn * 4
    args = []
    for arr, kind in extras:
        if kind == "col":
            in_specs.append(pl.BlockSpec((1, tn), lambda i, j: (0, j)))
            vmem += 2 * SUBLANES * tn * 4
        elif kind == "tile":
            in_specs.append(pl.BlockSpec((tm, tn), lambda i, j: (i, j)))
            vmem += 2 * tm * tn * arr.dtype.itemsize
        else:
            in_specs.append(pl.BlockSpec((SUBLANES, tn), lambda i, j: (0, j)))
            vmem += 2 * SUBLANES * tn * 4
        args.append(arr)
    tiles_per_batch = (seq // tm) if seq else 1
    return pl.pallas_call(
        functools.partial(_mm_kernel, epilogue=epilogue, n_extra=len(extras),
                          tiles_per_batch=tiles_per_batch),
        grid=(m // tm, n // tn),
        in_specs=in_specs,
        out_specs=pl.BlockSpec((tm, tn), lambda i, j: (i, j)),
        out_shape=jax.ShapeDtypeStruct((m, n), out_dtype),
        compiler_params=_cparams(("arbitrary", "arbitrary"), vmem),
        name=name,
    )(a, b, *args)


def _ep_scale(acc, extras, b):
    return acc * extras[0][...]


def _ep_sigmoid(acc, extras, b):
    return jax.nn.sigmoid(acc)


def _ep_norm(acc, extras, b, *, width):
    nf_ref, cs_ref = extras
    outs = []
    for c in range(acc.shape[1] // LANES):
        sl = slice(c * LANES, (c + 1) * LANES)
        blk = acc[:, sl]
        sq = blk * blk
        if width == LANES:
            ms = jnp.mean(sq, axis=-1, keepdims=True)
        else:
            lane = lax.broadcasted_iota(jnp.int32, blk.shape, 1)
            first = lane < width
            lo = jnp.sum(jnp.where(first, sq, 0.0), axis=-1, keepdims=True) / width
            hi = jnp.sum(jnp.where(first, 0.0, sq), axis=-1, keepdims=True) / width
            ms = jnp.where(first, lo, hi)
        inv = lax.rsqrt(ms + NORM_EPS)
        f = jnp.where(nf_ref[:, sl] > 0.0, inv, 1.0)
        outs.append(blk * f * cs_ref[:, sl])
    return jnp.concatenate(outs, axis=1)


def _ep_residual(acc, extras, b):
    x_ref, g_ref = extras
    return x_ref[...] + g_ref[pl.ds(b, 1), :] * acc


def _bucket_np(dist):
    n = np.maximum(dist, 0)
    max_exact = REL_BUCKETS // 2
    nf = np.maximum(n, max_exact).astype(np.float32)
    large = max_exact + (np.log(nf / np.float32(max_exact))
                         / np.float32(math.log(REL_MAX_DIST / max_exact))
                         * np.float32(REL_BUCKETS - max_exact)).astype(np.int32)
    large = np.minimum(large, REL_BUCKETS - 1)
    return np.where(n < max_exact, n, large).astype(np.int32)


def _bucket_starts():
    b = _bucket_np(np.arange(4 * REL_MAX_DIST))
    return [int(np.argmax(b >= k)) for k in range(1, REL_BUCKETS)]


def _delta_bias(table, dist, valid):
    tb = table.T.astype(F32)
    shape = (tb.shape[0],) + (1,) * dist.ndim
    val = jnp.broadcast_to(tb[:, 0].reshape(shape), (tb.shape[0],) + dist.shape)
    for k, start in enumerate(_bucket_starts(), start=1):
        val = jnp.where(dist[None] >= start, tb[:, k].reshape(shape), val)
    val = val - tb[:, REL_BUCKETS - 1].reshape(shape)
    return jnp.where(valid[None], val, NEG_INF)


def _toeplitz_dist(n_off, step, nr, nc):
    shape = (n_off, nr, nc)
    return (lax.broadcasted_iota(jnp.int32, shape, 0) * step + lax.broadcasted_iota(jnp.int32, shape, 1)
            - lax.broadcasted_iota(jnp.int32, shape, 2))


def _stack_group_heads(tiles):
    nh, k, nr, nc = tiles.shape
    t = tiles.reshape(KV_HEADS, Q_PER_KV, k, nr, nc)
    return jnp.transpose(t, (0, 2, 1, 3, 4)).reshape(KV_HEADS, k, Q_PER_KV * nr, nc)


def _sb_kernel(q_ref, k_ref, v_ref, o_ref, *, tq):
    qi = pl.program_id(2)
    q = q_ref[...]
    row = lax.broadcasted_iota(jnp.int32, (tq, tq), 0)
    col = lax.broadcasted_iota(jnp.int32, (tq, tq), 1)
    later = jnp.where(row > col, 1.0, 0.0).astype(BF16)
    causal = col < row

    def step(kj, carry, acc, diag):
        k0 = pl.multiple_of(kj * tq, tq)
        k = k_ref[pl.ds(k0, tq), :]
        v = v_ref[pl.ds(k0, tq), :]
        z = _dot_nt(q, k)
        log_succ = jnp.minimum(z, 0.0) - jnp.log1p(jnp.exp(-jnp.abs(z)))
        log_fail = log_succ - z
        if diag:
            log_fail = jnp.where(causal, log_fail, 0.0)
        hi, lo = _split_bf16(log_fail)
        suffix = _dot(hi, later) + _dot(lo, later)
        a = jnp.exp(log_succ + suffix + carry)
        if diag:
            a = jnp.where(causal, a, 0.0)
        acc = acc + _dot(a.astype(BF16), v)
        carry = carry + jnp.sum(log_fail, axis=-1, keepdims=True)
        return carry, acc

    carry, acc = step(qi, jnp.zeros((tq, 1), F32), jnp.zeros((tq, HEAD_DIM), F32), True)

    def cond(state):
        kj, carry, _ = state
        return (kj >= 0) & (jnp.max(carry) > -SB_EXIT)

    def body(state):
        kj, carry, acc = state
        carry, acc = step(kj, carry, acc, False)
        return kj - 1, carry, acc

    _, _, acc = lax.while_loop(cond, body, (qi - 1, carry, acc))
    o_ref[...] = acc.astype(o_ref.dtype)


def _sb_attention(p_sb, batch, seq):
    tq = min(256, seq)
    nq = seq // tq
    t = batch * seq
    vmem = 4 * seq * HEAD_DIM * 2 + 4 * tq * HEAD_DIM * 2 + 16 * tq * tq * 4
    return pl.pallas_call(
        functools.partial(_sb_kernel, tq=tq),
        grid=(batch, N_HEADS, nq),
        in_specs=[pl.BlockSpec((tq, HEAD_DIM), lambda b, h, i: (b * nq + i, h)),
                  pl.BlockSpec((seq, HEAD_DIM), lambda b, h, i: (b, N_HEADS + h)),
                  pl.BlockSpec((seq, HEAD_DIM), lambda b, h, i: (b, 2 * N_HEADS + h))],
        out_specs=pl.BlockSpec((tq, HEAD_DIM), lambda b, h, i: (b * nq + i, h)),
        out_shape=jax.ShapeDtypeStruct((t, BRANCH_W), BF16),
        compiler_params=_cparams(("arbitrary",) * 3, vmem),
        name="sb_attention",
    )(p_sb, p_sb, p_sb)


def _softmax_first(s, v, m_sc, l_sc, acc_sc):
    m = jnp.max(s, axis=-1, keepdims=True)
    p = jnp.exp(s - m)
    m_sc[...] = m
    l_sc[...] = jnp.sum(p, axis=-1, keepdims=True)
    acc_sc[...] = _dot(p.astype(BF16), v)


def _softmax_next(s, v, m_sc, l_sc, acc_sc):
    m_old = m_sc[...]
    m_new = jnp.maximum(m_old, jnp.max(s, axis=-1, keepdims=True))
    p = jnp.exp(s - m_new)
    alpha = jnp.exp(m_old - m_new)
    l_sc[...] = alpha * l_sc[...] + jnp.sum(p, axis=-1, keepdims=True)
    acc_sc[...] = alpha * acc_sc[...] + _dot(p.astype(BF16), v)
    m_sc[...] = m_new


def _softmax_result(l_sc, acc_sc):
    return acc_sc[...] / jnp.maximum(l_sc[...], 1.0)


def _compress_kernel(x_ref, p_ref, w1a_ref, w1b_ref, w2_ref, gain_ref, o_ref):
    nc = x_ref.shape[0]
    is_key = pl.program_id(1) < KV_HEADS
    x = x_ref[...].astype(F32)
    first = _dot((x + p_ref[0:1, :]).astype(BF16), w1a_ref[...])
    second = _dot((x + p_ref[1:2, :]).astype(BF16), w1b_ref[...])
    pre = first + pltpu.roll(second, nc - 1, 0)
    hid = pre * jax.nn.sigmoid(pre)
    out = _dot(hid.astype(BF16), w2_ref[...])
    ms = jnp.mean(out * out, axis=-1, keepdims=True)
    normed = out * lax.rsqrt(ms + NORM_EPS) * gain_ref[...]
    o_ref[...] = jnp.where(is_key, normed, out).astype(o_ref.dtype)


def _compress(p_nskv, cmp_pos, cmp_w1, cmp_w2, k_gain0, batch, seq):
    nc = seq // CMP_STRIDE
    half = CMP_STRIDE * HEAD_DIM
    x = p_nskv[:, :2 * KV_W].reshape(batch, nc, CMP_STRIDE, 2 * KV_HEADS, HEAD_DIM)
    x = jnp.transpose(x, (0, 3, 1, 2, 4)).reshape(batch, 2 * KV_HEADS, nc, half)
    pos = cmp_pos.reshape(2, 2, half)
    w1 = cmp_w1.astype(BF16).reshape(2, 2, half, HEAD_DIM)
    w2 = cmp_w2.astype(BF16)
    return pl.pallas_call(
        _compress_kernel,
        grid=(batch, 2 * KV_HEADS),
        in_specs=[pl.BlockSpec((None, None, nc, half), lambda b, j: (b, j, 0, 0)),
                  pl.BlockSpec((None, 2, half), lambda b, j: (j // KV_HEADS, 0, 0)),
                  pl.BlockSpec((None, None, half, HEAD_DIM), lambda b, j: (j // KV_HEADS, 0, 0, 0)),
                  pl.BlockSpec((None, None, half, HEAD_DIM), lambda b, j: (j // KV_HEADS, 1, 0, 0)),
                  pl.BlockSpec((None, HEAD_DIM, HEAD_DIM), lambda b, j: (j // KV_HEADS, 0, 0)),
                  pl.BlockSpec((1, HEAD_DIM), lambda b, j: (0, 0))],
        out_specs=pl.BlockSpec((None, None, nc, HEAD_DIM), lambda b, j: (b, j, 0, 0)),
        out_shape=jax.ShapeDtypeStruct((batch, 2 * KV_HEADS, nc, HEAD_DIM), BF16),
        compiler_params=_cparams(("arbitrary", "arbitrary"), 8 * nc * half * 2 + 8 * half * HEAD_DIM * 2),
        name="nsa_compress",
    )(x, pos, w1, w1, w2, k_gain0.reshape(1, HEAD_DIM))


def _nsa_kernel(q_ref, kcm_ref, vcm_ref, ks_ref, vs_ref, kw_ref, vw_ref, gate_ref,
                pt_ref, bt_ref, wt_ref, o_ref, m_sc, l_sc, acc_sc, *, tq, tk, tw, n_blk, k_sel):
    qi = pl.program_id(2)
    q0 = qi * tq
    rows = Q_PER_KV * tq
    nc = kcm_ref.shape[0]
    q4 = jnp.concatenate([q_ref[:, r * HEAD_DIM:(r + 1) * HEAD_DIM] for r in range(Q_PER_KV)], axis=0)

    per_tile = tq // CMP_STRIDE
    lead = 2 * per_tile
    shift = lax.rem(qi * per_tile + (nc - lead), nc)
    cb = pltpu.roll(pt_ref[...], shift, 1)
    rel = lax.broadcasted_iota(jnp.int32, (rows, nc), 1) - qi * per_tile
    cb = jnp.where(rel < -lead, 0.0, jnp.where(rel >= per_tile, NEG_INF, cb))
    s = _dot_nt(q4, kcm_ref[...]) + cb
    valid = s > MASKED_BELOW
    s = jnp.where(valid, s, NEG_INF)
    p = jnp.where(valid, jnp.exp(s - jnp.max(s, axis=-1, keepdims=True)), 0.0)
    p = p / jnp.maximum(jnp.sum(p, axis=-1, keepdims=True), 1.0)
    o_cmp = _dot(p.astype(BF16), vcm_ref[...])

    psum = p[0:tq]
    for r in range(1, Q_PER_KV):
        psum = psum + p[r * tq:(r + 1) * tq]
    n_id = lax.broadcasted_iota(jnp.int32, (nc, LANES), 0)
    j_id = lax.broadcasted_iota(jnp.int32, (nc, LANES), 1)
    overlap = ((CMP_STRIDE * n_id < SLC_LEN * j_id + SLC_LEN)
               & (CMP_STRIDE * n_id + CMP_LEN > SLC_LEN * j_id) & (j_id < n_blk))
    overlap = jnp.where(overlap, 1.0, 0.0).astype(BF16)
    p_hi, p_lo = _split_bf16(psum)
    imp = _dot(p_hi, overlap) + _dot(p_lo, overlap)
    imp_t = jnp.transpose(imp)
    blk = lax.broadcasted_iota(jnp.int32, (LANES, tq), 0)
    cur = (q0 + lax.broadcasted_iota(jnp.int32, (LANES, tq), 1)) // SLC_LEN
    forced = (blk == 0) | ((blk <= cur) & (blk > cur - N_LOCAL_SLC))
    imp_t = jnp.where(forced, FORCE_SCORE, jnp.where(blk > cur, NEG_INF, imp_t))
    imp_t = jnp.where(blk < n_blk, imp_t, -3e38)
    rank = jnp.zeros((LANES, tq), F32)
    for j in range(n_blk):
        cj = imp_t[j:j + 1, :]
        ahead = jnp.where(blk > j, jnp.where(cj >= imp_t, 1.0, 0.0), jnp.where(cj > imp_t, 1.0, 0.0))
        rank = rank + ahead
    sel_t = jnp.where((rank < k_sel) & (blk < n_blk), 1.0, 0.0)
    sel = jnp.transpose(sel_t).astype(BF16)

    per = tk // tq
    kd = qi // per
    par = qi - kd * per
    e_row = lax.broadcasted_iota(jnp.int32, (LANES, tk), 0)
    e_col = lax.broadcasted_iota(jnp.int32, (LANES, tk), 1)

    def slc_scores(kj, bias):
        k0 = pl.multiple_of(kj * tk, tk)
        expand = jnp.where((k0 + e_col) // SLC_LEN == e_row, 1.0, 0.0).astype(BF16)
        chosen = _dot(sel, expand)
        s = _dot_nt(q4, ks_ref[pl.ds(k0, tk), :])
        if bias is not None:
            s = s + bias
        s = jnp.where(chosen[None] > 0.5, s.reshape(Q_PER_KV, tq, tk), NEG_INF).reshape(rows, tk)
        return s, vs_ref[pl.ds(k0, tk), :]

    s, v = slc_scores(kd, bt_ref[par])
    _softmax_first(s, v, m_sc, l_sc, acc_sc)

    @pl.when(kd >= 1)
    def _():
        s, v = slc_scores(kd - 1, bt_ref[per + par])
        _softmax_next(s, v, m_sc, l_sc, acc_sc)

    def slc_far(i, _):
        s, v = slc_scores(kd - 2 - i, None)
        _softmax_next(s, v, m_sc, l_sc, acc_sc)
        return 0

    lax.fori_loop(0, jnp.maximum(kd - 1, 0), slc_far, 0)
    o_slc = _softmax_result(l_sc, acc_sc)

    start = pl.multiple_of(jnp.maximum(q0 - WINDOW, 0), tq)
    s = _dot_nt(q4, kw_ref[pl.ds(start, tw), :]) + wt_ref[jnp.minimum(qi, WINDOW // tq)]
    p = jnp.exp(s - jnp.max(s, axis=-1, keepdims=True))
    o_win = _dot(p.astype(BF16), vw_ref[pl.ds(start, tw), :]) / jnp.maximum(
        jnp.sum(p, axis=-1, keepdims=True), 1.0)

    gate = gate_ref[...]
    for r in range(Q_PER_KV):
        sl = slice(r * tq, (r + 1) * tq)
        o = (gate[:, r:r + 1] * o_cmp[sl] + gate[:, Q_PER_KV + r:Q_PER_KV + r + 1] * o_slc[sl]
             + gate[:, 2 * Q_PER_KV + r:2 * Q_PER_KV + r + 1] * o_win[sl])
        o_ref[:, r * HEAD_DIM:(r + 1) * HEAD_DIM] = o.astype(o_ref.dtype)


def _nsa_attention(p_nsq, kvcm, p_nskv, gates, rel_nsa, batch, seq):
    tq = 128
    tk = min(512, seq)
    tw = WINDOW + tq
    nq = seq // tq
    nc = seq // CMP_STRIDE
    n_blk = seq // SLC_LEN
    k_sel = min(N_SLC, n_blk)
    t = batch * seq
    rows = Q_PER_KV * tq
    per = tk // tq
    assert n_blk <= LANES and nc % LANES == 0 and WINDOW % tq == 0 and tw <= seq and tq == LANES

    d = _toeplitz_dist(2 * per, tq, tq, tk)
    bt = _stack_group_heads(_delta_bias(rel_nsa, d, d >= 0))
    d = _toeplitz_dist(WINDOW // tq + 1, tq, tq, tw)
    wt = _stack_group_heads(_delta_bias(rel_nsa, d, (d >= 0) & (d < WINDOW)))
    per_tile = tq // CMP_STRIDE
    d = (lax.broadcasted_iota(jnp.int32, (1, tq, nc), 1)
         - CMP_STRIDE * (lax.broadcasted_iota(jnp.int32, (1, tq, nc), 2) - 2 * per_tile) - (CMP_LEN - 1))
    pt = _stack_group_heads(_delta_bias(rel_nsa, d, d >= 0))[:, 0]

    const = lambda shape: pl.BlockSpec((None,) + shape, lambda b, g, i: (g,) + (0,) * len(shape),
                                       pipeline_mode=pl.Buffered(1))
    vmem = (8 * seq * HEAD_DIM * 2 + (bt.size + wt.size + pt.size) // KV_HEADS * 4
            + 10 * rows * tw * 4 + 8 * rows * nc * 4)
    return pl.pallas_call(
        functools.partial(_nsa_kernel, tq=tq, tk=tk, tw=tw, n_blk=n_blk, k_sel=k_sel),
        grid=(batch, KV_HEADS, nq),
        in_specs=[pl.BlockSpec((tq, Q_PER_KV * HEAD_DIM), lambda b, g, i: (b * nq + i, g)),
                  pl.BlockSpec((None, None, nc, HEAD_DIM), lambda b, g, i: (b, g, 0, 0)),
                  pl.BlockSpec((None, None, nc, HEAD_DIM), lambda b, g, i: (b, KV_HEADS + g, 0, 0)),
                  pl.BlockSpec((seq, HEAD_DIM), lambda b, g, i: (b, 2 * KV_HEADS + g)),
                  pl.BlockSpec((seq, HEAD_DIM), lambda b, g, i: (b, 3 * KV_HEADS + g)),
                  pl.BlockSpec((seq, HEAD_DIM), lambda b, g, i: (b, 4 * KV_HEADS + g)),
                  pl.BlockSpec((seq, HEAD_DIM), lambda b, g, i: (b, 5 * KV_HEADS + g)),
                  pl.BlockSpec((tq, LANES), lambda b, g, i: (b * nq + i, g)),
                  const(pt.shape[1:]), const(bt.shape[1:]), const(wt.shape[1:])],
        out_specs=pl.BlockSpec((tq, Q_PER_KV * HEAD_DIM), lambda b, g, i: (b * nq + i, g)),
        out_shape=jax.ShapeDtypeStruct((t, BRANCH_W), BF16),
        scratch_shapes=[pltpu.VMEM((rows, 1), F32), pltpu.VMEM((rows, 1), F32),
                        pltpu.VMEM((rows, HEAD_DIM), F32)],
        compiler_params=_cparams(("arbitrary",) * 3, vmem),
        name="nsa_attention",
    )(p_nsq, kvcm, kvcm, p_nskv, p_nskv, p_nskv, p_nskv, gates, pt, bt, wt)


def _diff_kernel(q_ref, k_ref, v_ref, dt_ref, lam_ref, og_ref, o_ref, m_sc, l_sc, acc_sc,
                 *, tq, tk, lam_init):
    qi = pl.program_id(2)
    per = tk // tq
    kd = qi // per
    par = qi - kd * per
    q = q_ref[...]
    lane = lax.broadcasted_iota(jnp.int32, q.shape, 1)
    zero = jnp.zeros_like(q)
    q2 = jnp.concatenate([jnp.where(lane < DIFF_QK, q, zero), jnp.where(lane < DIFF_QK, zero, q)], axis=0)

    def scores(kj, bias):
        k0 = pl.multiple_of(kj * tk, tk)
        s = _dot_nt(q2, k_ref[pl.ds(k0, tk), :])
        if bias is not None:
            s = s + bias
        return s, v_ref[pl.ds(k0, tk), :]

    s, v = scores(kd, dt_ref[par])
    _softmax_first(s, v, m_sc, l_sc, acc_sc)

    @pl.when(kd >= 1)
    def _():
        s, v = scores(kd - 1, dt_ref[per + par])
        _softmax_next(s, v, m_sc, l_sc, acc_sc)

    def far(i, _):
        s, v = scores(kd - 2 - i, None)
        _softmax_next(s, v, m_sc, l_sc, acc_sc)
        return 0

    lax.fori_loop(0, jnp.maximum(kd - 1, 0), far, 0)
    o2 = _softmax_result(l_sc, acc_sc)
    lp = lam_ref[...]
    lam = (jnp.exp(jnp.sum(lp[0:1] * lp[1:2], axis=-1, keepdims=True))
           - jnp.exp(jnp.sum(lp[2:3] * lp[3:4], axis=-1, keepdims=True)) + lam_init)
    o = o2[:tq] - lam * o2[tq:]
    ms = jnp.mean(o * o, axis=-1, keepdims=True)
    o = o * lax.rsqrt(ms + NORM_EPS) * og_ref[...]
    o_ref[...] = (o * (1.0 - lam_init)).astype(o_ref.dtype)


def _diff_attention(p_df, rel_diff, lam_params, out_gain, lam_init, batch, seq):
    tq = min(256, seq)
    tk = min(512, seq)
    per = tk // tq
    nq = seq // tq
    t = batch * seq
    d = _toeplitz_dist(2 * per, tq, tq, tk)
    dt = _delta_bias(rel_diff, d, d >= 0)
    dt = jnp.concatenate([dt, dt], axis=2)
    vmem = 4 * seq * HEAD_DIM * 2 + 2 * 2 * per * 2 * tq * tk * 4 + 12 * 2 * tq * tk * 4
    return pl.pallas_call(
        functools.partial(_diff_kernel, tq=tq, tk=tk, lam_init=lam_init),
        grid=(batch, N_HEADS, nq),
        in_specs=[pl.BlockSpec((tq, HEAD_DIM), lambda b, h, i: (b * nq + i, h)),
                  pl.BlockSpec((seq, HEAD_DIM), lambda b, h, i: (b, N_HEADS + h)),
                  pl.BlockSpec((seq, HEAD_DIM), lambda b, h, i: (b, 2 * N_HEADS + h)),
                  pl.BlockSpec((None, 2 * per, 2 * tq, tk), lambda b, h, i: (h, 0, 0, 0)),
                  pl.BlockSpec((4, DIFF_QK), lambda b, h, i: (0, 0)),
                  pl.BlockSpec((1, HEAD_DIM), lambda b, h, i: (0, 0))],
        out_specs=pl.BlockSpec((tq, HEAD_DIM), lambda b, h, i: (b * nq + i, h)),
        out_shape=jax.ShapeDtypeStruct((t, BRANCH_W), BF16),
        scratch_shapes=[pltpu.VMEM((2 * tq, 1), F32), pltpu.VMEM((2 * tq, 1), F32),
                        pltpu.VMEM((2 * tq, HEAD_DIM), F32)],
        compiler_params=_cparams(("arbitrary",) * 3, vmem),
        name="diff_attention",
    )(p_df, p_df, p_df, dt, lam_params, out_gain.reshape(1, HEAD_DIM))


def _merge_kernel(o0_ref, o1_ref, o2_ref, w_ref, g0_ref, g1_ref, g2_ref, out_ref):
    acc = g0_ref[...].astype(F32) * _dot(o0_ref[...], w_ref[0])
    acc = acc + g1_ref[...].astype(F32) * _dot(o1_ref[...], w_ref[1])
    acc = acc + g2_ref[...].astype(F32) * _dot(o2_ref[...], w_ref[2])
    out_ref[...] = acc.astype(out_ref.dtype)


def _branch_merge(o_sb, o_ns, o_df, w_branch, gates):
    t = o_sb.shape[0]
    d = w_branch.shape[2]
    tm = min(512, t)
    tn = min(512, d)
    nj = d // tn
    o_spec = pl.BlockSpec((tm, BRANCH_W), lambda i, j: (i, 0))
    vmem = 2 * (3 * tm * BRANCH_W * 2 + 3 * BRANCH_W * tn * 2 + 3 * tm * tn * 2 + tm * tn * 2) + 4 * tm * tn * 4
    return pl.pallas_call(
        _merge_kernel,
        grid=(t // tm, nj),
        in_specs=[o_spec, o_spec, o_spec,
                  pl.BlockSpec((3, BRANCH_W, tn), lambda i, j: (0, 0, j)),
                  pl.BlockSpec((tm, tn), lambda i, j: (i, j)),
                  pl.BlockSpec((tm, tn), lambda i, j: (i, nj + j)),
                  pl.BlockSpec((tm, tn), lambda i, j: (i, 2 * nj + j))],
        out_specs=pl.BlockSpec((tm, tn), lambda i, j: (i, j)),
        out_shape=jax.ShapeDtypeStruct((t, d), BF16),
        compiler_params=_cparams(("arbitrary", "arbitrary"), vmem),
        name="branch_merge",
    )(o_sb, o_ns, o_df, w_branch, gates, gates, gates)


def _row_copy(src_ref, src_row, dst_ref, dst_row, sem):
    return pltpu.make_async_copy(src_ref.at[pl.ds(src_row, 1)], dst_ref.at[pl.ds(dst_row, 1)], sem)


def _gather_kernel(idx_ref, cnt_ref, src_ref, o_ref, buf, sem, *, tg):
    i = pl.program_id(0)
    base = i * tg
    n = cnt_ref[i]

    @pl.when(i == 0)
    def _():
        buf[...] = jnp.zeros(buf.shape, buf.dtype)

    for parity in range(2):
        def issue(r2, _, parity=parity):
            r = 2 * r2 + parity
            _row_copy(src_ref, idx_ref[base + r], buf, r, sem).start(priority=parity)
            return 0

        lax.fori_loop(0, (n + 1 - parity) // 2, issue, 0)

    def drain(r, _):
        _row_copy(src_ref, 0, buf, r, sem).wait()
        return 0

    lax.fori_loop(0, n, drain, 0)
    o_ref[...] = buf[...].astype(o_ref.dtype)


def _gather_rows(row_token, tile_rows, h2, n_rows, tg):
    d = h2.shape[1]
    return pl.pallas_call(
        functools.partial(_gather_kernel, tg=tg),
        grid_spec=pltpu.PrefetchScalarGridSpec(
            num_scalar_prefetch=2,
            grid=(n_rows // tg,),
            in_specs=[pl.BlockSpec(memory_space=pl.ANY)],
            out_specs=pl.BlockSpec((tg, d), lambda i, idx, cnt: (i, 0)),
            scratch_shapes=[pltpu.VMEM((tg, d), F32), pltpu.SemaphoreType.DMA(())]),
        out_shape=jax.ShapeDtypeStruct((n_rows, d), BF16),
        compiler_params=_cparams(("arbitrary",), tg * d * 4 + 2 * tg * d * 2),
        name="moe_gather",
    )(row_token, tile_rows, h2)


def _expert_changed(te_ref, t):
    return (t == 0) | (te_ref[t] != te_ref[jnp.maximum(t - 1, 0)])


def _expert_up_kernel(te_ref, tv_ref, x_ref, wg_ref, wu_ref, o_ref, wg_sc, wu_sc):
    t = pl.program_id(1)

    @pl.when(_expert_changed(te_ref, t))
    def _():
        wg_sc[...] = wg_ref[...].astype(BF16)
        wu_sc[...] = wu_ref[...].astype(BF16)

    @pl.when(tv_ref[t] > 0)
    def _():
        x = x_ref[...]
        g = _dot(x, wg_sc[...])
        u = _dot(x, wu_sc[...])
        o_ref[...] = (g * jax.nn.sigmoid(g) * u).astype(o_ref.dtype)

    @pl.when(tv_ref[t] == 0)
    def _():
        o_ref[...] = jnp.zeros(o_ref.shape, o_ref.dtype)


def _expert_down_kernel(te_ref, tv_ref, h_ref, wd_ref, o_ref, wd_sc):
    t = pl.program_id(1)

    @pl.when(_expert_changed(te_ref, t))
    def _():
        wd_sc[...] = wd_ref[...].astype(BF16)

    @pl.when(tv_ref[t] > 0)
    def _():
        o_ref[...] = _dot(h_ref[...], wd_sc[...])

    @pl.when(tv_ref[t] == 0)
    def _():
        o_ref[...] = jnp.zeros(o_ref.shape, o_ref.dtype)


def _expert_mlps(tile_expert, tile_valid, xs, w_gate, w_up, w_down, tm):
    n_rows, d = xs.shape
    f = w_gate.shape[2]
    tf = min(512, f)
    td = min(2048, d)
    nt = n_rows // tm
    hidden = pl.pallas_call(
        _expert_up_kernel,
        grid_spec=pltpu.PrefetchScalarGridSpec(
            num_scalar_prefetch=2,
            grid=(f // tf, nt),
            in_specs=[pl.BlockSpec((tm, d), lambda j, t, te, tv: (t, 0)),
                      pl.BlockSpec((None, d, tf), lambda j, t, te, tv: (te[t], 0, j)),
                      pl.BlockSpec((None, d, tf), lambda j, t, te, tv: (te[t], 0, j))],
            out_specs=pl.BlockSpec((tm, tf), lambda j, t, te, tv: (t, j)),
            scratch_shapes=[pltpu.VMEM((d, tf), BF16), pltpu.VMEM((d, tf), BF16)]),
        out_shape=jax.ShapeDtypeStruct((n_rows, f), BF16),
        compiler_params=_cparams(("arbitrary", "arbitrary"),
                                 2 * (tm * d * 2 + 2 * d * tf * 4 + tm * tf * 2) + 2 * d * tf * 2 + 4 * tm * tf * 4),
        name="moe_gate_up",
    )(tile_expert, tile_valid, xs, w_gate, w_up)
    return pl.pallas_call(
        _expert_down_kernel,
        grid_spec=pltpu.PrefetchScalarGridSpec(
            num_scalar_prefetch=2,
            grid=(d // td, nt),
            in_specs=[pl.BlockSpec((tm, f), lambda j, t, te, tv: (t, 0)),
                      pl.BlockSpec((None, f, td), lambda j, t, te, tv: (te[t], 0, j))],
            out_specs=pl.BlockSpec((tm, td), lambda j, t, te, tv: (t, j)),
            scratch_shapes=[pltpu.VMEM((f, td), BF16)]),
        out_shape=jax.ShapeDtypeStruct((n_rows, d), F32),
        compiler_params=_cparams(("arbitrary", "arbitrary"),
                                 2 * (tm * f * 2 + f * td * 4 + tm * td * 4) + f * td * 2 + tm * td * 4),
        name="moe_down",
    )(tile_expert, tile_valid, hidden, w_down)


def _combine_kernel(pos_ref, y_ref, x_ref, g_ref, route_ref, o_ref, buf0, buf1, sem,
                    *, tc, tiles_per_batch):
    i = pl.program_id(0)
    base = i * tc

    def issue(r, _):
        _row_copy(y_ref, pos_ref[2 * (base + r)], buf0, r, sem).start(priority=0)
        _row_copy(y_ref, pos_ref[2 * (base + r) + 1], buf1, r, sem).start(priority=1)
        return 0

    lax.fori_loop(0, tc, issue, 0)

    def drain(r, _):
        _row_copy(y_ref, 0, buf0, r, sem).wait()
        _row_copy(y_ref, 0, buf1, r, sem).wait()
        return 0

    lax.fori_loop(0, tc, drain, 0)
    route = route_ref[...]
    moe = route[:, 2:3] * buf0[...] + route[:, 3:4] * buf1[...]
    o_ref[...] = x_ref[...] + g_ref[pl.ds(i // tiles_per_batch, 1), :] * moe


def _combine(pos, y, x2, mod_l, g_chunk, route, seq):
    t, d = x2.shape
    tc = 128
    return pl.pallas_call(
        functools.partial(_combine_kernel, tc=tc, tiles_per_batch=seq // tc),
        grid_spec=pltpu.PrefetchScalarGridSpec(
            num_scalar_prefetch=1,
            grid=(t // tc,),
            in_specs=[pl.BlockSpec(memory_space=pl.ANY),
                      pl.BlockSpec((tc, d), lambda i, p: (i, 0)),
                      pl.BlockSpec((SUBLANES, d), lambda i, p: (0, g_chunk)),
                      pl.BlockSpec((tc, LANES), lambda i, p: (i, 0))],
            out_specs=pl.BlockSpec((tc, d), lambda i, p: (i, 0)),
            scratch_shapes=[pltpu.VMEM((tc, d), F32), pltpu.VMEM((tc, d), F32),
                            pltpu.SemaphoreType.DMA(())]),
        out_shape=jax.ShapeDtypeStruct((t, d), F32),
        compiler_params=_cparams(("arbitrary",), 2 * tc * d * 4 + 4 * tc * d * 4 + 4 * tc * d * 4),
        name="moe_combine",
    )(pos, y, x2, mod_l, route)


def _routing_tables(route, tm):
    t = route.shape[0]
    experts = route[:, :2].astype(jnp.int32).reshape(-1)
    onehot = (experts[:, None] == jnp.arange(N_EXPERTS)[None, :]).astype(jnp.int32)
    csum = jnp.cumsum(onehot, axis=0)
    rank = jnp.sum((csum - onehot) * onehot, axis=1)
    counts = csum[-1]
    padded = ((counts + tm - 1) // tm) * tm
    ends = jnp.cumsum(padded)
    starts = ends - padded
    dest = starts[experts] + rank
    n_rows = 2 * t + N_EXPERTS * tm
    row_token = jnp.zeros((n_rows,), jnp.int32).at[dest].set(jnp.arange(2 * t, dtype=jnp.int32) // 2)
    tile_start = jnp.arange(n_rows // tm, dtype=jnp.int32) * tm
    tile_expert = jnp.minimum(jnp.sum((tile_start[:, None] >= ends[None, :]).astype(jnp.int32), axis=1),
                              N_EXPERTS - 1)
    tile_valid = (tile_start < ends[-1]).astype(jnp.int32)
    real_end = (starts + counts)[tile_expert]
    tile_rows = jnp.clip(real_end - tile_start, 0, tm).astype(jnp.int32) * tile_valid
    return row_token, dest.astype(jnp.int32), tile_expert, tile_valid, tile_rows, n_rows


def _tiled(vec, reps):
    return jnp.tile(vec.astype(F32), reps)


def _mixer(h, seq, batch, w_in, w_branch, w_out, rel_bias, cmp_pos, cmp_w1, cmp_w2,
           nsa_q_g, nsa_k_g, diff_q_g, diff_k_g, diff_lam, diff_out_g, lam_init, x2, mod_l, d):
    scale = HEAD_DIM ** -0.5
    ones = lambda n: jnp.ones((n,), F32)
    zeros = lambda n: jnp.zeros((n,), F32)
    wb = lambda lo, hi: w_in[:, lo:hi].astype(BF16)

    cs = jnp.concatenate([jnp.full((BRANCH_W,), scale, F32), ones(2 * BRANCH_W)]).reshape(1, -1)
    p_sb = _matmul(h, wb(_C_SB, _C_NSQ), [(cs, "col")], _ep_scale, BF16, tm=1024, tn=512, name="proj_sb")
    nf = ones(BRANCH_W).reshape(1, -1)
    cs = (_tiled(nsa_q_g, N_HEADS) * scale).reshape(1, -1)
    p_nsq = _matmul(h, wb(_C_NSQ, _C_NSKV), [(nf, "col"), (cs, "col")],
                    functools.partial(_ep_norm, width=HEAD_DIM), BF16, tm=1024, tn=512, name="proj_nsq")
    nf = jnp.concatenate([zeros(2 * KV_W), ones(KV_W), zeros(KV_W), ones(KV_W), zeros(KV_W)]).reshape(1, -1)
    cs = jnp.concatenate([ones(2 * KV_W), _tiled(nsa_k_g[1], KV_HEADS), ones(KV_W),
                          _tiled(nsa_k_g[2], KV_HEADS), ones(KV_W)]).reshape(1, -1)
    p_nskv = _matmul(h, wb(_C_NSKV, _C_GNS), [(nf, "col"), (cs, "col")],
                     functools.partial(_ep_norm, width=HEAD_DIM), BF16, tm=1024, tn=512, name="proj_nskv")
    cols = np.array([_C_GNS + c * N_HEADS + Q_PER_KV * g + r
                     for g in range(KV_HEADS) for c in range(3) for r in range(Q_PER_KV)])
    wg = w_in[:, cols].reshape(d, KV_HEADS, 3 * Q_PER_KV)
    wg = jnp.pad(wg, ((0, 0), (0, 0), (0, LANES - 3 * Q_PER_KV))).reshape(d, KV_HEADS * LANES).astype(BF16)
    ns_gates = _matmul(h, wg, [], _ep_sigmoid, F32, tm=1024, tn=256, name="proj_nsgate")
    nf = jnp.concatenate([ones(2 * BRANCH_W), zeros(BRANCH_W)]).reshape(1, -1)
    cs = jnp.concatenate([_tiled(diff_q_g, 2 * N_HEADS) * DIFF_QK ** -0.5, _tiled(diff_k_g, 2 * N_HEADS),
                          ones(BRANCH_W)]).reshape(1, -1)
    p_df = _matmul(h, wb(_C_DF, _C_MERGE), [(nf, "col"), (cs, "col")],
                   functools.partial(_ep_norm, width=DIFF_QK), BF16, tm=1024, tn=512, name="proj_diff")
    m_gates = _matmul(h, wb(_C_MERGE, _C_MERGE + 3 * d), [], _ep_sigmoid, BF16, tm=1024, tn=512,
                      name="proj_merge_gate")

    o_sb = _sb_attention(p_sb, batch, seq)
    kvcm = _compress(p_nskv, cmp_pos, cmp_w1, cmp_w2, nsa_k_g[0], batch, seq)
    o_ns = _nsa_attention(p_nsq, kvcm, p_nskv, ns_gates, rel_bias[:, :N_HEADS], batch, seq)
    o_df = _diff_attention(p_df, rel_bias[:, N_HEADS:], diff_lam, diff_out_g, lam_init, batch, seq)
    merged = _branch_merge(o_sb, o_ns, o_df, w_branch.astype(BF16), m_gates)
    g1 = lax.slice_in_dim(mod_l, 2 * d, 3 * d, axis=1)
    return _matmul(merged, w_out.astype(BF16), [(x2, "tile"), (g1, "batch")], _ep_residual, F32,
                   tm=1024, tn=512, name="proj_out", seq=seq)


def _moe(x2, seq, mod_l, norm_gain, w_router, router_bias, w_gate, w_up, w_down):
    tm = 256
    h2, route = _norm_router(x2, norm_gain, mod_l, 3, 4, seq, w_router, router_bias)
    row_token, pos, tile_expert, tile_valid, tile_rows, n_rows = _routing_tables(route, tm)
    xs = _gather_rows(row_token, tile_rows, h2, n_rows, tm)
    y = _expert_mlps(tile_expert, tile_valid, xs, w_gate, w_up, w_down, tm)
    return _combine(pos, y, x2, mod_l, 5, route, seq)


def kernel(x, c, rel_bias, w_router, router_bias, w_ada, b_ada, norm_mix, norm_ffn, w_in, nsa_cmp_pos,
           nsa_cmp_w1, nsa_cmp_w2, nsa_q_norm, nsa_k_norm, diff_q_norm, diff_k_norm, diff_lambda,
           diff_out_norm, w_branch, w_out, w_exp_gate, w_exp_up, w_exp_down):
    batch, seq, d = x.shape
    depth = w_ada.shape[0]
    mod = _adaln(c, w_ada, b_ada)
    x2 = x.reshape(batch * seq, d)
    for layer in range(depth):
        lam_init = 0.8 - 0.6 * math.exp(-0.3 * layer)
        mod_l = mod[layer]
        h = _norm_mod(x2, norm_mix[layer], mod_l, 0, 1, seq)
        x2 = _mixer(h, seq, batch, w_in[layer], w_branch[layer], w_out[layer], rel_bias,
                    nsa_cmp_pos[layer], nsa_cmp_w1[layer], nsa_cmp_w2[layer], nsa_q_norm[layer],
                    nsa_k_norm[layer], diff_q_norm[layer], diff_k_norm[layer], diff_lambda[layer],
                    diff_out_norm[layer], lam_init, x2, mod_l, d)
        x2 = _moe(x2, seq, mod_l, norm_ffn[layer], w_router, router_bias, w_exp_gate[layer],
                  w_exp_up[layer], w_exp_down[layer])
    return x2.reshape(batch, seq, d)
```

```python
import functools
import math

import numpy as np
import jax
import jax.numpy as jnp
from jax import lax
from jax.experimental import pallas as pl
from jax.experimental.pallas import tpu as pltpu

F32 = jnp.float32
BF16 = jnp.bfloat16

HEAD_DIM = 128
N_HEADS = 8
KV_HEADS = 2
Q_PER_KV = N_HEADS // KV_HEADS
DIFF_QK = HEAD_DIM // 2
BRANCH_W = N_HEADS * HEAD_DIM
KV_W = KV_HEADS * HEAD_DIM
CMP_LEN = 32
CMP_STRIDE = 16
SLC_LEN = 64
N_SLC = 16
N_LOCAL_SLC = 2
WINDOW = 512
REL_BUCKETS = 32
REL_MAX_DIST = 128
N_EXPERTS = 16
N_GROUPS = 4
EXP_PER_GROUP = N_EXPERTS // N_GROUPS
NORM_EPS = 1e-6
NEG_INF = -1e30
MASKED_BELOW = -1e29
FORCE_SCORE = 1e9
LANES = 128
SUBLANES = 8
VMEM_CAP = 56 * 1024 * 1024
SB_EXIT = 100.0

_C_SB = 0
_C_NSQ = 3 * BRANCH_W
_C_NSKV = _C_NSQ + BRANCH_W
_C_GNS = _C_NSKV + 6 * KV_W
_C_DF = _C_GNS + 3 * N_HEADS
_C_MERGE = _C_DF + 3 * BRANCH_W


def _cparams(sem, vmem_bytes):
    lim = int(min(VMEM_CAP, max(vmem_bytes * 5 // 4 + (2 << 20), 16 << 20)))
    return pltpu.CompilerParams(dimension_semantics=sem, vmem_limit_bytes=lim)


def _dot(a, b):
    return jnp.dot(a, b, preferred_element_type=F32)


def _dot_nt(a, b):
    return lax.dot_general(a, b, (((1,), (1,)), ((), ())), preferred_element_type=F32)


def _split_bf16(x):
    hi = x.astype(BF16)
    lo = (x - hi.astype(F32)).astype(BF16)
    return hi, lo


def _ada_kernel(cb_ref, w_ref, b_ref, o_ref, ca_sc, *, nb, rows_per_iter):
    k_dim, tn = w_ref.shape

    @pl.when((pl.program_id(0) == 0) & (pl.program_id(1) == 0))
    def _():
        cb = cb_ref[...]
        ca_sc[...] = cb * jax.nn.sigmoid(cb)

    reps = tn // LANES
    groups = rows_per_iter // SUBLANES

    def body(i, accs):
        r0 = pl.multiple_of(i * rows_per_iter, rows_per_iter)
        w3 = w_ref[pl.ds(r0, rows_per_iter), :].reshape(groups, SUBLANES, tn)
        new = []
        for b in range(nb):
            ca = ca_sc[b, pl.ds(r0, rows_per_iter), :].reshape(groups, SUBLANES, LANES)
            ca = jnp.concatenate([ca] * reps, axis=2)
            new.append(accs[b] + jnp.sum(w3 * ca, axis=0))
        return tuple(new)

    accs = lax.fori_loop(0, k_dim // rows_per_iter, body,
                         tuple(jnp.zeros((SUBLANES, tn), F32) for _ in range(nb)))
    rows = [jnp.sum(a, axis=0, keepdims=True) + b_ref[...] for a in accs]
    rows.append(jnp.zeros((SUBLANES - nb, tn), F32))
    o_ref[...] = jnp.concatenate(rows, axis=0)


def _adaln(c, w_ada, b_ada):
    depth, d, n = w_ada.shape
    nb = c.shape[0]
    tn = 512
    cb = jnp.broadcast_to(c[:, :, None], (nb, d, LANES))
    vmem = 2 * d * tn * 4 + 3 * nb * d * LANES * 4
    return pl.pallas_call(
        functools.partial(_ada_kernel, nb=nb, rows_per_iter=64),
        grid=(depth, n // tn),
        in_specs=[pl.BlockSpec((nb, d, LANES), lambda l, j: (0, 0, 0)),
                  pl.BlockSpec((None, d, tn), lambda l, j: (l, 0, j)),
                  pl.BlockSpec((None, 1, tn), lambda l, j: (l, 0, j))],
        out_specs=pl.BlockSpec((None, SUBLANES, tn), lambda l, j: (l, 0, j)),
        out_shape=jax.ShapeDtypeStruct((depth, SUBLANES, n), F32),
        scratch_shapes=[pltpu.VMEM((nb, d, LANES), F32)],
        compiler_params=_cparams(("arbitrary", "arbitrary"), vmem),
        name="adaln_mod",
    )(cb, w_ada, b_ada.reshape(depth, 1, n))


def _modulated_norm(x_ref, gain_ref, sc_ref, sh_ref, tiles_per_batch):
    b = pl.program_id(0) // tiles_per_batch
    x = x_ref[...]
    ms = jnp.mean(x * x, axis=-1, keepdims=True)
    y = x * lax.rsqrt(ms + NORM_EPS) * gain_ref[...]
    return y * (1.0 + sc_ref[pl.ds(b, 1), :]) + sh_ref[pl.ds(b, 1), :]


def _norm_kernel(x_ref, gain_ref, sc_ref, sh_ref, o_ref, *, tiles_per_batch):
    o_ref[...] = _modulated_norm(x_ref, gain_ref, sc_ref, sh_ref, tiles_per_batch).astype(o_ref.dtype)


def _norm_mod(x2, gain, mod_l, sh_chunk, sc_chunk, seq):
    t, d = x2.shape
    tm = 256
    return pl.pallas_call(
        functools.partial(_norm_kernel, tiles_per_batch=seq // tm),
        grid=(t // tm,),
        in_specs=[pl.BlockSpec((tm, d), lambda i: (i, 0)),
                  pl.BlockSpec((1, d), lambda i: (0, 0)),
                  pl.BlockSpec((SUBLANES, d), lambda i: (0, sc_chunk)),
                  pl.BlockSpec((SUBLANES, d), lambda i: (0, sh_chunk))],
        out_specs=pl.BlockSpec((tm, d), lambda i: (i, 0)),
        out_shape=jax.ShapeDtypeStruct((t, d), BF16),
        compiler_params=_cparams(("arbitrary",), 2 * tm * d * 6 + 4 * SUBLANES * d * 4),
        name="norm_mix",
    )(x2, gain.reshape(1, d), mod_l, mod_l)


def _norm_router_kernel(x_ref, gain_ref, sc_ref, sh_ref, wr_ref, rb_ref, h_ref, route_ref,
                        *, tiles_per_batch):
    h = _modulated_norm(x_ref, gain_ref, sc_ref, sh_ref, tiles_per_batch)
    h_ref[...] = h
    h_hi, h_lo = _split_bf16(h)
    w_hi, w_lo = _split_bf16(wr_ref[...])
    logits = _dot(h_hi, w_hi) + _dot(h_hi, w_lo) + _dot(h_lo, w_hi)
    aff = jax.nn.sigmoid(logits)
    biased = aff + rb_ref[...]
    lane_i = lax.broadcasted_iota(jnp.int32, aff.shape, 1)
    lane = lane_i.astype(F32)
    low = jnp.float32(-3e38)
    big = jnp.float32(1 << 20)
    best = None
    for gi in range(N_GROUPS):
        in_group = (lane_i // EXP_PER_GROUP) == gi
        v = jnp.where(in_group, biased, low)
        m1 = jnp.max(v, axis=-1, keepdims=True)
        i1 = jnp.min(jnp.where(v == m1, lane, big), axis=-1, keepdims=True)
        v2 = jnp.where(lane == i1, low, v)
        m2 = jnp.max(v2, axis=-1, keepdims=True)
        i2 = jnp.min(jnp.where(v2 == m2, lane, big), axis=-1, keepdims=True)
        score = m1 + m2
        if best is None:
            best = (score, i1, i2)
        else:
            better = score > best[0]
            best = (jnp.where(better, score, best[0]), jnp.where(better, i1, best[1]),
                    jnp.where(better, i2, best[2]))
    _, e1, e2 = best
    a1 = jnp.sum(jnp.where(lane == e1, aff, 0.0), axis=-1, keepdims=True)
    a2 = jnp.sum(jnp.where(lane == e2, aff, 0.0), axis=-1, keepdims=True)
    den = a1 + a2
    out = jnp.where(lane_i == 0, e1,
                    jnp.where(lane_i == 1, e2,
                              jnp.where(lane_i == 2, a1 / den, jnp.where(lane_i == 3, a2 / den, 0.0))))
    route_ref[...] = out


def _norm_router(x2, gain, mod_l, sh_chunk, sc_chunk, seq, w_router, router_bias):
    t, d = x2.shape
    tm = 256
    wr = jnp.pad(w_router, ((0, 0), (0, LANES - N_EXPERTS)))
    rb = jnp.pad(router_bias, (0, LANES - N_EXPERTS)).reshape(1, LANES)
    return pl.pallas_call(
        functools.partial(_norm_router_kernel, tiles_per_batch=seq // tm),
        grid=(t // tm,),
        in_specs=[pl.BlockSpec((tm, d), lambda i: (i, 0)),
                  pl.BlockSpec((1, d), lambda i: (0, 0)),
                  pl.BlockSpec((SUBLANES, d), lambda i: (0, sc_chunk)),
                  pl.BlockSpec((SUBLANES, d), lambda i: (0, sh_chunk)),
                  pl.BlockSpec((d, LANES), lambda i: (0, 0)),
                  pl.BlockSpec((1, LANES), lambda i: (0, 0))],
        out_specs=[pl.BlockSpec((tm, d), lambda i: (i, 0)),
                   pl.BlockSpec((tm, LANES), lambda i: (i, 0))],
        out_shape=[jax.ShapeDtypeStruct((t, d), F32),
                   jax.ShapeDtypeStruct((t, LANES), F32)],
        compiler_params=_cparams(("arbitrary",), 2 * tm * d * 8 + 4 * d * LANES * 4 + 8 * tm * d),
        name="norm_ffn_router",
    )(x2, gain.reshape(1, d), mod_l, mod_l, wr, rb)


def _mm_kernel(a_ref, b_ref, *rest, epilogue, n_extra, tiles_per_batch):
    extras = rest[:n_extra]
    o_ref = rest[n_extra]
    acc = _dot(a_ref[...], b_ref[...])
    o_ref[...] = epilogue(acc, extras, pl.program_id(0) // tiles_per_batch).astype(o_ref.dtype)


def _matmul(a, b, extras, epilogue, out_dtype, *, tm, tn, name, seq=None):
    m, k = a.shape
    n = b.shape[1]
    tm = min(tm, m)
    tn = min(tn, n)
    assert m % tm == 0 and n % tn == 0
    in_specs = [pl.BlockSpec((tm, k), lambda i, j: (i, 0)),
                pl.BlockSpec((k, tn), lambda i, j: (0, j))]
    vmem = 2 * (tm * k + k * tn) * 2 + 2 * tm * tn * jnp.dtype(out_dtype).itemsize + 2 * tm * tn * 4
    args = []
    for arr, kind in extras:
        if kind == "col":
            in_specs.append(pl.BlockSpec((1, tn), lambda i, j: (0, j)))
            vmem += 2 * SUBLANES * tn * 4
        elif kind == "tile":
            in_specs.append(pl.BlockSpec((tm, tn), lambda i, j: (i, j)))
            vmem += 2 * tm * tn * arr.dtype.itemsize
        else:
            in_specs.append(pl.BlockSpec((SUBLANES, tn), lambda i, j: (0, j)))
            vmem += 2 * SUBLANES * tn * 4
        args.append(arr)
    tiles_per_batch = (seq // tm) if seq else 1
    return pl.pallas_call(
        functools.partial(_mm_kernel, epilogue=epilogue, n_extra=len(extras),
                          tiles_per_batch=tiles_per_batch),
        grid=(m // tm, n // tn),
        in_specs=in_specs,
        out_specs=pl.BlockSpec((tm, tn), lambda i, j: (i, j)),
        out_shape=jax.ShapeDtypeStruct((m, n), out_dtype),
        compiler_params=_cparams(("arbitrary", "arbitrary"), vmem),
        name=name,
    )(a, b, *args)


def _mm_w32_kernel(a_ref, *rest, epilogue, n_extra, tiles_per_batch, shift):
    n_w = 2 if shift else 1
    extras = rest[n_w:n_w + n_extra]
    o_ref, w_sc = rest[n_w + n_extra], rest[n_w + n_extra + 1]

    @pl.when(pl.program_id(1) == 0)
    def _():
        w_ref = rest[0]
        if not shift:
            w_sc[...] = w_ref[...].astype(BF16)
            return
        pieces = w_sc.shape[1] // LANES
        for c in range(pieces):
            left = w_ref[:, c * LANES:(c + 1) * LANES]
            right = w_ref[:, (c + 1) * LANES:(c + 2) * LANES] if c + 1 < pieces else rest[1][...]
            both = jnp.concatenate([left, right], axis=1)
            w_sc[:, c * LANES:(c + 1) * LANES] = both[:, shift:shift + LANES].astype(BF16)

    acc = _dot(a_ref[...], w_sc[...])
    o_ref[...] = epilogue(acc, extras, pl.program_id(1) // tiles_per_batch).astype(o_ref.dtype)


def _matmul_w32(a, w, lead, col_lo, n, extras, epilogue, out_dtype, *, tm, tn, name, seq=None):
    m, k = a.shape
    tm = min(tm, m)
    tn = min(tn, n)
    shift = col_lo % LANES
    base = col_lo - shift
    assert m % tm == 0 and n % tn == 0 and base % tn == 0
    none = (None,) * len(lead)
    in_specs = [pl.BlockSpec((tm, k), lambda j, i: (i, 0)),
                pl.BlockSpec(none + (k, tn), lambda j, i: lead + (0, base // tn + j))]
    args = [a, w]
    vmem = 2 * tm * k * 2 + 2 * k * tn * 4 + k * tn * 2 + 2 * tm * tn * jnp.dtype(out_dtype).itemsize + 2 * tm * tn * 4
    if shift:
        per = tn // LANES
        in_specs.append(pl.BlockSpec(none + (k, LANES), lambda j, i: lead + (0, (base + tn) // LANES + j * per)))
        args.append(w)
        vmem += 2 * k * LANES * 4 + 3 * k * 2 * LANES * 4
    for arr, kind in extras:
        if kind == "tile":
            in_specs.append(pl.BlockSpec((tm, tn), lambda j, i: (i, j)))
            vmem += 2 * tm * tn * arr.dtype.itemsize
        else:
            rows = 1 if kind == "col" else SUBLANES
            in_specs.append(pl.BlockSpec((rows, tn), lambda j, i: (0, j)))
            vmem += 2 * SUBLANES * tn * 4
        args.append(arr)
    tiles_per_batch = (seq // tm) if seq else 1
    return pl.pallas_call(
        functools.partial(_mm_w32_kernel, epilogue=epilogue, n_extra=len(extras),
                          tiles_per_batch=tiles_per_batch, shift=shift),
        grid=(n // tn, m // tm),
        in_specs=in_specs,
        out_specs=pl.BlockSpec((tm, tn), lambda j, i: (i, j)),
        out_shape=jax.ShapeDtypeStruct((m, n), out_dtype),
        scratch_shapes=[pltpu.VMEM((k, tn), BF16)],
        compiler_params=_cparams(("arbitrary", "arbitrary"), vmem),
        name=name,
    )(*args)


def _ep_scale(acc, extras, b):
    return acc * extras[0][...]


def _ep_sigmoid(acc, extras, b):
    return jax.nn.sigmoid(acc)


def _ep_norm(acc, extras, b, *, width):
    nf_ref, cs_ref = extras
    outs = []
    for c in range(acc.shape[1] // LANES):
        sl = slice(c * LANES, (c + 1) * LANES)
        blk = acc[:, sl]
        sq = blk * blk
        if width == LANES:
            ms = jnp.mean(sq, axis=-1, keepdims=True)
        else:
            lane = lax.broadcasted_iota(jnp.int32, blk.shape, 1)
            first = lane < width
            lo = jnp.sum(jnp.where(first, sq, 0.0), axis=-1, keepdims=True) / width
            hi = jnp.sum(jnp.where(first, 0.0, sq), axis=-1, keepdims=True) / width
            ms = jnp.where(first, lo, hi)
        inv = lax.rsqrt(ms + NORM_EPS)
        f = jnp.where(nf_ref[:, sl] > 0.0, inv, 1.0)
        outs.append(blk * f * cs_ref[:, sl])
    return jnp.concatenate(outs, axis=1)


def _ep_residual(acc, extras, b):
    x_ref, g_ref = extras
    return x_ref[...] + g_ref[pl.ds(b, 1), :] * acc


def _bucket_np(dist):
    n = np.maximum(dist, 0)
    max_exact = REL_BUCKETS // 2
    nf = np.maximum(n, max_exact).astype(np.float32)
    large = max_exact + (np.log(nf / np.float32(max_exact))
                         / np.float32(math.log(REL_MAX_DIST / max_exact))
                         * np.float32(REL_BUCKETS - max_exact)).astype(np.int32)
    large = np.minimum(large, REL_BUCKETS - 1)
    return np.where(n < max_exact, n, large).astype(np.int32)


def _bucket_starts():
    b = _bucket_np(np.arange(4 * REL_MAX_DIST))
    return [int(np.argmax(b >= k)) for k in range(1, REL_BUCKETS)]


def _delta_bias(table, dist, valid):
    tb = table.T.astype(F32)
    shape = (tb.shape[0],) + (1,) * dist.ndim
    val = jnp.broadcast_to(tb[:, 0].reshape(shape), (tb.shape[0],) + dist.shape)
    for k, start in enumerate(_bucket_starts(), start=1):
        val = jnp.where(dist[None] >= start, tb[:, k].reshape(shape), val)
    val = val - tb[:, REL_BUCKETS - 1].reshape(shape)
    return jnp.where(valid[None], val, NEG_INF)


def _toeplitz_dist(n_off, step, nr, nc):
    shape = (n_off, nr, nc)
    return (lax.broadcasted_iota(jnp.int32, shape, 0) * step + lax.broadcasted_iota(jnp.int32, shape, 1)
            - lax.broadcasted_iota(jnp.int32, shape, 2))


def _stack_group_heads(tiles):
    nh, k, nr, nc = tiles.shape
    t = tiles.reshape(KV_HEADS, Q_PER_KV, k, nr, nc)
    return jnp.transpose(t, (0, 2, 1, 3, 4)).reshape(KV_HEADS, k, Q_PER_KV * nr, nc)


def _sb_kernel(q_ref, k_ref, v_ref, o_ref, *, tq):
    qi = pl.program_id(2)
    q = q_ref[...]
    row = lax.broadcasted_iota(jnp.int32, (tq, tq), 0)
    col = lax.broadcasted_iota(jnp.int32, (tq, tq), 1)
    later = jnp.where(row > col, 1.0, 0.0).astype(BF16)
    causal = col < row

    def step(kj, carry, acc, diag):
        k0 = pl.multiple_of(kj * tq, tq)
        k = k_ref[pl.ds(k0, tq), :]
        v = v_ref[pl.ds(k0, tq), :]
        z = _dot_nt(q, k)
        log_succ = jnp.minimum(z, 0.0) - jnp.log1p(jnp.exp(-jnp.abs(z)))
        log_fail = log_succ - z
        if diag:
            log_fail = jnp.where(causal, log_fail, 0.0)
        hi, lo = _split_bf16(log_fail)
        suffix = _dot(hi, later) + _dot(lo, later)
        a = jnp.exp(log_succ + suffix + carry)
        if diag:
            a = jnp.where(causal, a, 0.0)
        acc = acc + _dot(a.astype(BF16), v)
        carry = carry + jnp.sum(log_fail, axis=-1, keepdims=True)
        return carry, acc

    carry, acc = step(qi, jnp.zeros((tq, 1), F32), jnp.zeros((tq, HEAD_DIM), F32), True)

    def cond(state):
        kj, carry, _ = state
        return (kj >= 0) & (jnp.max(carry) > -SB_EXIT)

    def body(state):
        kj, carry, acc = state
        carry, acc = step(kj, carry, acc, False)
        return kj - 1, carry, acc

    _, _, acc = lax.while_loop(cond, body, (qi - 1, carry, acc))
    o_ref[...] = acc.astype(o_ref.dtype)


def _sb_attention(p_sb, batch, seq):
    tq = min(256, seq)
    nq = seq // tq
    t = batch * seq
    vmem = 4 * seq * HEAD_DIM * 2 + 4 * tq * HEAD_DIM * 2 + 16 * tq * tq * 4
    return pl.pallas_call(
        functools.partial(_sb_kernel, tq=tq),
        grid=(batch, N_HEADS, nq),
        in_specs=[pl.BlockSpec((tq, HEAD_DIM), lambda b, h, i: (b * nq + i, h)),
                  pl.BlockSpec((seq, HEAD_DIM), lambda b, h, i: (b, N_HEADS + h)),
                  pl.BlockSpec((seq, HEAD_DIM), lambda b, h, i: (b, 2 * N_HEADS + h))],
        out_specs=pl.BlockSpec((tq, HEAD_DIM), lambda b, h, i: (b * nq + i, h)),
        out_shape=jax.ShapeDtypeStruct((t, BRANCH_W), BF16),
        compiler_params=_cparams(("arbitrary",) * 3, vmem),
        name="sb_attention",
    )(p_sb, p_sb, p_sb)


LOGIT_CAP = 40.0


def _with_ones(v):
    return jnp.concatenate([v, jnp.ones_like(v)], axis=1)


def _softmax_first(s, v, m_sc, acc_sc, bounded):
    if bounded:
        acc_sc[...] = _dot(jnp.exp(s).astype(BF16), _with_ones(v))
        return
    m = jnp.max(s, axis=-1, keepdims=True)
    m_sc[...] = m
    acc_sc[...] = _dot(jnp.exp(s - m).astype(BF16), _with_ones(v))


def _softmax_next(s, v, m_sc, acc_sc, bounded):
    if bounded:
        acc_sc[...] = acc_sc[...] + _dot(jnp.exp(s).astype(BF16), _with_ones(v))
        return
    m_old = m_sc[...]
    m_new = jnp.maximum(m_old, jnp.max(s, axis=-1, keepdims=True))
    p = jnp.exp(s - m_new)
    acc_sc[...] = jnp.exp(m_old - m_new) * acc_sc[...] + _dot(p.astype(BF16), _with_ones(v))
    m_sc[...] = m_new


def _softmax_result(acc_sc, bounded):
    acc = acc_sc[...]
    total = acc[:, HEAD_DIM:]
    den = jnp.where(total > 0.0, total, 1.0) if bounded else jnp.maximum(total, 1.0)
    return acc[:, :HEAD_DIM] / den


def _logit_bound_ok(width, scale, q_gain, k_gains, table):
    qk = width * scale * jnp.max(jnp.abs(q_gain)) * jnp.max(jnp.abs(k_gains)) * 1.02
    delta = jnp.max(jnp.abs(table - table[REL_BUCKETS - 1:]))
    return (qk + delta <= LOGIT_CAP).astype(jnp.int32).reshape(1)


def _compress_kernel(x_ref, p_ref, w1a_ref, w1b_ref, w2_ref, gain_ref, o_ref):
    nc = x_ref.shape[0]
    is_key = pl.program_id(1) < KV_HEADS
    x = x_ref[...].astype(F32)
    first = _dot((x + p_ref[0:1, :]).astype(BF16), w1a_ref[...])
    second = _dot((x + p_ref[1:2, :]).astype(BF16), w1b_ref[...])
    pre = first + pltpu.roll(second, nc - 1, 0)
    hid = pre * jax.nn.sigmoid(pre)
    out = _dot(hid.astype(BF16), w2_ref[...])
    ms = jnp.mean(out * out, axis=-1, keepdims=True)
    normed = out * lax.rsqrt(ms + NORM_EPS) * gain_ref[...]
    o_ref[...] = jnp.where(is_key, normed, out).astype(o_ref.dtype)


def _compress(p_nskv, cmp_pos, cmp_w1, cmp_w2, k_gain0, batch, seq):
    nc = seq // CMP_STRIDE
    half = CMP_STRIDE * HEAD_DIM
    x = p_nskv[:, :2 * KV_W].reshape(batch, nc, CMP_STRIDE, 2 * KV_HEADS, HEAD_DIM)
    x = jnp.transpose(x, (0, 3, 1, 2, 4)).reshape(batch, 2 * KV_HEADS, nc, half)
    pos = cmp_pos.reshape(2, 2, half)
    w1 = cmp_w1.astype(BF16).reshape(2, 2, half, HEAD_DIM)
    w2 = cmp_w2.astype(BF16)
    return pl.pallas_call(
        _compress_kernel,
        grid=(batch, 2 * KV_HEADS),
        in_specs=[pl.BlockSpec((None, None, nc, half), lambda b, j: (b, j, 0, 0)),
                  pl.BlockSpec((None, 2, half), lambda b, j: (j // KV_HEADS, 0, 0)),
                  pl.BlockSpec((None, None, half, HEAD_DIM), lambda b, j: (j // KV_HEADS, 0, 0, 0)),
                  pl.BlockSpec((None, None, half, HEAD_DIM), lambda b, j: (j // KV_HEADS, 1, 0, 0)),
                  pl.BlockSpec((None, HEAD_DIM, HEAD_DIM), lambda b, j: (j // KV_HEADS, 0, 0)),
                  pl.BlockSpec((1, HEAD_DIM), lambda b, j: (0, 0))],
        out_specs=pl.BlockSpec((None, None, nc, HEAD_DIM), lambda b, j: (b, j, 0, 0)),
        out_shape=jax.ShapeDtypeStruct((batch, 2 * KV_HEADS, nc, HEAD_DIM), BF16),
        compiler_params=_cparams(("arbitrary", "arbitrary"), 8 * nc * half * 2 + 8 * half * HEAD_DIM * 2),
        name="nsa_compress",
    )(x, pos, w1, w1, w2, k_gain0.reshape(1, HEAD_DIM))


def _nsa_kernel(ok_ref, q_ref, kcm_ref, vcm_ref, ks_ref, vs_ref, kw_ref, vw_ref, gate_ref,
                pt_ref, bt_ref, wt_ref, o_ref, m_sc, acc_sc, *, tq, tk, tw, n_blk, k_sel):
    qi = pl.program_id(2)
    q0 = qi * tq
    rows = Q_PER_KV * tq
    nc = kcm_ref.shape[0]
    q4 = jnp.concatenate([q_ref[:, r * HEAD_DIM:(r + 1) * HEAD_DIM] for r in range(Q_PER_KV)], axis=0)

    per_tile = tq // CMP_STRIDE
    lead = 2 * per_tile
    shift = lax.rem(qi * per_tile + (nc - lead), nc)
    cb = pltpu.roll(pt_ref[...], shift, 1)
    rel = lax.broadcasted_iota(jnp.int32, (rows, nc), 1) - qi * per_tile
    cb = jnp.where(rel < -lead, 0.0, jnp.where(rel >= per_tile, NEG_INF, cb))
    s = _dot_nt(q4, kcm_ref[...]) + cb
    valid = s > MASKED_BELOW
    s = jnp.where(valid, s, NEG_INF)
    p = jnp.where(valid, jnp.exp(s - jnp.max(s, axis=-1, keepdims=True)), 0.0)
    p = p / jnp.maximum(jnp.sum(p, axis=-1, keepdims=True), 1.0)
    o_cmp = _dot(p.astype(BF16), vcm_ref[...])

    psum = p[0:tq]
    for r in range(1, Q_PER_KV):
        psum = psum + p[r * tq:(r + 1) * tq]
    n_id = lax.broadcasted_iota(jnp.int32, (nc, LANES), 0)
    j_id = lax.broadcasted_iota(jnp.int32, (nc, LANES), 1)
    overlap = ((CMP_STRIDE * n_id < SLC_LEN * j_id + SLC_LEN)
               & (CMP_STRIDE * n_id + CMP_LEN > SLC_LEN * j_id) & (j_id < n_blk))
    overlap = jnp.where(overlap, 1.0, 0.0).astype(BF16)
    p_hi, p_lo = _split_bf16(psum)
    imp = _dot(p_hi, overlap) + _dot(p_lo, overlap)
    imp_t = jnp.transpose(imp)
    blk = lax.broadcasted_iota(jnp.int32, (LANES, tq), 0)
    cur = (q0 + lax.broadcasted_iota(jnp.int32, (LANES, tq), 1)) // SLC_LEN
    forced = (blk == 0) | ((blk <= cur) & (blk > cur - N_LOCAL_SLC))
    imp_t = jnp.where(forced, FORCE_SCORE, jnp.where(blk > cur, NEG_INF, imp_t))
    imp_t = jnp.where(blk < n_blk, imp_t, -3e38)
    rank = jnp.zeros((LANES, tq), F32)
    for j in range(n_blk):
        cj = imp_t[j:j + 1, :]
        ahead = jnp.where(blk > j, jnp.where(cj >= imp_t, 1.0, 0.0), jnp.where(cj > imp_t, 1.0, 0.0))
        rank = rank + ahead
    sel_t = jnp.where((rank < k_sel) & (blk < n_blk), 1.0, 0.0)
    sel = jnp.transpose(sel_t).astype(BF16)

    per = tk // tq
    kd = qi // per
    par = qi - kd * per
    e_row = lax.broadcasted_iota(jnp.int32, (LANES, tk), 0)
    e_col = lax.broadcasted_iota(jnp.int32, (LANES, tk), 1)

    def slc_scores(kj, bias):
        k0 = pl.multiple_of(kj * tk, tk)
        expand = jnp.where((k0 + e_col) // SLC_LEN == e_row, 1.0, 0.0).astype(BF16)
        chosen = _dot(sel, expand)
        s = _dot_nt(q4, ks_ref[pl.ds(k0, tk), :])
        if bias is not None:
            s = s + bias
        s = jnp.where(chosen[None] > 0.5, s.reshape(Q_PER_KV, tq, tk), NEG_INF).reshape(rows, tk)
        return s, vs_ref[pl.ds(k0, tk), :]

    def attend(bounded):
        s, v = slc_scores(kd, bt_ref[par])
        _softmax_first(s, v, m_sc, acc_sc, bounded)

        @pl.when(kd >= 1)
        def _():
            s, v = slc_scores(kd - 1, bt_ref[per + par])
            _softmax_next(s, v, m_sc, acc_sc, bounded)

        def slc_far(i, _):
            s, v = slc_scores(kd - 2 - i, None)
            _softmax_next(s, v, m_sc, acc_sc, bounded)
            return 0

        lax.fori_loop(0, jnp.maximum(kd - 1, 0), slc_far, 0)
        o_slc = _softmax_result(acc_sc, bounded)

        start = pl.multiple_of(jnp.maximum(q0 - WINDOW, 0), tq)
        s = _dot_nt(q4, kw_ref[pl.ds(start, tw), :]) + wt_ref[jnp.minimum(qi, WINDOW // tq)]
        _softmax_first(s, vw_ref[pl.ds(start, tw), :], m_sc, acc_sc, bounded)
        o_win = _softmax_result(acc_sc, bounded)

        gate = gate_ref[...]
        for r in range(Q_PER_KV):
            sl = slice(r * tq, (r + 1) * tq)
            o = (gate[:, r:r + 1] * o_cmp[sl] + gate[:, Q_PER_KV + r:Q_PER_KV + r + 1] * o_slc[sl]
                 + gate[:, 2 * Q_PER_KV + r:2 * Q_PER_KV + r + 1] * o_win[sl])
            o_ref[:, r * HEAD_DIM:(r + 1) * HEAD_DIM] = o.astype(o_ref.dtype)

    @pl.when(ok_ref[0] > 0)
    def _():
        attend(True)

    @pl.when(ok_ref[0] == 0)
    def _():
        attend(False)


def _nsa_attention(p_nsq, kvcm, p_nskv, gates, rel_nsa, q_gain, k_gains, batch, seq):
    tq = 128
    tk = min(512, seq)
    tw = WINDOW + tq
    nq = seq // tq
    nc = seq // CMP_STRIDE
    n_blk = seq // SLC_LEN
    k_sel = min(N_SLC, n_blk)
    t = batch * seq
    rows = Q_PER_KV * tq
    per = tk // tq
    assert n_blk <= LANES and nc % LANES == 0 and WINDOW % tq == 0 and tw <= seq and tq == LANES

    d = _toeplitz_dist(2 * per, tq, tq, tk)
    bt = _stack_group_heads(_delta_bias(rel_nsa, d, d >= 0))
    d = _toeplitz_dist(WINDOW // tq + 1, tq, tq, tw)
    wt = _stack_group_heads(_delta_bias(rel_nsa, d, (d >= 0) & (d < WINDOW)))
    per_tile = tq // CMP_STRIDE
    d = (lax.broadcasted_iota(jnp.int32, (1, tq, nc), 1)
         - CMP_STRIDE * (lax.broadcasted_iota(jnp.int32, (1, tq, nc), 2) - 2 * per_tile) - (CMP_LEN - 1))
    pt = _stack_group_heads(_delta_bias(rel_nsa, d, d >= 0))[:, 0]

    ok = _logit_bound_ok(HEAD_DIM, HEAD_DIM ** -0.5, q_gain, k_gains[1:], rel_nsa)
    const = lambda shape: pl.BlockSpec((None,) + shape, lambda b, g, i, ok: (g,) + (0,) * len(shape),
                                       pipeline_mode=pl.Buffered(1))
    vmem = (8 * seq * HEAD_DIM * 2 + (bt.size + wt.size + pt.size) // KV_HEADS * 4
            + 10 * rows * tw * 4 + 8 * rows * nc * 4)
    return pl.pallas_call(
        functools.partial(_nsa_kernel, tq=tq, tk=tk, tw=tw, n_blk=n_blk, k_sel=k_sel),
        grid_spec=pltpu.PrefetchScalarGridSpec(
            num_scalar_prefetch=1,
            grid=(batch, KV_HEADS, nq),
            in_specs=[pl.BlockSpec((tq, Q_PER_KV * HEAD_DIM), lambda b, g, i, ok: (b * nq + i, g)),
                      pl.BlockSpec((None, None, nc, HEAD_DIM), lambda b, g, i, ok: (b, g, 0, 0)),
                      pl.BlockSpec((None, None, nc, HEAD_DIM), lambda b, g, i, ok: (b, KV_HEADS + g, 0, 0)),
                      pl.BlockSpec((seq, HEAD_DIM), lambda b, g, i, ok: (b, 2 * KV_HEADS + g)),
                      pl.BlockSpec((seq, HEAD_DIM), lambda b, g, i, ok: (b, 3 * KV_HEADS + g)),
                      pl.BlockSpec((seq, HEAD_DIM), lambda b, g, i, ok: (b, 4 * KV_HEADS + g)),
                      pl.BlockSpec((seq, HEAD_DIM), lambda b, g, i, ok: (b, 5 * KV_HEADS + g)),
                      pl.BlockSpec((tq, LANES), lambda b, g, i, ok: (b * nq + i, g)),
                      const(pt.shape[1:]), const(bt.shape[1:]), const(wt.shape[1:])],
            out_specs=pl.BlockSpec((tq, Q_PER_KV * HEAD_DIM), lambda b, g, i, ok: (b * nq + i, g)),
            scratch_shapes=[pltpu.VMEM((rows, 1), F32), pltpu.VMEM((rows, 2 * HEAD_DIM), F32)]),
        out_shape=jax.ShapeDtypeStruct((t, BRANCH_W), BF16),
        compiler_params=_cparams(("arbitrary",) * 3, vmem),
        name="nsa_attention",
    )(ok, p_nsq, kvcm, kvcm, p_nskv, p_nskv, p_nskv, p_nskv, gates, pt, bt, wt)


def _diff_kernel(ok_ref, q_ref, k_ref, v_ref, dt_ref, lam_ref, og_ref, o_ref, m_sc, acc_sc,
                 *, tq, tk, lam_init):
    qi = pl.program_id(2)
    per = tk // tq
    kd = qi // per
    par = qi - kd * per
    q = q_ref[...]
    lane = lax.broadcasted_iota(jnp.int32, q.shape, 1)
    zero = jnp.zeros_like(q)
    q2 = jnp.concatenate([jnp.where(lane < DIFF_QK, q, zero), jnp.where(lane < DIFF_QK, zero, q)], axis=0)

    def scores(kj, bias):
        k0 = pl.multiple_of(kj * tk, tk)
        s = _dot_nt(q2, k_ref[pl.ds(k0, tk), :])
        if bias is not None:
            s = s + bias
        return s, v_ref[pl.ds(k0, tk), :]

    def attend(bounded):
        s, v = scores(kd, dt_ref[par])
        _softmax_first(s, v, m_sc, acc_sc, bounded)

        @pl.when(kd >= 1)
        def _():
            s, v = scores(kd - 1, dt_ref[per + par])
            _softmax_next(s, v, m_sc, acc_sc, bounded)

        def far(i, _):
            s, v = scores(kd - 2 - i, None)
            _softmax_next(s, v, m_sc, acc_sc, bounded)
            return 0

        lax.fori_loop(0, jnp.maximum(kd - 1, 0), far, 0)
        o2 = _softmax_result(acc_sc, bounded)
        lp = lam_ref[...]
        lam = (jnp.exp(jnp.sum(lp[0:1] * lp[1:2], axis=-1, keepdims=True))
               - jnp.exp(jnp.sum(lp[2:3] * lp[3:4], axis=-1, keepdims=True)) + lam_init)
        o = o2[:tq] - lam * o2[tq:]
        ms = jnp.mean(o * o, axis=-1, keepdims=True)
        o = o * lax.rsqrt(ms + NORM_EPS) * og_ref[...]
        o_ref[...] = (o * (1.0 - lam_init)).astype(o_ref.dtype)

    @pl.when(ok_ref[0] > 0)
    def _():
        attend(True)

    @pl.when(ok_ref[0] == 0)
    def _():
        attend(False)


def _diff_attention(p_df, rel_diff, lam_params, out_gain, q_gain, k_gain, lam_init, batch, seq):
    tq = min(256, seq)
    tk = min(512, seq)
    per = tk // tq
    nq = seq // tq
    t = batch * seq
    d = _toeplitz_dist(2 * per, tq, tq, tk)
    dt = _delta_bias(rel_diff, d, d >= 0)
    dt = jnp.concatenate([dt, dt], axis=2)
    ok = _logit_bound_ok(DIFF_QK, DIFF_QK ** -0.5, q_gain, k_gain, rel_diff)
    vmem = 4 * seq * HEAD_DIM * 2 + 2 * 2 * per * 2 * tq * tk * 4 + 12 * 2 * tq * tk * 4
    return pl.pallas_call(
        functools.partial(_diff_kernel, tq=tq, tk=tk, lam_init=lam_init),
        grid_spec=pltpu.PrefetchScalarGridSpec(
            num_scalar_prefetch=1,
            grid=(batch, N_HEADS, nq),
            in_specs=[pl.BlockSpec((tq, HEAD_DIM), lambda b, h, i, ok: (b * nq + i, h)),
                      pl.BlockSpec((seq, HEAD_DIM), lambda b, h, i, ok: (b, N_HEADS + h)),
                      pl.BlockSpec((seq, HEAD_DIM), lambda b, h, i, ok: (b, 2 * N_HEADS + h)),
                      pl.BlockSpec((None, 2 * per, 2 * tq, tk), lambda b, h, i, ok: (h, 0, 0, 0)),
                      pl.BlockSpec((4, DIFF_QK), lambda b, h, i, ok: (0, 0)),
                      pl.BlockSpec((1, HEAD_DIM), lambda b, h, i, ok: (0, 0))],
            out_specs=pl.BlockSpec((tq, HEAD_DIM), lambda b, h, i, ok: (b * nq + i, h)),
            scratch_shapes=[pltpu.VMEM((2 * tq, 1), F32), pltpu.VMEM((2 * tq, 2 * HEAD_DIM), F32)]),
        out_shape=jax.ShapeDtypeStruct((t, BRANCH_W), BF16),
        compiler_params=_cparams(("arbitrary",) * 3, vmem),
        name="diff_attention",
    )(ok, p_df, p_df, p_df, dt, lam_params, out_gain.reshape(1, HEAD_DIM))


def _merge_kernel(o0_ref, o1_ref, o2_ref, w_ref, g0_ref, g1_ref, g2_ref, out_ref, w_sc):
    @pl.when(pl.program_id(1) == 0)
    def _():
        w_sc[...] = w_ref[...].astype(BF16)

    acc = g0_ref[...].astype(F32) * _dot(o0_ref[...], w_sc[0])
    acc = acc + g1_ref[...].astype(F32) * _dot(o1_ref[...], w_sc[1])
    acc = acc + g2_ref[...].astype(F32) * _dot(o2_ref[...], w_sc[2])
    out_ref[...] = acc.astype(out_ref.dtype)


def _branch_merge(o_sb, o_ns, o_df, w_branch, layer, gates):
    t = o_sb.shape[0]
    d = w_branch.shape[3]
    tm = min(1024, t)
    tn = min(512, d)
    nj = d // tn
    o_spec = pl.BlockSpec((tm, BRANCH_W), lambda j, i: (i, 0))
    vmem = (2 * (3 * tm * BRANCH_W * 2 + 3 * BRANCH_W * tn * 4 + 3 * tm * tn * 2 + tm * tn * 2)
            + 3 * BRANCH_W * tn * 2 + 4 * tm * tn * 4)
    return pl.pallas_call(
        _merge_kernel,
        grid=(nj, t // tm),
        in_specs=[o_spec, o_spec, o_spec,
                  pl.BlockSpec((None, 3, BRANCH_W, tn), lambda j, i: (layer, 0, 0, j)),
                  pl.BlockSpec((tm, tn), lambda j, i: (i, j)),
                  pl.BlockSpec((tm, tn), lambda j, i: (i, nj + j)),
                  pl.BlockSpec((tm, tn), lambda j, i: (i, 2 * nj + j))],
        out_specs=pl.BlockSpec((tm, tn), lambda j, i: (i, j)),
        out_shape=jax.ShapeDtypeStruct((t, d), BF16),
        scratch_shapes=[pltpu.VMEM((3, BRANCH_W, tn), BF16)],
        compiler_params=_cparams(("arbitrary", "arbitrary"), vmem),
        name="branch_merge",
    )(o_sb, o_ns, o_df, w_branch, gates, gates, gates)


def _row_copy(src_ref, src_row, dst_ref, dst_row, sem):
    return pltpu.make_async_copy(src_ref.at[pl.ds(src_row, 1)], dst_ref.at[pl.ds(dst_row, 1)], sem)


def _gather_kernel(idx_ref, cnt_ref, src_ref, o_ref, buf, sem, *, tg):
    i = pl.program_id(0)
    base = i * tg
    n = cnt_ref[i]

    @pl.when(i == 0)
    def _():
        buf[...] = jnp.zeros(buf.shape, buf.dtype)

    for parity in range(2):
        def issue(r2, _, parity=parity):
            r = 2 * r2 + parity
            _row_copy(src_ref, idx_ref[base + r], buf, r, sem).start(priority=parity)
            return 0

        lax.fori_loop(0, (n + 1 - parity) // 2, issue, 0)

    def drain(r, _):
        _row_copy(src_ref, 0, buf, r, sem).wait()
        return 0

    lax.fori_loop(0, n, drain, 0)
    o_ref[...] = buf[...].astype(o_ref.dtype)


def _gather_rows(row_token, tile_rows, h2, n_rows, tg):
    d = h2.shape[1]
    return pl.pallas_call(
        functools.partial(_gather_kernel, tg=tg),
        grid_spec=pltpu.PrefetchScalarGridSpec(
            num_scalar_prefetch=2,
            grid=(n_rows // tg,),
            in_specs=[pl.BlockSpec(memory_space=pl.ANY)],
            out_specs=pl.BlockSpec((tg, d), lambda i, idx, cnt: (i, 0)),
            scratch_shapes=[pltpu.VMEM((tg, d), F32), pltpu.SemaphoreType.DMA(())]),
        out_shape=jax.ShapeDtypeStruct((n_rows, d), BF16),
        compiler_params=_cparams(("arbitrary",), tg * d * 4 + 2 * tg * d * 2),
        name="moe_gather",
    )(row_token, tile_rows, h2)


def _expert_changed(te_ref, t):
    return (t == 0) | (te_ref[t] != te_ref[jnp.maximum(t - 1, 0)])


def _expert_up_kernel(te_ref, tv_ref, x_ref, wg_ref, wu_ref, o_ref, wg_sc, wu_sc):
    t = pl.program_id(1)

    @pl.when(_expert_changed(te_ref, t))
    def _():
        wg_sc[...] = wg_ref[...].astype(BF16)
        wu_sc[...] = wu_ref[...].astype(BF16)

    @pl.when(tv_ref[t] > 0)
    def _():
        x = x_ref[...]
        g = _dot(x, wg_sc[...])
        u = _dot(x, wu_sc[...])
        o_ref[...] = (g * jax.nn.sigmoid(g) * u).astype(o_ref.dtype)

    @pl.when(tv_ref[t] == 0)
    def _():
        o_ref[...] = jnp.zeros(o_ref.shape, o_ref.dtype)


def _expert_down_kernel(te_ref, tv_ref, h_ref, wd_ref, o_ref, wd_sc):
    t = pl.program_id(1)

    @pl.when(_expert_changed(te_ref, t))
    def _():
        wd_sc[...] = wd_ref[...].astype(BF16)

    @pl.when(tv_ref[t] > 0)
    def _():
        o_ref[...] = _dot(h_ref[...], wd_sc[...])

    @pl.when(tv_ref[t] == 0)
    def _():
        o_ref[...] = jnp.zeros(o_ref.shape, o_ref.dtype)


def _expert_mlps(tile_expert, tile_valid, xs, layer, w_gate, w_up, w_down, tm):
    n_rows, d = xs.shape
    f = w_gate.shape[3]
    tf = min(512, f)
    td = min(2048, d)
    nt = n_rows // tm
    hidden = pl.pallas_call(
        _expert_up_kernel,
        grid_spec=pltpu.PrefetchScalarGridSpec(
            num_scalar_prefetch=2,
            grid=(f // tf, nt),
            in_specs=[pl.BlockSpec((tm, d), lambda j, t, te, tv: (t, 0)),
                      pl.BlockSpec((None, None, d, tf), lambda j, t, te, tv: (layer, te[t], 0, j)),
                      pl.BlockSpec((None, None, d, tf), lambda j, t, te, tv: (layer, te[t], 0, j))],
            out_specs=pl.BlockSpec((tm, tf), lambda j, t, te, tv: (t, j)),
            scratch_shapes=[pltpu.VMEM((d, tf), BF16), pltpu.VMEM((d, tf), BF16)]),
        out_shape=jax.ShapeDtypeStruct((n_rows, f), BF16),
        compiler_params=_cparams(("arbitrary", "arbitrary"),
                                 2 * (tm * d * 2 + 2 * d * tf * 4 + tm * tf * 2) + 2 * d * tf * 2 + 4 * tm * tf * 4),
        name="moe_gate_up",
    )(tile_expert, tile_valid, xs, w_gate, w_up)
    return pl.pallas_call(
        _expert_down_kernel,
        grid_spec=pltpu.PrefetchScalarGridSpec(
            num_scalar_prefetch=2,
            grid=(d // td, nt),
            in_specs=[pl.BlockSpec((tm, f), lambda j, t, te, tv: (t, 0)),
                      pl.BlockSpec((None, None, f, td), lambda j, t, te, tv: (layer, te[t], 0, j))],
            out_specs=pl.BlockSpec((tm, td), lambda j, t, te, tv: (t, j)),
            scratch_shapes=[pltpu.VMEM((f, td), BF16)]),
        out_shape=jax.ShapeDtypeStruct((n_rows, d), F32),
        compiler_params=_cparams(("arbitrary", "arbitrary"),
                                 2 * (tm * f * 2 + f * td * 4 + tm * td * 4) + f * td * 2 + tm * td * 4),
        name="moe_down",
    )(tile_expert, tile_valid, hidden, w_down)


def _combine_kernel(pos_ref, y_ref, x_ref, g_ref, route_ref, o_ref, buf0, buf1, sem,
                    *, tc, tiles_per_batch):
    i = pl.program_id(0)
    base = i * tc

    def issue(r, _):
        _row_copy(y_ref, pos_ref[2 * (base + r)], buf0, r, sem).start(priority=0)
        _row_copy(y_ref, pos_ref[2 * (base + r) + 1], buf1, r, sem).start(priority=1)
        return 0

    lax.fori_loop(0, tc, issue, 0)

    def drain(r, _):
        _row_copy(y_ref, 0, buf0, r, sem).wait()
        _row_copy(y_ref, 0, buf1, r, sem).wait()
        return 0

    lax.fori_loop(0, tc, drain, 0)
    route = route_ref[...]
    moe = route[:, 2:3] * buf0[...] + route[:, 3:4] * buf1[...]
    o_ref[...] = x_ref[...] + g_ref[pl.ds(i // tiles_per_batch, 1), :] * moe


def _combine(pos, y, x2, mod_l, g_chunk, route, seq):
    t, d = x2.shape
    tc = 128
    return pl.pallas_call(
        functools.partial(_combine_kernel, tc=tc, tiles_per_batch=seq // tc),
        grid_spec=pltpu.PrefetchScalarGridSpec(
            num_scalar_prefetch=1,
            grid=(t // tc,),
            in_specs=[pl.BlockSpec(memory_space=pl.ANY),
                      pl.BlockSpec((tc, d), lambda i, p: (i, 0)),
                      pl.BlockSpec((SUBLANES, d), lambda i, p: (0, g_chunk)),
                      pl.BlockSpec((tc, LANES), lambda i, p: (i, 0))],
            out_specs=pl.BlockSpec((tc, d), lambda i, p: (i, 0)),
            scratch_shapes=[pltpu.VMEM((tc, d), F32), pltpu.VMEM((tc, d), F32),
                            pltpu.SemaphoreType.DMA(())]),
        out_shape=jax.ShapeDtypeStruct((t, d), F32),
        compiler_params=_cparams(("arbitrary",), 2 * tc * d * 4 + 4 * tc * d * 4 + 4 * tc * d * 4),
        name="moe_combine",
    )(pos, y, x2, mod_l, route)


def _routing_tables(route, tm):
    t = route.shape[0]
    experts = route[:, :2].astype(jnp.int32).reshape(-1)
    onehot = (experts[:, None] == jnp.arange(N_EXPERTS)[None, :]).astype(jnp.int32)
    csum = jnp.cumsum(onehot, axis=0)
    rank = jnp.sum((csum - onehot) * onehot, axis=1)
    counts = csum[-1]
    padded = ((counts + tm - 1) // tm) * tm
    ends = jnp.cumsum(padded)
    starts = ends - padded
    dest = starts[experts] + rank
    n_rows = 2 * t + N_EXPERTS * tm
    row_token = jnp.zeros((n_rows,), jnp.int32).at[dest].set(jnp.arange(2 * t, dtype=jnp.int32) // 2)
    tile_start = jnp.arange(n_rows // tm, dtype=jnp.int32) * tm
    tile_expert = jnp.minimum(jnp.sum((tile_start[:, None] >= ends[None, :]).astype(jnp.int32), axis=1),
                              N_EXPERTS - 1)
    tile_valid = (tile_start < ends[-1]).astype(jnp.int32)
    real_end = (starts + counts)[tile_expert]
    tile_rows = jnp.clip(real_end - tile_start, 0, tm).astype(jnp.int32) * tile_valid
    return row_token, dest.astype(jnp.int32), tile_expert, tile_valid, tile_rows, n_rows


def _tiled(vec, reps):
    return jnp.tile(vec.astype(F32), reps)


def _mixer(h, seq, batch, layer, w_in, w_branch, w_out, rel_bias, cmp_pos, cmp_w1, cmp_w2,
           nsa_q_g, nsa_k_g, diff_q_g, diff_k_g, diff_lam, diff_out_g, lam_init, x2, mod_l, d):
    scale = HEAD_DIM ** -0.5
    ones = lambda n: jnp.ones((n,), F32)
    zeros = lambda n: jnp.zeros((n,), F32)
    proj = functools.partial(_matmul_w32, h, w_in, (layer,), tm=512, tn=512)

    cs = jnp.concatenate([jnp.full((BRANCH_W,), scale, F32), ones(2 * BRANCH_W)]).reshape(1, -1)
    p_sb = proj(_C_SB, 3 * BRANCH_W, [(cs, "col")], _ep_scale, BF16, name="proj_sb")
    nf = ones(BRANCH_W).reshape(1, -1)
    cs = (_tiled(nsa_q_g, N_HEADS) * scale).reshape(1, -1)
    p_nsq = proj(_C_NSQ, BRANCH_W, [(nf, "col"), (cs, "col")],
                 functools.partial(_ep_norm, width=HEAD_DIM), BF16, name="proj_nsq")
    nf = jnp.concatenate([zeros(2 * KV_W), ones(KV_W), zeros(KV_W), ones(KV_W), zeros(KV_W)]).reshape(1, -1)
    cs = jnp.concatenate([ones(2 * KV_W), _tiled(nsa_k_g[1], KV_HEADS), ones(KV_W),
                          _tiled(nsa_k_g[2], KV_HEADS), ones(KV_W)]).reshape(1, -1)
    p_nskv = proj(_C_NSKV, 6 * KV_W, [(nf, "col"), (cs, "col")],
                  functools.partial(_ep_norm, width=HEAD_DIM), BF16, name="proj_nskv")
    cols = np.array([_C_GNS + c * N_HEADS + Q_PER_KV * g + r
                     for g in range(KV_HEADS) for c in range(3) for r in range(Q_PER_KV)])
    wg = w_in[layer][:, cols].reshape(d, KV_HEADS, 3 * Q_PER_KV)
    wg = jnp.pad(wg, ((0, 0), (0, 0), (0, LANES - 3 * Q_PER_KV))).reshape(d, KV_HEADS * LANES).astype(BF16)
    ns_gates = _matmul(h, wg, [], _ep_sigmoid, F32, tm=1024, tn=256, name="proj_nsgate")
    nf = jnp.concatenate([ones(2 * BRANCH_W), zeros(BRANCH_W)]).reshape(1, -1)
    cs = jnp.concatenate([_tiled(diff_q_g, 2 * N_HEADS) * DIFF_QK ** -0.5, _tiled(diff_k_g, 2 * N_HEADS),
                          ones(BRANCH_W)]).reshape(1, -1)
    p_df = proj(_C_DF, 3 * BRANCH_W, [(nf, "col"), (cs, "col")],
                functools.partial(_ep_norm, width=DIFF_QK), BF16, name="proj_diff")
    m_gates = proj(_C_MERGE, 3 * d, [], _ep_sigmoid, BF16, name="proj_merge_gate")

    o_sb = _sb_attention(p_sb, batch, seq)
    kvcm = _compress(p_nskv, cmp_pos, cmp_w1, cmp_w2, nsa_k_g[0], batch, seq)
    o_ns = _nsa_attention(p_nsq, kvcm, p_nskv, ns_gates, rel_bias[:, :N_HEADS], nsa_q_g, nsa_k_g, batch, seq)
    o_df = _diff_attention(p_df, rel_bias[:, N_HEADS:], diff_lam, diff_out_g, diff_q_g, diff_k_g, lam_init,
                           batch, seq)
    merged = _branch_merge(o_sb, o_ns, o_df, w_branch, layer, m_gates)
    g1 = lax.slice_in_dim(mod_l, 2 * d, 3 * d, axis=1)
    return _matmul_w32(merged, w_out, (layer,), 0, d, [(x2, "tile"), (g1, "batch")], _ep_residual, F32,
                       tm=1024, tn=512, name="proj_out", seq=seq)


def _moe(x2, seq, mod_l, norm_gain, w_router, router_bias, layer, w_gate, w_up, w_down):
    tm = 256
    h2, route = _norm_router(x2, norm_gain, mod_l, 3, 4, seq, w_router, router_bias)
    row_token, pos, tile_expert, tile_valid, tile_rows, n_rows = _routing_tables(route, tm)
    xs = _gather_rows(row_token, tile_rows, h2, n_rows, tm)
    y = _expert_mlps(tile_expert, tile_valid, xs, layer, w_gate, w_up, w_down, tm)
    return _combine(pos, y, x2, mod_l, 5, route, seq)


def kernel(x, c, rel_bias, w_router, router_bias, w_ada, b_ada, norm_mix, norm_ffn, w_in, nsa_cmp_pos,
           nsa_cmp_w1, nsa_cmp_w2, nsa_q_norm, nsa_k_norm, diff_q_norm, diff_k_norm, diff_lambda,
           diff_out_norm, w_branch, w_out, w_exp_gate, w_exp_up, w_exp_down):
    batch, seq, d = x.shape
    depth = w_ada.shape[0]
    mod = _adaln(c, w_ada, b_ada)
    x2 = x.reshape(batch * seq, d)
    for layer in range(depth):
        lam_init = 0.8 - 0.6 * math.exp(-0.3 * layer)
        mod_l = mod[layer]
        h = _norm_mod(x2, norm_mix[layer], mod_l, 0, 1, seq)
        x2 = _mixer(h, seq, batch, layer, w_in, w_branch, w_out, rel_bias,
                    nsa_cmp_pos[layer], nsa_cmp_w1[layer], nsa_cmp_w2[layer], nsa_q_norm[layer],
                    nsa_k_norm[layer], diff_q_norm[layer], diff_k_norm[layer], diff_lambda[layer],
                    diff_out_norm[layer], lam_init, x2, mod_l, d)
        x2 = _moe(x2, seq, mod_l, norm_ffn[layer], w_router, router_bias, layer, w_exp_gate,
                  w_exp_up, w_exp_down)
    return x2.reshape(batch, seq, d)
```

```python
import functools
import math

import numpy as np
import jax
import jax.numpy as jnp
from jax import lax
from jax.experimental import pallas as pl
from jax.experimental.pallas import tpu as pltpu

F32 = jnp.float32
BF16 = jnp.bfloat16

HEAD_DIM = 128
N_HEADS = 8
KV_HEADS = 2
Q_PER_KV = N_HEADS // KV_HEADS
DIFF_QK = HEAD_DIM // 2
BRANCH_W = N_HEADS * HEAD_DIM
KV_W = KV_HEADS * HEAD_DIM
CMP_LEN = 32
CMP_STRIDE = 16
SLC_LEN = 64
N_SLC = 16
N_LOCAL_SLC = 2
WINDOW = 512
REL_BUCKETS = 32
REL_MAX_DIST = 128
N_EXPERTS = 16
N_GROUPS = 4
EXP_PER_GROUP = N_EXPERTS // N_GROUPS
NORM_EPS = 1e-6
NEG_INF = -1e30
MASKED_BELOW = -1e29
FORCE_SCORE = 1e9
LANES = 128
SUBLANES = 8
VMEM_CAP = 56 * 1024 * 1024
SB_EXIT = 100.0

_C_SB = 0
_C_NSQ = 3 * BRANCH_W
_C_NSKV = _C_NSQ + BRANCH_W
_C_GNS = _C_NSKV + 6 * KV_W
_C_DF = _C_GNS + 3 * N_HEADS
_C_MERGE = _C_DF + 3 * BRANCH_W


def _cparams(sem, vmem_bytes):
    lim = int(min(VMEM_CAP, max(vmem_bytes * 5 // 4 + (2 << 20), 16 << 20)))
    return pltpu.CompilerParams(dimension_semantics=sem, vmem_limit_bytes=lim)


def _dot(a, b):
    return jnp.dot(a, b, preferred_element_type=F32)


def _dot_nt(a, b):
    return lax.dot_general(a, b, (((1,), (1,)), ((), ())), preferred_element_type=F32)


def _split_bf16(x):
    hi = x.astype(BF16)
    lo = (x - hi.astype(F32)).astype(BF16)
    return hi, lo


def _ada_kernel(cb_ref, w_ref, b_ref, o_ref, ca_sc, *, nb, rows_per_iter):
    k_dim, tn = w_ref.shape

    @pl.when((pl.program_id(0) == 0) & (pl.program_id(1) == 0))
    def _():
        cb = cb_ref[...]
        ca_sc[...] = cb * jax.nn.sigmoid(cb)

    reps = tn // LANES
    groups = rows_per_iter // SUBLANES

    def body(i, accs):
        r0 = pl.multiple_of(i * rows_per_iter, rows_per_iter)
        w3 = w_ref[pl.ds(r0, rows_per_iter), :].reshape(groups, SUBLANES, tn)
        new = []
        for b in range(nb):
            ca = ca_sc[b, pl.ds(r0, rows_per_iter), :].reshape(groups, SUBLANES, LANES)
            ca = jnp.concatenate([ca] * reps, axis=2)
            new.append(accs[b] + jnp.sum(w3 * ca, axis=0))
        return tuple(new)

    accs = lax.fori_loop(0, k_dim // rows_per_iter, body,
                         tuple(jnp.zeros((SUBLANES, tn), F32) for _ in range(nb)))
    rows = [jnp.sum(a, axis=0, keepdims=True) + b_ref[...] for a in accs]
    rows.append(jnp.zeros((SUBLANES - nb, tn), F32))
    o_ref[...] = jnp.concatenate(rows, axis=0)


def _adaln(c, w_ada, b_ada):
    depth, d, n = w_ada.shape
    nb = c.shape[0]
    tn = 512
    cb = jnp.broadcast_to(c[:, :, None], (nb, d, LANES))
    vmem = 2 * d * tn * 4 + 3 * nb * d * LANES * 4
    return pl.pallas_call(
        functools.partial(_ada_kernel, nb=nb, rows_per_iter=64),
        grid=(depth, n // tn),
        in_specs=[pl.BlockSpec((nb, d, LANES), lambda l, j: (0, 0, 0)),
                  pl.BlockSpec((None, d, tn), lambda l, j: (l, 0, j)),
                  pl.BlockSpec((None, 1, tn), lambda l, j: (l, 0, j))],
        out_specs=pl.BlockSpec((None, SUBLANES, tn), lambda l, j: (l, 0, j)),
        out_shape=jax.ShapeDtypeStruct((depth, SUBLANES, n), F32),
        scratch_shapes=[pltpu.VMEM((nb, d, LANES), F32)],
        compiler_params=_cparams(("arbitrary", "arbitrary"), vmem),
        name="adaln_mod",
    )(cb, w_ada, b_ada.reshape(depth, 1, n))


def _modulated_norm(x_ref, gain_ref, sc_ref, sh_ref, tiles_per_batch):
    b = pl.program_id(0) // tiles_per_batch
    x = x_ref[...]
    ms = jnp.mean(x * x, axis=-1, keepdims=True)
    y = x * lax.rsqrt(ms + NORM_EPS) * gain_ref[...]
    return y * (1.0 + sc_ref[pl.ds(b, 1), :]) + sh_ref[pl.ds(b, 1), :]


def _norm_kernel(x_ref, gain_ref, sc_ref, sh_ref, o_ref, *, tiles_per_batch):
    o_ref[...] = _modulated_norm(x_ref, gain_ref, sc_ref, sh_ref, tiles_per_batch).astype(o_ref.dtype)


def _norm_mod(x2, gain, mod_l, sh_chunk, sc_chunk, seq):
    t, d = x2.shape
    tm = 256
    return pl.pallas_call(
        functools.partial(_norm_kernel, tiles_per_batch=seq // tm),
        grid=(t // tm,),
        in_specs=[pl.BlockSpec((tm, d), lambda i: (i, 0)),
                  pl.BlockSpec((1, d), lambda i: (0, 0)),
                  pl.BlockSpec((SUBLANES, d), lambda i: (0, sc_chunk)),
                  pl.BlockSpec((SUBLANES, d), lambda i: (0, sh_chunk))],
        out_specs=pl.BlockSpec((tm, d), lambda i: (i, 0)),
        out_shape=jax.ShapeDtypeStruct((t, d), BF16),
        compiler_params=_cparams(("arbitrary",), 2 * tm * d * 6 + 4 * SUBLANES * d * 4),
        name="norm_mix",
    )(x2, gain.reshape(1, d), mod_l, mod_l)


def _norm_router_kernel(x_ref, gain_ref, sc_ref, sh_ref, wr_ref, rb_ref, h_ref, route_ref,
                        *, tiles_per_batch):
    h = _modulated_norm(x_ref, gain_ref, sc_ref, sh_ref, tiles_per_batch)
    h_ref[...] = h
    h_hi, h_lo = _split_bf16(h)
    w_hi, w_lo = _split_bf16(wr_ref[...])
    logits = _dot(h_hi, w_hi) + _dot(h_hi, w_lo) + _dot(h_lo, w_hi)
    aff = jax.nn.sigmoid(logits)
    biased = aff + rb_ref[...]
    lane_i = lax.broadcasted_iota(jnp.int32, aff.shape, 1)
    lane = lane_i.astype(F32)
    low = jnp.float32(-3e38)
    big = jnp.float32(1 << 20)
    best = None
    for gi in range(N_GROUPS):
        in_group = (lane_i // EXP_PER_GROUP) == gi
        v = jnp.where(in_group, biased, low)
        m1 = jnp.max(v, axis=-1, keepdims=True)
        i1 = jnp.min(jnp.where(v == m1, lane, big), axis=-1, keepdims=True)
        v2 = jnp.where(lane == i1, low, v)
        m2 = jnp.max(v2, axis=-1, keepdims=True)
        i2 = jnp.min(jnp.where(v2 == m2, lane, big), axis=-1, keepdims=True)
        score = m1 + m2
        if best is None:
            best = (score, i1, i2)
        else:
            better = score > best[0]
            best = (jnp.where(better, score, best[0]), jnp.where(better, i1, best[1]),
                    jnp.where(better, i2, best[2]))
    _, e1, e2 = best
    a1 = jnp.sum(jnp.where(lane == e1, aff, 0.0), axis=-1, keepdims=True)
    a2 = jnp.sum(jnp.where(lane == e2, aff, 0.0), axis=-1, keepdims=True)
    den = a1 + a2
    out = jnp.where(lane_i == 0, e1,
                    jnp.where(lane_i == 1, e2,
                              jnp.where(lane_i == 2, a1 / den, jnp.where(lane_i == 3, a2 / den, 0.0))))
    route_ref[...] = out


def _norm_router(x2, gain, mod_l, sh_chunk, sc_chunk, seq, w_router, router_bias):
    t, d = x2.shape
    tm = 256
    wr = jnp.pad(w_router, ((0, 0), (0, LANES - N_EXPERTS)))
    rb = jnp.pad(router_bias, (0, LANES - N_EXPERTS)).reshape(1, LANES)
    return pl.pallas_call(
        functools.partial(_norm_router_kernel, tiles_per_batch=seq // tm),
        grid=(t // tm,),
        in_specs=[pl.BlockSpec((tm, d), lambda i: (i, 0)),
                  pl.BlockSpec((1, d), lambda i: (0, 0)),
                  pl.BlockSpec((SUBLANES, d), lambda i: (0, sc_chunk)),
                  pl.BlockSpec((SUBLANES, d), lambda i: (0, sh_chunk)),
                  pl.BlockSpec((d, LANES), lambda i: (0, 0)),
                  pl.BlockSpec((1, LANES), lambda i: (0, 0))],
        out_specs=[pl.BlockSpec((tm, d), lambda i: (i, 0)),
                   pl.BlockSpec((tm, LANES), lambda i: (i, 0))],
        out_shape=[jax.ShapeDtypeStruct((t, d), F32),
                   jax.ShapeDtypeStruct((t, LANES), F32)],
        compiler_params=_cparams(("arbitrary",), 2 * tm * d * 8 + 4 * d * LANES * 4 + 8 * tm * d),
        name="norm_ffn_router",
    )(x2, gain.reshape(1, d), mod_l, mod_l, wr, rb)


def _mm_w32_kernel(a_ref, *rest, epilogue, n_extra, tiles_per_batch, shift):
    n_w = 2 if shift else 1
    extras = rest[n_w:n_w + n_extra]
    o_ref, w_sc = rest[n_w + n_extra], rest[n_w + n_extra + 1]

    @pl.when(pl.program_id(1) == 0)
    def _():
        w_ref = rest[0]
        if not shift:
            w_sc[...] = w_ref[...].astype(BF16)
            return
        pieces = w_sc.shape[1] // LANES
        chunk = min(512, w_sc.shape[0])

        def move(r, _):
            rs = pl.ds(pl.multiple_of(r * chunk, chunk), chunk)
            for c in range(pieces):
                left = w_ref[rs, c * LANES:(c + 1) * LANES]
                right = w_ref[rs, (c + 1) * LANES:(c + 2) * LANES] if c + 1 < pieces else rest[1][rs, :]
                both = jnp.concatenate([left, right], axis=1)
                w_sc[rs, c * LANES:(c + 1) * LANES] = both[:, shift:shift + LANES].astype(BF16)
            return 0

        lax.fori_loop(0, w_sc.shape[0] // chunk, move, 0)

    acc = _dot(a_ref[...], w_sc[...])
    o_ref[...] = epilogue(acc, extras, pl.program_id(1) // tiles_per_batch).astype(o_ref.dtype)


def _matmul_w32(a, w, lead, col_lo, n, extras, epilogue, out_dtype, *, tm, tn, name, seq=None):
    m, k = a.shape
    tm = min(tm, m)
    tn = min(tn, n)
    shift = col_lo % LANES
    base = col_lo - shift
    assert m % tm == 0 and n % tn == 0 and base % tn == 0
    none = (None,) * len(lead)
    in_specs = [pl.BlockSpec((tm, k), lambda j, i: (i, 0)),
                pl.BlockSpec(none + (k, tn), lambda j, i: lead + (0, base // tn + j))]
    args = [a, w]
    vmem = 2 * tm * k * 2 + 2 * k * tn * 4 + k * tn * 2 + 2 * tm * tn * jnp.dtype(out_dtype).itemsize + 2 * tm * tn * 4
    if shift:
        per = tn // LANES
        in_specs.append(pl.BlockSpec(none + (k, LANES), lambda j, i: lead + (0, (base + tn) // LANES + j * per)))
        args.append(w)
        vmem += 2 * k * LANES * 4
    for arr, kind in extras:
        if kind == "tile":
            in_specs.append(pl.BlockSpec((tm, tn), lambda j, i: (i, j)))
            vmem += 2 * tm * tn * arr.dtype.itemsize
        else:
            rows = 1 if kind == "col" else SUBLANES
            in_specs.append(pl.BlockSpec((rows, tn), lambda j, i: (0, j)))
            vmem += 2 * SUBLANES * tn * 4
        args.append(arr)
    tiles_per_batch = (seq // tm) if seq else 1
    return pl.pallas_call(
        functools.partial(_mm_w32_kernel, epilogue=epilogue, n_extra=len(extras),
                          tiles_per_batch=tiles_per_batch, shift=shift),
        grid=(n // tn, m // tm),
        in_specs=in_specs,
        out_specs=pl.BlockSpec((tm, tn), lambda j, i: (i, j)),
        out_shape=jax.ShapeDtypeStruct((m, n), out_dtype),
        scratch_shapes=[pltpu.VMEM((k, tn), BF16)],
        compiler_params=_cparams(("arbitrary", "arbitrary"), vmem),
        name=name,
    )(*args)


def _ep_scale(acc, extras, b):
    return acc * extras[0][...]


def _ep_sigmoid(acc, extras, b):
    return jax.nn.sigmoid(acc)


def _ep_norm(acc, extras, b, *, width):
    nf_ref, cs_ref = extras
    outs = []
    for c in range(acc.shape[1] // LANES):
        sl = slice(c * LANES, (c + 1) * LANES)
        blk = acc[:, sl]
        sq = blk * blk
        if width == LANES:
            ms = jnp.mean(sq, axis=-1, keepdims=True)
        else:
            lane = lax.broadcasted_iota(jnp.int32, blk.shape, 1)
            first = lane < width
            lo = jnp.sum(jnp.where(first, sq, 0.0), axis=-1, keepdims=True) / width
            hi = jnp.sum(jnp.where(first, 0.0, sq), axis=-1, keepdims=True) / width
            ms = jnp.where(first, lo, hi)
        inv = lax.rsqrt(ms + NORM_EPS)
        f = jnp.where(nf_ref[:, sl] > 0.0, inv, 1.0)
        outs.append(blk * f * cs_ref[:, sl])
    return jnp.concatenate(outs, axis=1)


def _ep_residual(acc, extras, b):
    x_ref, g_ref = extras
    return x_ref[...] + g_ref[pl.ds(b, 1), :] * acc


def _bucket_np(dist):
    n = np.maximum(dist, 0)
    max_exact = REL_BUCKETS // 2
    nf = np.maximum(n, max_exact).astype(np.float32)
    large = max_exact + (np.log(nf / np.float32(max_exact))
                         / np.float32(math.log(REL_MAX_DIST / max_exact))
                         * np.float32(REL_BUCKETS - max_exact)).astype(np.int32)
    large = np.minimum(large, REL_BUCKETS - 1)
    return np.where(n < max_exact, n, large).astype(np.int32)


def _bucket_starts():
    b = _bucket_np(np.arange(4 * REL_MAX_DIST))
    return [int(np.argmax(b >= k)) for k in range(1, REL_BUCKETS)]


def _delta_bias(table, dist, valid):
    tb = table.T.astype(F32)
    shape = (tb.shape[0],) + (1,) * dist.ndim
    val = jnp.broadcast_to(tb[:, 0].reshape(shape), (tb.shape[0],) + dist.shape)
    for k, start in enumerate(_bucket_starts(), start=1):
        val = jnp.where(dist[None] >= start, tb[:, k].reshape(shape), val)
    val = val - tb[:, REL_BUCKETS - 1].reshape(shape)
    return jnp.where(valid[None], val, NEG_INF)


def _toeplitz_dist(n_off, step, nr, nc):
    shape = (n_off, nr, nc)
    return (lax.broadcasted_iota(jnp.int32, shape, 0) * step + lax.broadcasted_iota(jnp.int32, shape, 1)
            - lax.broadcasted_iota(jnp.int32, shape, 2))


def _stack_group_heads(tiles):
    nh, k, nr, nc = tiles.shape
    t = tiles.reshape(KV_HEADS, Q_PER_KV, k, nr, nc)
    return jnp.transpose(t, (0, 2, 1, 3, 4)).reshape(KV_HEADS, k, Q_PER_KV * nr, nc)


def _sb_kernel(q_ref, k_ref, v_ref, o_ref, *, tq):
    qi = pl.program_id(2)
    q = q_ref[...]
    row = lax.broadcasted_iota(jnp.int32, (tq, tq), 0)
    col = lax.broadcasted_iota(jnp.int32, (tq, tq), 1)
    later = jnp.where(row > col, 1.0, 0.0).astype(BF16)
    causal = col < row

    def step(kj, carry, acc, diag):
        k0 = pl.multiple_of(kj * tq, tq)
        k = k_ref[pl.ds(k0, tq), :]
        v = v_ref[pl.ds(k0, tq), :]
        z = _dot_nt(q, k)
        log_succ = jnp.minimum(z, 0.0) - jnp.log1p(jnp.exp(-jnp.abs(z)))
        log_fail = log_succ - z
        if diag:
            log_fail = jnp.where(causal, log_fail, 0.0)
        hi, lo = _split_bf16(log_fail)
        suffix = _dot(hi, later) + _dot(lo, later)
        a = jnp.exp(log_succ + suffix + carry)
        if diag:
            a = jnp.where(causal, a, 0.0)
        acc = acc + _dot(a.astype(BF16), v)
        carry = carry + jnp.sum(log_fail, axis=-1, keepdims=True)
        return carry, acc

    carry, acc = step(qi, jnp.zeros((tq, 1), F32), jnp.zeros((tq, HEAD_DIM), F32), True)

    def cond(state):
        kj, carry, _ = state
        return (kj >= 0) & (jnp.max(carry) > -SB_EXIT)

    def body(state):
        kj, carry, acc = state
        carry, acc = step(kj, carry, acc, False)
        return kj - 1, carry, acc

    _, _, acc = lax.while_loop(cond, body, (qi - 1, carry, acc))
    o_ref[...] = acc.astype(o_ref.dtype)


def _sb_attention(p_sb, batch, seq):
    tq = min(256, seq)
    nq = seq // tq
    t = batch * seq
    vmem = 4 * seq * HEAD_DIM * 2 + 4 * tq * HEAD_DIM * 2 + 16 * tq * tq * 4
    return pl.pallas_call(
        functools.partial(_sb_kernel, tq=tq),
        grid=(batch, N_HEADS, nq),
        in_specs=[pl.BlockSpec((tq, HEAD_DIM), lambda b, h, i: (b * nq + i, h)),
                  pl.BlockSpec((seq, HEAD_DIM), lambda b, h, i: (b, N_HEADS + h)),
                  pl.BlockSpec((seq, HEAD_DIM), lambda b, h, i: (b, 2 * N_HEADS + h))],
        out_specs=pl.BlockSpec((tq, HEAD_DIM), lambda b, h, i: (b * nq + i, h)),
        out_shape=jax.ShapeDtypeStruct((t, BRANCH_W), BF16),
        compiler_params=_cparams(("arbitrary",) * 3, vmem),
        name="sb_attention",
    )(p_sb, p_sb, p_sb)


LOGIT_CAP = 40.0


def _with_ones(v):
    return jnp.concatenate([v, jnp.ones_like(v)], axis=1)


def _softmax_first(s, v, m_sc, acc_sc, bounded):
    if bounded:
        acc_sc[...] = _dot(jnp.exp(s).astype(BF16), _with_ones(v))
        return
    m = jnp.max(s, axis=-1, keepdims=True)
    m_sc[...] = m
    acc_sc[...] = _dot(jnp.exp(s - m).astype(BF16), _with_ones(v))


def _softmax_next(s, v, m_sc, acc_sc, bounded):
    if bounded:
        acc_sc[...] = acc_sc[...] + _dot(jnp.exp(s).astype(BF16), _with_ones(v))
        return
    m_old = m_sc[...]
    m_new = jnp.maximum(m_old, jnp.max(s, axis=-1, keepdims=True))
    p = jnp.exp(s - m_new)
    acc_sc[...] = jnp.exp(m_old - m_new) * acc_sc[...] + _dot(p.astype(BF16), _with_ones(v))
    m_sc[...] = m_new


def _softmax_result(acc_sc, bounded):
    acc = acc_sc[...]
    total = acc[:, HEAD_DIM:]
    den = jnp.where(total > 0.0, total, 1.0) if bounded else jnp.maximum(total, 1.0)
    return acc[:, :HEAD_DIM] / den


def _logit_bound_ok(width, scale, q_gain, k_gains, table):
    qk = width * scale * jnp.max(jnp.abs(q_gain)) * jnp.max(jnp.abs(k_gains)) * 1.02
    delta = jnp.max(jnp.abs(table - table[REL_BUCKETS - 1:]))
    return (qk + delta <= LOGIT_CAP).astype(jnp.int32).reshape(1)


def _compress_kernel(x_ref, p_ref, w1a_ref, w1b_ref, w2_ref, gain_ref, o_ref):
    nc = x_ref.shape[0]
    is_key = pl.program_id(1) < KV_HEADS
    x = x_ref[...].astype(F32)
    first = _dot((x + p_ref[0:1, :]).astype(BF16), w1a_ref[...])
    second = _dot((x + p_ref[1:2, :]).astype(BF16), w1b_ref[...])
    pre = first + pltpu.roll(second, nc - 1, 0)
    hid = pre * jax.nn.sigmoid(pre)
    out = _dot(hid.astype(BF16), w2_ref[...])
    ms = jnp.mean(out * out, axis=-1, keepdims=True)
    normed = out * lax.rsqrt(ms + NORM_EPS) * gain_ref[...]
    o_ref[...] = jnp.where(is_key, normed, out).astype(o_ref.dtype)


def _compress(p_nskv, cmp_pos, cmp_w1, cmp_w2, k_gain0, batch, seq):
    nc = seq // CMP_STRIDE
    half = CMP_STRIDE * HEAD_DIM
    x = p_nskv[:, :2 * KV_W].reshape(batch, nc, CMP_STRIDE, 2 * KV_HEADS, HEAD_DIM)
    x = jnp.transpose(x, (0, 3, 1, 2, 4)).reshape(batch, 2 * KV_HEADS, nc, half)
    pos = cmp_pos.reshape(2, 2, half)
    w1 = cmp_w1.astype(BF16).reshape(2, 2, half, HEAD_DIM)
    w2 = cmp_w2.astype(BF16)
    return pl.pallas_call(
        _compress_kernel,
        grid=(batch, 2 * KV_HEADS),
        in_specs=[pl.BlockSpec((None, None, nc, half), lambda b, j: (b, j, 0, 0)),
                  pl.BlockSpec((None, 2, half), lambda b, j: (j // KV_HEADS, 0, 0)),
                  pl.BlockSpec((None, None, half, HEAD_DIM), lambda b, j: (j // KV_HEADS, 0, 0, 0)),
                  pl.BlockSpec((None, None, half, HEAD_DIM), lambda b, j: (j // KV_HEADS, 1, 0, 0)),
                  pl.BlockSpec((None, HEAD_DIM, HEAD_DIM), lambda b, j: (j // KV_HEADS, 0, 0)),
                  pl.BlockSpec((1, HEAD_DIM), lambda b, j: (0, 0))],
        out_specs=pl.BlockSpec((None, None, nc, HEAD_DIM), lambda b, j: (b, j, 0, 0)),
        out_shape=jax.ShapeDtypeStruct((batch, 2 * KV_HEADS, nc, HEAD_DIM), BF16),
        compiler_params=_cparams(("arbitrary", "arbitrary"), 8 * nc * half * 2 + 8 * half * HEAD_DIM * 2),
        name="nsa_compress",
    )(x, pos, w1, w1, w2, k_gain0.reshape(1, HEAD_DIM))


def _nsa_kernel(ok_ref, q_ref, kcm_ref, vcm_ref, ks_ref, vs_ref, kw_ref, vw_ref, gate_ref,
                pt_ref, bt_ref, wt_ref, o_ref, m_sc, acc_sc, *, tq, tk, tw, n_blk, k_sel):
    qi = pl.program_id(2)
    q0 = qi * tq
    rows = Q_PER_KV * tq
    nc = kcm_ref.shape[0]
    q4 = jnp.concatenate([q_ref[:, r * HEAD_DIM:(r + 1) * HEAD_DIM] for r in range(Q_PER_KV)], axis=0)

    per_tile = tq // CMP_STRIDE
    lead = 2 * per_tile
    shift = lax.rem(qi * per_tile + (nc - lead), nc)
    cb = pltpu.roll(pt_ref[...], shift, 1)
    rel = lax.broadcasted_iota(jnp.int32, (rows, nc), 1) - qi * per_tile
    cb = jnp.where(rel < -lead, 0.0, jnp.where(rel >= per_tile, NEG_INF, cb))
    s = _dot_nt(q4, kcm_ref[...]) + cb
    valid = s > MASKED_BELOW
    s = jnp.where(valid, s, NEG_INF)
    p = jnp.where(valid, jnp.exp(s - jnp.max(s, axis=-1, keepdims=True)), 0.0)
    p = p / jnp.maximum(jnp.sum(p, axis=-1, keepdims=True), 1.0)
    o_cmp = _dot(p.astype(BF16), vcm_ref[...])

    psum = p[0:tq]
    for r in range(1, Q_PER_KV):
        psum = psum + p[r * tq:(r + 1) * tq]
    n_id = lax.broadcasted_iota(jnp.int32, (nc, LANES), 0)
    j_id = lax.broadcasted_iota(jnp.int32, (nc, LANES), 1)
    overlap = ((CMP_STRIDE * n_id < SLC_LEN * j_id + SLC_LEN)
               & (CMP_STRIDE * n_id + CMP_LEN > SLC_LEN * j_id) & (j_id < n_blk))
    overlap = jnp.where(overlap, 1.0, 0.0).astype(BF16)
    p_hi, p_lo = _split_bf16(psum)
    imp = _dot(p_hi, overlap) + _dot(p_lo, overlap)
    nr = -(-n_blk // SUBLANES) * SUBLANES
    imp_t = jnp.transpose(imp)[:nr]
    blk = lax.broadcasted_iota(jnp.int32, (nr, tq), 0)
    cur = (q0 + lax.broadcasted_iota(jnp.int32, (nr, tq), 1)) // SLC_LEN
    forced = (blk == 0) | ((blk <= cur) & (blk > cur - N_LOCAL_SLC))
    imp_t = jnp.where(forced, FORCE_SCORE, jnp.where(blk > cur, NEG_INF, imp_t))
    imp_t = jnp.where(blk < n_blk, imp_t, -3e38)
    rank = jnp.zeros((nr, tq), F32)
    for j in range(n_blk):
        cj = imp_t[j:j + 1, :]
        ahead = jnp.where(blk > j, jnp.where(cj >= imp_t, 1.0, 0.0), jnp.where(cj > imp_t, 1.0, 0.0))
        rank = rank + ahead
    sel_t = jnp.where((rank < k_sel) & (blk < n_blk), 1.0, 0.0)
    if nr < LANES:
        sel_t = jnp.concatenate([sel_t, jnp.zeros((LANES - nr, tq), F32)], axis=0)
    sel = jnp.transpose(sel_t).astype(BF16)

    per = tk // tq
    kd = qi // per
    par = qi - kd * per
    e_row = lax.broadcasted_iota(jnp.int32, (LANES, tk), 0)
    e_col = lax.broadcasted_iota(jnp.int32, (LANES, tk), 1)

    def slc_scores(kj, bias):
        k0 = pl.multiple_of(kj * tk, tk)
        expand = jnp.where((k0 + e_col) // SLC_LEN == e_row, 1.0, 0.0).astype(BF16)
        chosen = _dot(sel, expand)
        s = _dot_nt(q4, ks_ref[pl.ds(k0, tk), :])
        if bias is not None:
            s = s + bias
        s = jnp.where(chosen[None] > 0.5, s.reshape(Q_PER_KV, tq, tk), NEG_INF).reshape(rows, tk)
        return s, vs_ref[pl.ds(k0, tk), :]

    def attend(bounded):
        s, v = slc_scores(kd, bt_ref[par])
        _softmax_first(s, v, m_sc, acc_sc, bounded)

        @pl.when(kd >= 1)
        def _():
            s, v = slc_scores(kd - 1, bt_ref[per + par])
            _softmax_next(s, v, m_sc, acc_sc, bounded)

        def slc_far(i, _):
            s, v = slc_scores(kd - 2 - i, None)
            _softmax_next(s, v, m_sc, acc_sc, bounded)
            return 0

        lax.fori_loop(0, jnp.maximum(kd - 1, 0), slc_far, 0)
        o_slc = _softmax_result(acc_sc, bounded)

        start = pl.multiple_of(jnp.maximum(q0 - WINDOW, 0), tq)
        s = _dot_nt(q4, kw_ref[pl.ds(start, tw), :]) + wt_ref[jnp.minimum(qi, WINDOW // tq)]
        _softmax_first(s, vw_ref[pl.ds(start, tw), :], m_sc, acc_sc, bounded)
        o_win = _softmax_result(acc_sc, bounded)

        gate = gate_ref[...]
        first_group = pl.program_id(1) == 0

        def gate_col(branch, r):
            lo = branch * N_HEADS + r
            hi = lo + Q_PER_KV
            return jnp.where(first_group, gate[:, lo:lo + 1], gate[:, hi:hi + 1])

        for r in range(Q_PER_KV):
            sl = slice(r * tq, (r + 1) * tq)
            o = gate_col(0, r) * o_cmp[sl] + gate_col(1, r) * o_slc[sl] + gate_col(2, r) * o_win[sl]
            o_ref[:, r * HEAD_DIM:(r + 1) * HEAD_DIM] = o.astype(o_ref.dtype)

    @pl.when(ok_ref[0] > 0)
    def _():
        attend(True)

    @pl.when(ok_ref[0] == 0)
    def _():
        attend(False)


def _nsa_attention(p_nsq, kvcm, p_nskv, gates, rel_nsa, q_gain, k_gains, batch, seq):
    tq = 128
    tk = min(512, seq)
    tw = WINDOW + tq
    nq = seq // tq
    nc = seq // CMP_STRIDE
    n_blk = seq // SLC_LEN
    k_sel = min(N_SLC, n_blk)
    t = batch * seq
    rows = Q_PER_KV * tq
    per = tk // tq
    assert n_blk <= LANES and nc % LANES == 0 and WINDOW % tq == 0 and tw <= seq and tq == LANES

    d = _toeplitz_dist(2 * per, tq, tq, tk)
    bt = _stack_group_heads(_delta_bias(rel_nsa, d, d >= 0))
    d = _toeplitz_dist(WINDOW // tq + 1, tq, tq, tw)
    wt = _stack_group_heads(_delta_bias(rel_nsa, d, (d >= 0) & (d < WINDOW)))
    per_tile = tq // CMP_STRIDE
    d = (lax.broadcasted_iota(jnp.int32, (1, tq, nc), 1)
         - CMP_STRIDE * (lax.broadcasted_iota(jnp.int32, (1, tq, nc), 2) - 2 * per_tile) - (CMP_LEN - 1))
    pt = _stack_group_heads(_delta_bias(rel_nsa, d, d >= 0))[:, 0]

    ok = _logit_bound_ok(HEAD_DIM, HEAD_DIM ** -0.5, q_gain, k_gains[1:], rel_nsa)
    const = lambda shape: pl.BlockSpec((None,) + shape, lambda b, g, i, ok: (g,) + (0,) * len(shape),
                                       pipeline_mode=pl.Buffered(1))
    vmem = (8 * seq * HEAD_DIM * 2 + (bt.size + wt.size + pt.size) // KV_HEADS * 4
            + 10 * rows * tw * 4 + 8 * rows * nc * 4)
    return pl.pallas_call(
        functools.partial(_nsa_kernel, tq=tq, tk=tk, tw=tw, n_blk=n_blk, k_sel=k_sel),
        grid_spec=pltpu.PrefetchScalarGridSpec(
            num_scalar_prefetch=1,
            grid=(batch, KV_HEADS, nq),
            in_specs=[pl.BlockSpec((tq, Q_PER_KV * HEAD_DIM), lambda b, g, i, ok: (b * nq + i, g)),
                      pl.BlockSpec((None, None, nc, HEAD_DIM), lambda b, g, i, ok: (b, g, 0, 0)),
                      pl.BlockSpec((None, None, nc, HEAD_DIM), lambda b, g, i, ok: (b, KV_HEADS + g, 0, 0)),
                      pl.BlockSpec((seq, HEAD_DIM), lambda b, g, i, ok: (b, 2 * KV_HEADS + g)),
                      pl.BlockSpec((seq, HEAD_DIM), lambda b, g, i, ok: (b, 3 * KV_HEADS + g)),
                      pl.BlockSpec((seq, HEAD_DIM), lambda b, g, i, ok: (b, 4 * KV_HEADS + g)),
                      pl.BlockSpec((seq, HEAD_DIM), lambda b, g, i, ok: (b, 5 * KV_HEADS + g)),
                      pl.BlockSpec((tq, LANES), lambda b, g, i, ok: (b * nq + i, 0)),
                      const(pt.shape[1:]), const(bt.shape[1:]), const(wt.shape[1:])],
            out_specs=pl.BlockSpec((tq, Q_PER_KV * HEAD_DIM), lambda b, g, i, ok: (b * nq + i, g)),
            scratch_shapes=[pltpu.VMEM((rows, 1), F32), pltpu.VMEM((rows, 2 * HEAD_DIM), F32)]),
        out_shape=jax.ShapeDtypeStruct((t, BRANCH_W), BF16),
        compiler_params=_cparams(("arbitrary",) * 3, vmem),
        name="nsa_attention",
    )(ok, p_nsq, kvcm, kvcm, p_nskv, p_nskv, p_nskv, p_nskv, gates, pt, bt, wt)


def _diff_kernel(ok_ref, q_ref, k_ref, v_ref, dt_ref, lam_ref, og_ref, o_ref, m_sc, acc_sc,
                 *, tq, tk, lam_init):
    qi = pl.program_id(2)
    per = tk // tq
    kd = qi // per
    par = qi - kd * per
    q = q_ref[...]
    lane = lax.broadcasted_iota(jnp.int32, q.shape, 1)
    zero = jnp.zeros_like(q)
    q2 = jnp.concatenate([jnp.where(lane < DIFF_QK, q, zero), jnp.where(lane < DIFF_QK, zero, q)], axis=0)

    def scores(kj, bias):
        k0 = pl.multiple_of(kj * tk, tk)
        s = _dot_nt(q2, k_ref[pl.ds(k0, tk), :])
        if bias is not None:
            s = s + bias
        return s, v_ref[pl.ds(k0, tk), :]

    def attend(bounded):
        s, v = scores(kd, dt_ref[par])
        _softmax_first(s, v, m_sc, acc_sc, bounded)

        @pl.when(kd >= 1)
        def _():
            s, v = scores(kd - 1, dt_ref[per + par])
            _softmax_next(s, v, m_sc, acc_sc, bounded)

        def far(i, _):
            s, v = scores(kd - 2 - i, None)
            _softmax_next(s, v, m_sc, acc_sc, bounded)
            return 0

        lax.fori_loop(0, jnp.maximum(kd - 1, 0), far, 0)
        o2 = _softmax_result(acc_sc, bounded)
        lp = lam_ref[...]
        lam = (jnp.exp(jnp.sum(lp[0:1] * lp[1:2], axis=-1, keepdims=True))
               - jnp.exp(jnp.sum(lp[2:3] * lp[3:4], axis=-1, keepdims=True)) + lam_init)
        o = o2[:tq] - lam * o2[tq:]
        ms = jnp.mean(o * o, axis=-1, keepdims=True)
        o = o * lax.rsqrt(ms + NORM_EPS) * og_ref[...]
        o_ref[...] = (o * (1.0 - lam_init)).astype(o_ref.dtype)

    @pl.when(ok_ref[0] > 0)
    def _():
        attend(True)

    @pl.when(ok_ref[0] == 0)
    def _():
        attend(False)


def _diff_attention(p_df, rel_diff, lam_params, out_gain, q_gain, k_gain, lam_init, batch, seq):
    tq = min(256, seq)
    tk = min(512, seq)
    per = tk // tq
    nq = seq // tq
    t = batch * seq
    d = _toeplitz_dist(2 * per, tq, tq, tk)
    dt = _delta_bias(rel_diff, d, d >= 0)
    dt = jnp.concatenate([dt, dt], axis=2)
    ok = _logit_bound_ok(DIFF_QK, DIFF_QK ** -0.5, q_gain, k_gain, rel_diff)
    vmem = 4 * seq * HEAD_DIM * 2 + 2 * 2 * per * 2 * tq * tk * 4 + 12 * 2 * tq * tk * 4
    return pl.pallas_call(
        functools.partial(_diff_kernel, tq=tq, tk=tk, lam_init=lam_init),
        grid_spec=pltpu.PrefetchScalarGridSpec(
            num_scalar_prefetch=1,
            grid=(batch, N_HEADS, nq),
            in_specs=[pl.BlockSpec((tq, HEAD_DIM), lambda b, h, i, ok: (b * nq + i, h)),
                      pl.BlockSpec((seq, HEAD_DIM), lambda b, h, i, ok: (b, N_HEADS + h)),
                      pl.BlockSpec((seq, HEAD_DIM), lambda b, h, i, ok: (b, 2 * N_HEADS + h)),
                      pl.BlockSpec((None, 2 * per, 2 * tq, tk), lambda b, h, i, ok: (h, 0, 0, 0)),
                      pl.BlockSpec((4, DIFF_QK), lambda b, h, i, ok: (0, 0)),
                      pl.BlockSpec((1, HEAD_DIM), lambda b, h, i, ok: (0, 0))],
            out_specs=pl.BlockSpec((tq, HEAD_DIM), lambda b, h, i, ok: (b * nq + i, h)),
            scratch_shapes=[pltpu.VMEM((2 * tq, 1), F32), pltpu.VMEM((2 * tq, 2 * HEAD_DIM), F32)]),
        out_shape=jax.ShapeDtypeStruct((t, BRANCH_W), BF16),
        compiler_params=_cparams(("arbitrary",) * 3, vmem),
        name="diff_attention",
    )(ok, p_df, p_df, p_df, dt, lam_params, out_gain.reshape(1, HEAD_DIM))


def _merge_kernel(o0_ref, o1_ref, o2_ref, w_ref, g0_ref, g1_ref, g2_ref, out_ref, w_sc):
    @pl.when(pl.program_id(1) == 0)
    def _():
        w_sc[...] = w_ref[...].astype(BF16)

    acc = g0_ref[...].astype(F32) * _dot(o0_ref[...], w_sc[0])
    acc = acc + g1_ref[...].astype(F32) * _dot(o1_ref[...], w_sc[1])
    acc = acc + g2_ref[...].astype(F32) * _dot(o2_ref[...], w_sc[2])
    out_ref[...] = acc.astype(out_ref.dtype)


def _branch_merge(o_sb, o_ns, o_df, w_branch, layer, gates):
    t = o_sb.shape[0]
    d = w_branch.shape[3]
    tm = min(1024, t)
    tn = min(512, d)
    nj = d // tn
    o_spec = pl.BlockSpec((tm, BRANCH_W), lambda j, i: (i, 0))
    vmem = (2 * (3 * tm * BRANCH_W * 2 + 3 * BRANCH_W * tn * 4 + 3 * tm * tn * 2 + tm * tn * 2)
            + 3 * BRANCH_W * tn * 2 + 4 * tm * tn * 4)
    return pl.pallas_call(
        _merge_kernel,
        grid=(nj, t // tm),
        in_specs=[o_spec, o_spec, o_spec,
                  pl.BlockSpec((None, 3, BRANCH_W, tn), lambda j, i: (layer, 0, 0, j)),
                  pl.BlockSpec((tm, tn), lambda j, i: (i, j)),
                  pl.BlockSpec((tm, tn), lambda j, i: (i, nj + j)),
                  pl.BlockSpec((tm, tn), lambda j, i: (i, 2 * nj + j))],
        out_specs=pl.BlockSpec((tm, tn), lambda j, i: (i, j)),
        out_shape=jax.ShapeDtypeStruct((t, d), BF16),
        scratch_shapes=[pltpu.VMEM((3, BRANCH_W, tn), BF16)],
        compiler_params=_cparams(("arbitrary", "arbitrary"), vmem),
        name="branch_merge",
    )(o_sb, o_ns, o_df, w_branch, gates, gates, gates)


def _row_copy(src_ref, src_row, dst_ref, dst_row, sem):
    return pltpu.make_async_copy(src_ref.at[pl.ds(src_row, 1)], dst_ref.at[pl.ds(dst_row, 1)], sem)


def _gather_kernel(idx_ref, cnt_ref, src_ref, o_ref, buf, sem, *, tg):
    i = pl.program_id(0)
    base = i * tg
    n = cnt_ref[i]

    @pl.when(i == 0)
    def _():
        buf[...] = jnp.zeros(buf.shape, buf.dtype)

    for parity in range(2):
        def issue(r2, _, parity=parity):
            r = 2 * r2 + parity
            _row_copy(src_ref, idx_ref[base + r], buf, r, sem).start(priority=parity)
            return 0

        lax.fori_loop(0, (n + 1 - parity) // 2, issue, 0)

    def drain(r, _):
        _row_copy(src_ref, 0, buf, r, sem).wait()
        return 0

    lax.fori_loop(0, n, drain, 0)
    o_ref[...] = buf[...].astype(o_ref.dtype)


def _gather_rows(row_token, tile_rows, h2, n_rows, tg):
    d = h2.shape[1]
    return pl.pallas_call(
        functools.partial(_gather_kernel, tg=tg),
        grid_spec=pltpu.PrefetchScalarGridSpec(
            num_scalar_prefetch=2,
            grid=(n_rows // tg,),
            in_specs=[pl.BlockSpec(memory_space=pl.ANY)],
            out_specs=pl.BlockSpec((tg, d), lambda i, idx, cnt: (i, 0)),
            scratch_shapes=[pltpu.VMEM((tg, d), F32), pltpu.SemaphoreType.DMA(())]),
        out_shape=jax.ShapeDtypeStruct((n_rows, d), BF16),
        compiler_params=_cparams(("arbitrary",), tg * d * 4 + 2 * tg * d * 2),
        name="moe_gather",
    )(row_token, tile_rows, h2)


def _expert_changed(te_ref, t):
    return (t == 0) | (te_ref[t] != te_ref[jnp.maximum(t - 1, 0)])


def _expert_up_kernel(te_ref, tv_ref, x_ref, wg_ref, wu_ref, o_ref, wg_sc, wu_sc):
    t = pl.program_id(1)

    @pl.when(_expert_changed(te_ref, t))
    def _():
        wg_sc[...] = wg_ref[...].astype(BF16)
        wu_sc[...] = wu_ref[...].astype(BF16)

    @pl.when(tv_ref[t] > 0)
    def _():
        x = x_ref[...]
        g = _dot(x, wg_sc[...])
        u = _dot(x, wu_sc[...])
        o_ref[...] = (g * jax.nn.sigmoid(g) * u).astype(o_ref.dtype)

    @pl.when(tv_ref[t] == 0)
    def _():
        o_ref[...] = jnp.zeros(o_ref.shape, o_ref.dtype)


def _expert_down_kernel(te_ref, tv_ref, h_ref, wd_ref, o_ref, wd_sc):
    t = pl.program_id(1)

    @pl.when(_expert_changed(te_ref, t))
    def _():
        wd_sc[...] = wd_ref[...].astype(BF16)

    @pl.when(tv_ref[t] > 0)
    def _():
        o_ref[...] = _dot(h_ref[...], wd_sc[...])

    @pl.when(tv_ref[t] == 0)
    def _():
        o_ref[...] = jnp.zeros(o_ref.shape, o_ref.dtype)


def _expert_mlps(tile_expert, tile_valid, xs, layer, w_gate, w_up, w_down, tm):
    n_rows, d = xs.shape
    f = w_gate.shape[3]
    tf = min(512, f)
    td = min(2048, d)
    nt = n_rows // tm
    hidden = pl.pallas_call(
        _expert_up_kernel,
        grid_spec=pltpu.PrefetchScalarGridSpec(
            num_scalar_prefetch=2,
            grid=(f // tf, nt),
            in_specs=[pl.BlockSpec((tm, d), lambda j, t, te, tv: (t, 0)),
                      pl.BlockSpec((None, None, d, tf), lambda j, t, te, tv: (layer, te[t], 0, j)),
                      pl.BlockSpec((None, None, d, tf), lambda j, t, te, tv: (layer, te[t], 0, j))],
            out_specs=pl.BlockSpec((tm, tf), lambda j, t, te, tv: (t, j)),
            scratch_shapes=[pltpu.VMEM((d, tf), BF16), pltpu.VMEM((d, tf), BF16)]),
        out_shape=jax.ShapeDtypeStruct((n_rows, f), BF16),
        compiler_params=_cparams(("arbitrary", "arbitrary"),
                                 2 * (tm * d * 2 + 2 * d * tf * 4 + tm * tf * 2) + 2 * d * tf * 2 + 4 * tm * tf * 4),
        name="moe_gate_up",
    )(tile_expert, tile_valid, xs, w_gate, w_up)
    return pl.pallas_call(
        _expert_down_kernel,
        grid_spec=pltpu.PrefetchScalarGridSpec(
            num_scalar_prefetch=2,
            grid=(d // td, nt),
            in_specs=[pl.BlockSpec((tm, f), lambda j, t, te, tv: (t, 0)),
                      pl.BlockSpec((None, None, f, td), lambda j, t, te, tv: (layer, te[t], 0, j))],
            out_specs=pl.BlockSpec((tm, td), lambda j, t, te, tv: (t, j)),
            scratch_shapes=[pltpu.VMEM((f, td), BF16)]),
        out_shape=jax.ShapeDtypeStruct((n_rows, d), F32),
        compiler_params=_cparams(("arbitrary", "arbitrary"),
                                 2 * (tm * f * 2 + f * td * 4 + tm * td * 4) + f * td * 2 + tm * td * 4),
        name="moe_down",
    )(tile_expert, tile_valid, hidden, w_down)


def _combine_kernel(pos_ref, y_ref, x_ref, g_ref, route_ref, o_ref, buf0, buf1, sem,
                    *, tc, tiles_per_batch):
    i = pl.program_id(0)
    base = i * tc

    def issue(r, _):
        _row_copy(y_ref, pos_ref[2 * (base + r)], buf0, r, sem).start(priority=0)
        _row_copy(y_ref, pos_ref[2 * (base + r) + 1], buf1, r, sem).start(priority=1)
        return 0

    lax.fori_loop(0, tc, issue, 0)

    def drain(r, _):
        _row_copy(y_ref, 0, buf0, r, sem).wait()
        _row_copy(y_ref, 0, buf1, r, sem).wait()
        return 0

    lax.fori_loop(0, tc, drain, 0)
    route = route_ref[...]
    moe = route[:, 2:3] * buf0[...] + route[:, 3:4] * buf1[...]
    o_ref[...] = x_ref[...] + g_ref[pl.ds(i // tiles_per_batch, 1), :] * moe


def _combine(pos, y, x2, mod_l, g_chunk, route, seq):
    t, d = x2.shape
    tc = 128
    return pl.pallas_call(
        functools.partial(_combine_kernel, tc=tc, tiles_per_batch=seq // tc),
        grid_spec=pltpu.PrefetchScalarGridSpec(
            num_scalar_prefetch=1,
            grid=(t // tc,),
            in_specs=[pl.BlockSpec(memory_space=pl.ANY),
                      pl.BlockSpec((tc, d), lambda i, p: (i, 0)),
                      pl.BlockSpec((SUBLANES, d), lambda i, p: (0, g_chunk)),
                      pl.BlockSpec((tc, LANES), lambda i, p: (i, 0))],
            out_specs=pl.BlockSpec((tc, d), lambda i, p: (i, 0)),
            scratch_shapes=[pltpu.VMEM((tc, d), F32), pltpu.VMEM((tc, d), F32),
                            pltpu.SemaphoreType.DMA(())]),
        out_shape=jax.ShapeDtypeStruct((t, d), F32),
        compiler_params=_cparams(("arbitrary",), 2 * tc * d * 4 + 4 * tc * d * 4 + 4 * tc * d * 4),
        name="moe_combine",
    )(pos, y, x2, mod_l, route)


def _routing_tables(route, tm):
    t = route.shape[0]
    experts = route[:, :2].astype(jnp.int32).reshape(-1)
    onehot = (experts[:, None] == jnp.arange(N_EXPERTS)[None, :]).astype(jnp.int32)
    csum = jnp.cumsum(onehot, axis=0)
    rank = jnp.sum((csum - onehot) * onehot, axis=1)
    counts = csum[-1]
    padded = ((counts + tm - 1) // tm) * tm
    ends = jnp.cumsum(padded)
    starts = ends - padded
    dest = starts[experts] + rank
    n_rows = 2 * t + N_EXPERTS * tm
    row_token = jnp.zeros((n_rows,), jnp.int32).at[dest].set(jnp.arange(2 * t, dtype=jnp.int32) // 2)
    tile_start = jnp.arange(n_rows // tm, dtype=jnp.int32) * tm
    tile_expert = jnp.minimum(jnp.sum((tile_start[:, None] >= ends[None, :]).astype(jnp.int32), axis=1),
                              N_EXPERTS - 1)
    tile_valid = (tile_start < ends[-1]).astype(jnp.int32)
    real_end = (starts + counts)[tile_expert]
    tile_rows = jnp.clip(real_end - tile_start, 0, tm).astype(jnp.int32) * tile_valid
    return row_token, dest.astype(jnp.int32), tile_expert, tile_valid, tile_rows, n_rows


def _tiled(vec, reps):
    return jnp.tile(vec.astype(F32), reps)


def _mixer(h, seq, batch, layer, w_in, w_branch, w_out, rel_bias, cmp_pos, cmp_w1, cmp_w2,
           nsa_q_g, nsa_k_g, diff_q_g, diff_k_g, diff_lam, diff_out_g, lam_init, x2, mod_l, d):
    scale = HEAD_DIM ** -0.5
    ones = lambda n: jnp.ones((n,), F32)
    zeros = lambda n: jnp.zeros((n,), F32)
    proj = functools.partial(_matmul_w32, h, w_in, (layer,), tm=1024, tn=512)

    cs = jnp.concatenate([jnp.full((BRANCH_W,), scale, F32), ones(2 * BRANCH_W)]).reshape(1, -1)
    p_sb = proj(_C_SB, 3 * BRANCH_W, [(cs, "col")], _ep_scale, BF16, name="proj_sb")
    nf = ones(BRANCH_W).reshape(1, -1)
    cs = (_tiled(nsa_q_g, N_HEADS) * scale).reshape(1, -1)
    p_nsq = proj(_C_NSQ, BRANCH_W, [(nf, "col"), (cs, "col")],
                 functools.partial(_ep_norm, width=HEAD_DIM), BF16, name="proj_nsq")
    nf = jnp.concatenate([zeros(2 * KV_W), ones(KV_W), zeros(KV_W), ones(KV_W), zeros(KV_W)]).reshape(1, -1)
    cs = jnp.concatenate([ones(2 * KV_W), _tiled(nsa_k_g[1], KV_HEADS), ones(KV_W),
                          _tiled(nsa_k_g[2], KV_HEADS), ones(KV_W)]).reshape(1, -1)
    p_nskv = proj(_C_NSKV, 6 * KV_W, [(nf, "col"), (cs, "col")],
                  functools.partial(_ep_norm, width=HEAD_DIM), BF16, name="proj_nskv")
    ns_gates = proj(_C_GNS, LANES, [], _ep_sigmoid, F32, name="proj_nsgate")
    nf = jnp.concatenate([ones(2 * BRANCH_W), zeros(BRANCH_W)]).reshape(1, -1)
    cs = jnp.concatenate([_tiled(diff_q_g, 2 * N_HEADS) * DIFF_QK ** -0.5, _tiled(diff_k_g, 2 * N_HEADS),
                          ones(BRANCH_W)]).reshape(1, -1)
    p_df = proj(_C_DF, 3 * BRANCH_W, [(nf, "col"), (cs, "col")],
                functools.partial(_ep_norm, width=DIFF_QK), BF16, name="proj_diff")
    m_gates = proj(_C_MERGE, 3 * d, [], _ep_sigmoid, BF16, name="proj_merge_gate")

    o_sb = _sb_attention(p_sb, batch, seq)
    kvcm = _compress(p_nskv, cmp_pos, cmp_w1, cmp_w2, nsa_k_g[0], batch, seq)
    o_ns = _nsa_attention(p_nsq, kvcm, p_nskv, ns_gates, rel_bias[:, :N_HEADS], nsa_q_g, nsa_k_g, batch, seq)
    o_df = _diff_attention(p_df, rel_bias[:, N_HEADS:], diff_lam, diff_out_g, diff_q_g, diff_k_g, lam_init,
                           batch, seq)
    merged = _branch_merge(o_sb, o_ns, o_df, w_branch, layer, m_gates)
    g1 = lax.slice_in_dim(mod_l, 2 * d, 3 * d, axis=1)
    return _matmul_w32(merged, w_out, (layer,), 0, d, [(x2, "tile"), (g1, "batch")], _ep_residual, F32,
                       tm=1024, tn=512, name="proj_out", seq=seq)


def _moe(x2, seq, mod_l, norm_gain, w_router, router_bias, layer, w_gate, w_up, w_down):
    tm = 256
    h2, route = _norm_router(x2, norm_gain, mod_l, 3, 4, seq, w_router, router_bias)
    row_token, pos, tile_expert, tile_valid, tile_rows, n_rows = _routing_tables(route, tm)
    xs = _gather_rows(row_token, tile_rows, h2, n_rows, tm)
    y = _expert_mlps(tile_expert, tile_valid, xs, layer, w_gate, w_up, w_down, tm)
    return _combine(pos, y, x2, mod_l, 5, route, seq)


def kernel(x, c, rel_bias, w_router, router_bias, w_ada, b_ada, norm_mix, norm_ffn, w_in, nsa_cmp_pos,
           nsa_cmp_w1, nsa_cmp_w2, nsa_q_norm, nsa_k_norm, diff_q_norm, diff_k_norm, diff_lambda,
           diff_out_norm, w_branch, w_out, w_exp_gate, w_exp_up, w_exp_down):
    batch, seq, d = x.shape
    depth = w_ada.shape[0]
    mod = _adaln(c, w_ada, b_ada)
    x2 = x.reshape(batch * seq, d)
    for layer in range(depth):
        lam_init = 0.8 - 0.6 * math.exp(-0.3 * layer)
        mod_l = mod[layer]
        h = _norm_mod(x2, norm_mix[layer], mod_l, 0, 1, seq)
        x2 = _mixer(h, seq, batch, layer, w_in, w_branch, w_out, rel_bias,
                    nsa_cmp_pos[layer], nsa_cmp_w1[layer], nsa_cmp_w2[layer], nsa_q_norm[layer],
                    nsa_k_norm[layer], diff_q_norm[layer], diff_k_norm[layer], diff_lambda[layer],
                    diff_out_norm[layer], lam_init, x2, mod_l, d)
        x2 = _moe(x2, seq, mod_l, norm_ffn[layer], w_router, router_bias, layer, w_exp_gate,
                  w_exp_up, w_exp_down)
    return x2.reshape(batch, seq, d)
```

```python
import functools
import math

import numpy as np
import jax
import jax.numpy as jnp
from jax import lax
from jax.experimental import pallas as pl
from jax.experimental.pallas import tpu as pltpu

F32 = jnp.float32
BF16 = jnp.bfloat16

HEAD_DIM = 128
N_HEADS = 8
KV_HEADS = 2
Q_PER_KV = N_HEADS // KV_HEADS
DIFF_QK = HEAD_DIM // 2
BRANCH_W = N_HEADS * HEAD_DIM
KV_W = KV_HEADS * HEAD_DIM
CMP_LEN = 32
CMP_STRIDE = 16
SLC_LEN = 64
N_SLC = 16
N_LOCAL_SLC = 2
WINDOW = 512
REL_BUCKETS = 32
REL_MAX_DIST = 128
N_EXPERTS = 16
N_GROUPS = 4
EXP_PER_GROUP = N_EXPERTS // N_GROUPS
NORM_EPS = 1e-6
NEG_INF = -1e30
MASKED_BELOW = -1e29
FORCE_SCORE = 1e9
LANES = 128
SUBLANES = 8
VMEM_CAP = 56 * 1024 * 1024
SB_EXIT = 100.0

_C_SB = 0
_C_NSQ = 3 * BRANCH_W
_C_NSKV = _C_NSQ + BRANCH_W
_C_GNS = _C_NSKV + 6 * KV_W
_C_DF = _C_GNS + 3 * N_HEADS
_C_MERGE = _C_DF + 3 * BRANCH_W


def _cparams(sem, vmem_bytes):
    lim = int(min(VMEM_CAP, max(vmem_bytes * 5 // 4 + (2 << 20), 16 << 20)))
    return pltpu.CompilerParams(dimension_semantics=sem, vmem_limit_bytes=lim)


def _dot(a, b):
    return jnp.dot(a, b, preferred_element_type=F32)


def _dot_nt(a, b):
    return lax.dot_general(a, b, (((1,), (1,)), ((), ())), preferred_element_type=F32)


def _split_bf16(x):
    hi = x.astype(BF16)
    lo = (x - hi.astype(F32)).astype(BF16)
    return hi, lo


def _ada_kernel(cb_ref, w_ref, b_ref, o_ref, ca_sc, *, nb, rows_per_iter):
    k_dim, tn = w_ref.shape

    @pl.when((pl.program_id(0) == 0) & (pl.program_id(1) == 0))
    def _():
        cb = cb_ref[...]
        ca_sc[...] = cb * jax.nn.sigmoid(cb)

    reps = tn // LANES
    groups = rows_per_iter // SUBLANES

    def body(i, accs):
        r0 = pl.multiple_of(i * rows_per_iter, rows_per_iter)
        w3 = w_ref[pl.ds(r0, rows_per_iter), :].reshape(groups, SUBLANES, tn)
        new = []
        for b in range(nb):
            ca = ca_sc[b, pl.ds(r0, rows_per_iter), :].reshape(groups, SUBLANES, LANES)
            ca = jnp.concatenate([ca] * reps, axis=2)
            new.append(accs[b] + jnp.sum(w3 * ca, axis=0))
        return tuple(new)

    accs = lax.fori_loop(0, k_dim // rows_per_iter, body,
                         tuple(jnp.zeros((SUBLANES, tn), F32) for _ in range(nb)))
    rows = [jnp.sum(a, axis=0, keepdims=True) + b_ref[...] for a in accs]
    rows.append(jnp.zeros((SUBLANES - nb, tn), F32))
    o_ref[...] = jnp.concatenate(rows, axis=0)


def _adaln(c, w_ada, b_ada):
    depth, d, n = w_ada.shape
    nb = c.shape[0]
    tn = 512
    cb = jnp.broadcast_to(c[:, :, None], (nb, d, LANES))
    vmem = 2 * d * tn * 4 + 3 * nb * d * LANES * 4
    return pl.pallas_call(
        functools.partial(_ada_kernel, nb=nb, rows_per_iter=64),
        grid=(depth, n // tn),
        in_specs=[pl.BlockSpec((nb, d, LANES), lambda l, j: (0, 0, 0)),
                  pl.BlockSpec((None, d, tn), lambda l, j: (l, 0, j)),
                  pl.BlockSpec((None, 1, tn), lambda l, j: (l, 0, j))],
        out_specs=pl.BlockSpec((None, SUBLANES, tn), lambda l, j: (l, 0, j)),
        out_shape=jax.ShapeDtypeStruct((depth, SUBLANES, n), F32),
        scratch_shapes=[pltpu.VMEM((nb, d, LANES), F32)],
        compiler_params=_cparams(("arbitrary", "arbitrary"), vmem),
        name="adaln_mod",
    )(cb, w_ada, b_ada.reshape(depth, 1, n))


def _modulated_norm(x_ref, gain_ref, sc_ref, sh_ref, tiles_per_batch):
    b = pl.program_id(0) // tiles_per_batch
    x = x_ref[...]
    ms = jnp.mean(x * x, axis=-1, keepdims=True)
    y = x * lax.rsqrt(ms + NORM_EPS) * gain_ref[...]
    return y * (1.0 + sc_ref[pl.ds(b, 1), :]) + sh_ref[pl.ds(b, 1), :]


def _norm_kernel(x_ref, gain_ref, sc_ref, sh_ref, o_ref, *, tiles_per_batch):
    o_ref[...] = _modulated_norm(x_ref, gain_ref, sc_ref, sh_ref, tiles_per_batch).astype(o_ref.dtype)


def _norm_mod(x2, gain, mod_l, sh_chunk, sc_chunk, seq):
    t, d = x2.shape
    tm = 256
    return pl.pallas_call(
        functools.partial(_norm_kernel, tiles_per_batch=seq // tm),
        grid=(t // tm,),
        in_specs=[pl.BlockSpec((tm, d), lambda i: (i, 0)),
                  pl.BlockSpec((1, d), lambda i: (0, 0)),
                  pl.BlockSpec((SUBLANES, d), lambda i: (0, sc_chunk)),
                  pl.BlockSpec((SUBLANES, d), lambda i: (0, sh_chunk))],
        out_specs=pl.BlockSpec((tm, d), lambda i: (i, 0)),
        out_shape=jax.ShapeDtypeStruct((t, d), BF16),
        compiler_params=_cparams(("arbitrary",), 2 * tm * d * 6 + 4 * SUBLANES * d * 4),
        name="norm_mix",
    )(x2, gain.reshape(1, d), mod_l, mod_l)


def _norm_router_kernel(x_ref, gain_ref, sc_ref, sh_ref, wr_ref, rb_ref, h_ref, route_ref,
                        *, tiles_per_batch):
    h = _modulated_norm(x_ref, gain_ref, sc_ref, sh_ref, tiles_per_batch)
    h_ref[...] = h
    h_hi, h_lo = _split_bf16(h)
    w_hi, w_lo = _split_bf16(wr_ref[...])
    logits = _dot(h_hi, w_hi) + _dot(h_hi, w_lo) + _dot(h_lo, w_hi)
    aff = jax.nn.sigmoid(logits)
    biased = aff + rb_ref[...]
    lane_i = lax.broadcasted_iota(jnp.int32, aff.shape, 1)
    lane = lane_i.astype(F32)
    low = jnp.float32(-3e38)
    big = jnp.float32(1 << 20)
    best = None
    for gi in range(N_GROUPS):
        in_group = (lane_i // EXP_PER_GROUP) == gi
        v = jnp.where(in_group, biased, low)
        m1 = jnp.max(v, axis=-1, keepdims=True)
        i1 = jnp.min(jnp.where(v == m1, lane, big), axis=-1, keepdims=True)
        v2 = jnp.where(lane == i1, low, v)
        m2 = jnp.max(v2, axis=-1, keepdims=True)
        i2 = jnp.min(jnp.where(v2 == m2, lane, big), axis=-1, keepdims=True)
        score = m1 + m2
        if best is None:
            best = (score, i1, i2)
        else:
            better = score > best[0]
            best = (jnp.where(better, score, best[0]), jnp.where(better, i1, best[1]),
                    jnp.where(better, i2, best[2]))
    _, e1, e2 = best
    a1 = jnp.sum(jnp.where(lane == e1, aff, 0.0), axis=-1, keepdims=True)
    a2 = jnp.sum(jnp.where(lane == e2, aff, 0.0), axis=-1, keepdims=True)
    den = a1 + a2
    out = jnp.where(lane_i == 0, e1,
                    jnp.where(lane_i == 1, e2,
                              jnp.where(lane_i == 2, a1 / den, jnp.where(lane_i == 3, a2 / den, 0.0))))
    route_ref[...] = out


def _norm_router(x2, gain, mod_l, sh_chunk, sc_chunk, seq, w_router, router_bias):
    t, d = x2.shape
    tm = 256
    wr = jnp.pad(w_router, ((0, 0), (0, LANES - N_EXPERTS)))
    rb = jnp.pad(router_bias, (0, LANES - N_EXPERTS)).reshape(1, LANES)
    return pl.pallas_call(
        functools.partial(_norm_router_kernel, tiles_per_batch=seq // tm),
        grid=(t // tm,),
        in_specs=[pl.BlockSpec((tm, d), lambda i: (i, 0)),
                  pl.BlockSpec((1, d), lambda i: (0, 0)),
                  pl.BlockSpec((SUBLANES, d), lambda i: (0, sc_chunk)),
                  pl.BlockSpec((SUBLANES, d), lambda i: (0, sh_chunk)),
                  pl.BlockSpec((d, LANES), lambda i: (0, 0)),
                  pl.BlockSpec((1, LANES), lambda i: (0, 0))],
        out_specs=[pl.BlockSpec((tm, d), lambda i: (i, 0)),
                   pl.BlockSpec((tm, LANES), lambda i: (i, 0))],
        out_shape=[jax.ShapeDtypeStruct((t, d), F32),
                   jax.ShapeDtypeStruct((t, LANES), F32)],
        compiler_params=_cparams(("arbitrary",), 2 * tm * d * 8 + 4 * d * LANES * 4 + 8 * tm * d),
        name="norm_ffn_router",
    )(x2, gain.reshape(1, d), mod_l, mod_l, wr, rb)


def _mm_w32_kernel(a_ref, w_ref, *rest, epilogue, n_extra, tiles_per_batch, transposed):
    extras = rest[:n_extra]
    o_ref, w_sc = rest[n_extra], rest[n_extra + 1]

    @pl.when(pl.program_id(1) == 0)
    def _():
        w_sc[...] = w_ref[...].reshape(w_sc.shape).astype(BF16)

    acc = _dot_nt(a_ref[...], w_sc[...]) if transposed else _dot(a_ref[...], w_sc[...])
    o_ref[...] = epilogue(acc, extras, pl.program_id(1) // tiles_per_batch).astype(o_ref.dtype)


def _matmul_w32(a, w, lead, col_lo, n, extras, epilogue, out_dtype, *, tm, tn, name, seq=None,
                transposed=False):
    m, k = a.shape
    tm = min(tm, m)
    tn = min(tn, n)
    assert m % tm == 0 and n % tn == 0
    none = (None,) * len(lead)
    if transposed:
        assert col_lo % SUBLANES == 0
        w_spec = pl.BlockSpec(tuple(pl.Element(1) for _ in lead) + (pl.Element(tn), pl.Element(k)),
                              lambda j, i: lead + (pl.multiple_of(col_lo + j * tn, SUBLANES), 0))
    else:
        assert col_lo % tn == 0
        w_spec = pl.BlockSpec(none + (k, tn), lambda j, i: lead + (0, col_lo // tn + j))
    in_specs = [pl.BlockSpec((tm, k), lambda j, i: (i, 0)), w_spec]
    args = [a, w]
    vmem = 2 * tm * k * 2 + 2 * k * tn * 4 + k * tn * 2 + 2 * tm * tn * jnp.dtype(out_dtype).itemsize + 2 * tm * tn * 4
    for arr, kind in extras:
        if kind == "tile":
            in_specs.append(pl.BlockSpec((tm, tn), lambda j, i: (i, j)))
            vmem += 2 * tm * tn * arr.dtype.itemsize
        else:
            rows = 1 if kind == "col" else SUBLANES
            in_specs.append(pl.BlockSpec((rows, tn), lambda j, i: (0, j)))
            vmem += 2 * SUBLANES * tn * 4
        args.append(arr)
    tiles_per_batch = (seq // tm) if seq else 1
    return pl.pallas_call(
        functools.partial(_mm_w32_kernel, epilogue=epilogue, n_extra=len(extras),
                          tiles_per_batch=tiles_per_batch, transposed=transposed),
        grid=(n // tn, m // tm),
        in_specs=in_specs,
        out_specs=pl.BlockSpec((tm, tn), lambda j, i: (i, j)),
        out_shape=jax.ShapeDtypeStruct((m, n), out_dtype),
        scratch_shapes=[pltpu.VMEM((tn, k) if transposed else (k, tn), BF16)],
        compiler_params=_cparams(("arbitrary", "arbitrary"), vmem),
        name=name,
    )(*args)


def _ep_scale(acc, extras, b):
    return acc * extras[0][...]


def _ep_sigmoid(acc, extras, b):
    return jax.nn.sigmoid(acc)


def _ep_norm(acc, extras, b, *, width):
    nf_ref, cs_ref = extras
    outs = []
    for c in range(acc.shape[1] // LANES):
        sl = slice(c * LANES, (c + 1) * LANES)
        blk = acc[:, sl]
        sq = blk * blk
        if width == LANES:
            ms = jnp.mean(sq, axis=-1, keepdims=True)
        else:
            lane = lax.broadcasted_iota(jnp.int32, blk.shape, 1)
            first = lane < width
            lo = jnp.sum(jnp.where(first, sq, 0.0), axis=-1, keepdims=True) / width
            hi = jnp.sum(jnp.where(first, 0.0, sq), axis=-1, keepdims=True) / width
            ms = jnp.where(first, lo, hi)
        inv = lax.rsqrt(ms + NORM_EPS)
        f = jnp.where(nf_ref[:, sl] > 0.0, inv, 1.0)
        outs.append(blk * f * cs_ref[:, sl])
    return jnp.concatenate(outs, axis=1)


def _ep_residual(acc, extras, b):
    x_ref, g_ref = extras
    return x_ref[...] + g_ref[pl.ds(b, 1), :] * acc


def _bucket_np(dist):
    n = np.maximum(dist, 0)
    max_exact = REL_BUCKETS // 2
    nf = np.maximum(n, max_exact).astype(np.float32)
    large = max_exact + (np.log(nf / np.float32(max_exact))
                         / np.float32(math.log(REL_MAX_DIST / max_exact))
                         * np.float32(REL_BUCKETS - max_exact)).astype(np.int32)
    large = np.minimum(large, REL_BUCKETS - 1)
    return np.where(n < max_exact, n, large).astype(np.int32)


def _bucket_starts():
    b = _bucket_np(np.arange(4 * REL_MAX_DIST))
    return [int(np.argmax(b >= k)) for k in range(1, REL_BUCKETS)]


def _delta_bias(table, dist, valid):
    tb = table.T.astype(F32)
    shape = (tb.shape[0],) + (1,) * dist.ndim
    val = jnp.broadcast_to(tb[:, 0].reshape(shape), (tb.shape[0],) + dist.shape)
    for k, start in enumerate(_bucket_starts(), start=1):
        val = jnp.where(dist[None] >= start, tb[:, k].reshape(shape), val)
    val = val - tb[:, REL_BUCKETS - 1].reshape(shape)
    return jnp.where(valid[None], val, NEG_INF)


def _toeplitz_dist(n_off, step, nr, nc):
    shape = (n_off, nr, nc)
    return (lax.broadcasted_iota(jnp.int32, shape, 0) * step + lax.broadcasted_iota(jnp.int32, shape, 1)
            - lax.broadcasted_iota(jnp.int32, shape, 2))


def _stack_group_heads(tiles):
    nh, k, nr, nc = tiles.shape
    t = tiles.reshape(KV_HEADS, Q_PER_KV, k, nr, nc)
    return jnp.transpose(t, (0, 2, 1, 3, 4)).reshape(KV_HEADS, k, Q_PER_KV * nr, nc)


def _sb_kernel(q_ref, k_ref, v_ref, o_ref, *, tq):
    qi = pl.program_id(2)
    q = q_ref[...]
    row = lax.broadcasted_iota(jnp.int32, (tq, tq), 0)
    col = lax.broadcasted_iota(jnp.int32, (tq, tq), 1)
    later = jnp.where(row > col, 1.0, 0.0).astype(BF16)
    causal = col < row

    def step(kj, carry, acc, diag):
        k0 = pl.multiple_of(kj * tq, tq)
        k = k_ref[pl.ds(k0, tq), :]
        v = v_ref[pl.ds(k0, tq), :]
        z = _dot_nt(q, k)
        log_succ = jnp.minimum(z, 0.0) - jnp.log1p(jnp.exp(-jnp.abs(z)))
        log_fail = log_succ - z
        if diag:
            log_fail = jnp.where(causal, log_fail, 0.0)
        hi, lo = _split_bf16(log_fail)
        suffix = _dot(hi, later) + _dot(lo, later)
        a = jnp.exp(log_succ + suffix + carry)
        if diag:
            a = jnp.where(causal, a, 0.0)
        acc = acc + _dot(a.astype(BF16), v)
        carry = carry + jnp.sum(log_fail, axis=-1, keepdims=True)
        return carry, acc

    carry, acc = step(qi, jnp.zeros((tq, 1), F32), jnp.zeros((tq, HEAD_DIM), F32), True)

    def cond(state):
        kj, carry, _ = state
        return (kj >= 0) & (jnp.max(carry) > -SB_EXIT)

    def body(state):
        kj, carry, acc = state
        carry, acc = step(kj, carry, acc, False)
        return kj - 1, carry, acc

    _, _, acc = lax.while_loop(cond, body, (qi - 1, carry, acc))
    o_ref[...] = acc.astype(o_ref.dtype)


def _sb_attention(p_sb, batch, seq):
    tq = min(256, seq)
    nq = seq // tq
    t = batch * seq
    vmem = 4 * seq * HEAD_DIM * 2 + 4 * tq * HEAD_DIM * 2 + 16 * tq * tq * 4
    return pl.pallas_call(
        functools.partial(_sb_kernel, tq=tq),
        grid=(batch, N_HEADS, nq),
        in_specs=[pl.BlockSpec((tq, HEAD_DIM), lambda b, h, i: (b * nq + i, h)),
                  pl.BlockSpec((seq, HEAD_DIM), lambda b, h, i: (b, N_HEADS + h)),
                  pl.BlockSpec((seq, HEAD_DIM), lambda b, h, i: (b, 2 * N_HEADS + h))],
        out_specs=pl.BlockSpec((tq, HEAD_DIM), lambda b, h, i: (b * nq + i, h)),
        out_shape=jax.ShapeDtypeStruct((t, BRANCH_W), BF16),
        compiler_params=_cparams(("arbitrary",) * 3, vmem),
        name="sb_attention",
    )(p_sb, p_sb, p_sb)


LOGIT_CAP = 40.0


def _with_ones(v):
    return jnp.concatenate([v, jnp.ones_like(v)], axis=1)


def _softmax_first(s, v, m_sc, acc_sc, bounded):
    if bounded:
        acc_sc[...] = _dot(jnp.exp(s).astype(BF16), _with_ones(v))
        return
    m = jnp.max(s, axis=-1, keepdims=True)
    m_sc[...] = m
    acc_sc[...] = _dot(jnp.exp(s - m).astype(BF16), _with_ones(v))


def _softmax_next(s, v, m_sc, acc_sc, bounded):
    if bounded:
        acc_sc[...] = acc_sc[...] + _dot(jnp.exp(s).astype(BF16), _with_ones(v))
        return
    m_old = m_sc[...]
    m_new = jnp.maximum(m_old, jnp.max(s, axis=-1, keepdims=True))
    p = jnp.exp(s - m_new)
    acc_sc[...] = jnp.exp(m_old - m_new) * acc_sc[...] + _dot(p.astype(BF16), _with_ones(v))
    m_sc[...] = m_new


def _softmax_result(acc_sc, bounded):
    acc = acc_sc[...]
    total = acc[:, HEAD_DIM:]
    den = jnp.where(total > 0.0, total, 1.0) if bounded else jnp.maximum(total, 1.0)
    return acc[:, :HEAD_DIM] / den


def _logit_bound_ok(width, scale, q_gain, k_gains, table):
    qk = width * scale * jnp.max(jnp.abs(q_gain)) * jnp.max(jnp.abs(k_gains)) * 1.02
    delta = jnp.max(jnp.abs(table - table[REL_BUCKETS - 1:]))
    return (qk + delta <= LOGIT_CAP).astype(jnp.int32).reshape(1)


def _compress_kernel(x_ref, p_ref, w1a_ref, w1b_ref, w2_ref, gain_ref, o_ref):
    nc = x_ref.shape[0]
    is_key = pl.program_id(1) < KV_HEADS
    x = x_ref[...].astype(F32)
    first = _dot((x + p_ref[0:1, :]).astype(BF16), w1a_ref[...])
    second = _dot((x + p_ref[1:2, :]).astype(BF16), w1b_ref[...])
    pre = first + pltpu.roll(second, nc - 1, 0)
    hid = pre * jax.nn.sigmoid(pre)
    out = _dot(hid.astype(BF16), w2_ref[...])
    ms = jnp.mean(out * out, axis=-1, keepdims=True)
    normed = out * lax.rsqrt(ms + NORM_EPS) * gain_ref[...]
    o_ref[...] = jnp.where(is_key, normed, out).astype(o_ref.dtype)


def _compress(p_nskv, cmp_pos, cmp_w1, cmp_w2, k_gain0, batch, seq):
    nc = seq // CMP_STRIDE
    half = CMP_STRIDE * HEAD_DIM
    x = p_nskv[:, :2 * KV_W].reshape(batch, nc, CMP_STRIDE, 2 * KV_HEADS, HEAD_DIM)
    x = jnp.transpose(x, (0, 3, 1, 2, 4)).reshape(batch, 2 * KV_HEADS, nc, half)
    pos = cmp_pos.reshape(2, 2, half)
    w1 = cmp_w1.astype(BF16).reshape(2, 2, half, HEAD_DIM)
    w2 = cmp_w2.astype(BF16)
    return pl.pallas_call(
        _compress_kernel,
        grid=(batch, 2 * KV_HEADS),
        in_specs=[pl.BlockSpec((None, None, nc, half), lambda b, j: (b, j, 0, 0)),
                  pl.BlockSpec((None, 2, half), lambda b, j: (j // KV_HEADS, 0, 0)),
                  pl.BlockSpec((None, None, half, HEAD_DIM), lambda b, j: (j // KV_HEADS, 0, 0, 0)),
                  pl.BlockSpec((None, None, half, HEAD_DIM), lambda b, j: (j // KV_HEADS, 1, 0, 0)),
                  pl.BlockSpec((None, HEAD_DIM, HEAD_DIM), lambda b, j: (j // KV_HEADS, 0, 0)),
                  pl.BlockSpec((1, HEAD_DIM), lambda b, j: (0, 0))],
        out_specs=pl.BlockSpec((None, None, nc, HEAD_DIM), lambda b, j: (b, j, 0, 0)),
        out_shape=jax.ShapeDtypeStruct((batch, 2 * KV_HEADS, nc, HEAD_DIM), BF16),
        compiler_params=_cparams(("arbitrary", "arbitrary"), 8 * nc * half * 2 + 8 * half * HEAD_DIM * 2),
        name="nsa_compress",
    )(x, pos, w1, w1, w2, k_gain0.reshape(1, HEAD_DIM))


def _nsa_kernel(ok_ref, q_ref, kcm_ref, vcm_ref, ks_ref, vs_ref, kw_ref, vw_ref, gate_ref,
                pt_ref, bt_ref, wt_ref, o_ref, m_sc, acc_sc, *, tq, tk, tw, n_blk, k_sel):
    qi = pl.program_id(2)
    q0 = qi * tq
    rows = Q_PER_KV * tq
    nc = kcm_ref.shape[0]
    q4 = jnp.concatenate([q_ref[:, r * HEAD_DIM:(r + 1) * HEAD_DIM] for r in range(Q_PER_KV)], axis=0)

    per_tile = tq // CMP_STRIDE
    lead = 2 * per_tile
    shift = lax.rem(qi * per_tile + (nc - lead), nc)
    cb = pltpu.roll(pt_ref[...], shift, 1)
    rel = lax.broadcasted_iota(jnp.int32, (rows, nc), 1) - qi * per_tile
    cb = jnp.where(rel < -lead, 0.0, jnp.where(rel >= per_tile, NEG_INF, cb))
    s = _dot_nt(q4, kcm_ref[...]) + cb
    valid = s > MASKED_BELOW
    s = jnp.where(valid, s, NEG_INF)
    p = jnp.where(valid, jnp.exp(s - jnp.max(s, axis=-1, keepdims=True)), 0.0)
    p = p / jnp.maximum(jnp.sum(p, axis=-1, keepdims=True), 1.0)
    o_cmp = _dot(p.astype(BF16), vcm_ref[...])

    psum = p[0:tq]
    for r in range(1, Q_PER_KV):
        psum = psum + p[r * tq:(r + 1) * tq]
    n_id = lax.broadcasted_iota(jnp.int32, (nc, LANES), 0)
    j_id = lax.broadcasted_iota(jnp.int32, (nc, LANES), 1)
    overlap = ((CMP_STRIDE * n_id < SLC_LEN * j_id + SLC_LEN)
               & (CMP_STRIDE * n_id + CMP_LEN > SLC_LEN * j_id) & (j_id < n_blk))
    overlap = jnp.where(overlap, 1.0, 0.0).astype(BF16)
    p_hi, p_lo = _split_bf16(psum)
    imp = _dot(p_hi, overlap) + _dot(p_lo, overlap)
    nr = -(-n_blk // SUBLANES) * SUBLANES
    imp_t = jnp.transpose(imp)[:nr]
    blk = lax.broadcasted_iota(jnp.int32, (nr, tq), 0)
    cur = (q0 + lax.broadcasted_iota(jnp.int32, (nr, tq), 1)) // SLC_LEN
    forced = (blk == 0) | ((blk <= cur) & (blk > cur - N_LOCAL_SLC))
    imp_t = jnp.where(forced, FORCE_SCORE, jnp.where(blk > cur, NEG_INF, imp_t))
    imp_t = jnp.where(blk < n_blk, imp_t, -3e38)
    rank = jnp.zeros((nr, tq), F32)
    for j in range(n_blk):
        cj = imp_t[j:j + 1, :]
        ahead = jnp.where(blk > j, jnp.where(cj >= imp_t, 1.0, 0.0), jnp.where(cj > imp_t, 1.0, 0.0))
        rank = rank + ahead
    sel_t = jnp.where((rank < k_sel) & (blk < n_blk), 1.0, 0.0)
    if nr < LANES:
        sel_t = jnp.concatenate([sel_t, jnp.zeros((LANES - nr, tq), F32)], axis=0)
    sel = jnp.transpose(sel_t).astype(BF16)

    per = tk // tq
    kd = qi // per
    par = qi - kd * per
    e_row = lax.broadcasted_iota(jnp.int32, (LANES, tk), 0)
    e_col = lax.broadcasted_iota(jnp.int32, (LANES, tk), 1)

    def slc_scores(kj, bias):
        k0 = pl.multiple_of(kj * tk, tk)
        expand = jnp.where((k0 + e_col) // SLC_LEN == e_row, 1.0, 0.0).astype(BF16)
        chosen = _dot(sel, expand)
        s = _dot_nt(q4, ks_ref[pl.ds(k0, tk), :])
        if bias is not None:
            s = s + bias
        s = jnp.where(chosen[None] > 0.5, s.reshape(Q_PER_KV, tq, tk), NEG_INF).reshape(rows, tk)
        return s, vs_ref[pl.ds(k0, tk), :]

    def attend(bounded):
        s, v = slc_scores(kd, bt_ref[par])
        _softmax_first(s, v, m_sc, acc_sc, bounded)

        @pl.when(kd >= 1)
        def _():
            s, v = slc_scores(kd - 1, bt_ref[per + par])
            _softmax_next(s, v, m_sc, acc_sc, bounded)

        def slc_far(i, _):
            s, v = slc_scores(kd - 2 - i, None)
            _softmax_next(s, v, m_sc, acc_sc, bounded)
            return 0

        lax.fori_loop(0, jnp.maximum(kd - 1, 0), slc_far, 0)
        o_slc = _softmax_result(acc_sc, bounded)

        start = pl.multiple_of(jnp.maximum(q0 - WINDOW, 0), tq)
        s = _dot_nt(q4, kw_ref[pl.ds(start, tw), :]) + wt_ref[jnp.minimum(qi, WINDOW // tq)]
        _softmax_first(s, vw_ref[pl.ds(start, tw), :], m_sc, acc_sc, bounded)
        o_win = _softmax_result(acc_sc, bounded)

        gate = gate_ref[...]
        first_group = pl.program_id(1) == 0

        def gate_col(branch, r):
            lo = branch * N_HEADS + r
            hi = lo + Q_PER_KV
            return jnp.where(first_group, gate[:, lo:lo + 1], gate[:, hi:hi + 1])

        for r in range(Q_PER_KV):
            sl = slice(r * tq, (r + 1) * tq)
            o = gate_col(0, r) * o_cmp[sl] + gate_col(1, r) * o_slc[sl] + gate_col(2, r) * o_win[sl]
            o_ref[:, r * HEAD_DIM:(r + 1) * HEAD_DIM] = o.astype(o_ref.dtype)

    @pl.when(ok_ref[0] > 0)
    def _():
        attend(True)

    @pl.when(ok_ref[0] == 0)
    def _():
        attend(False)


def _nsa_attention(p_nsq, kvcm, p_nskv, gates, rel_nsa, q_gain, k_gains, batch, seq):
    tq = 128
    tk = min(512, seq)
    tw = WINDOW + tq
    nq = seq // tq
    nc = seq // CMP_STRIDE
    n_blk = seq // SLC_LEN
    k_sel = min(N_SLC, n_blk)
    t = batch * seq
    rows = Q_PER_KV * tq
    per = tk // tq
    assert n_blk <= LANES and nc % LANES == 0 and WINDOW % tq == 0 and tw <= seq and tq == LANES

    d = _toeplitz_dist(2 * per, tq, tq, tk)
    bt = _stack_group_heads(_delta_bias(rel_nsa, d, d >= 0))
    d = _toeplitz_dist(WINDOW // tq + 1, tq, tq, tw)
    wt = _stack_group_heads(_delta_bias(rel_nsa, d, (d >= 0) & (d < WINDOW)))
    per_tile = tq // CMP_STRIDE
    d = (lax.broadcasted_iota(jnp.int32, (1, tq, nc), 1)
         - CMP_STRIDE * (lax.broadcasted_iota(jnp.int32, (1, tq, nc), 2) - 2 * per_tile) - (CMP_LEN - 1))
    pt = _stack_group_heads(_delta_bias(rel_nsa, d, d >= 0))[:, 0]

    ok = _logit_bound_ok(HEAD_DIM, HEAD_DIM ** -0.5, q_gain, k_gains[1:], rel_nsa)
    const = lambda shape: pl.BlockSpec((None,) + shape, lambda b, g, i, ok: (g,) + (0,) * len(shape),
                                       pipeline_mode=pl.Buffered(1))
    vmem = (8 * seq * HEAD_DIM * 2 + (bt.size + wt.size + pt.size) // KV_HEADS * 4
            + 10 * rows * tw * 4 + 8 * rows * nc * 4)
    return pl.pallas_call(
        functools.partial(_nsa_kernel, tq=tq, tk=tk, tw=tw, n_blk=n_blk, k_sel=k_sel),
        grid_spec=pltpu.PrefetchScalarGridSpec(
            num_scalar_prefetch=1,
            grid=(batch, KV_HEADS, nq),
            in_specs=[pl.BlockSpec((tq, Q_PER_KV * HEAD_DIM), lambda b, g, i, ok: (b * nq + i, g)),
                      pl.BlockSpec((None, None, nc, HEAD_DIM), lambda b, g, i, ok: (b, g, 0, 0)),
                      pl.BlockSpec((None, None, nc, HEAD_DIM), lambda b, g, i, ok: (b, KV_HEADS + g, 0, 0)),
                      pl.BlockSpec((seq, HEAD_DIM), lambda b, g, i, ok: (b, 2 * KV_HEADS + g)),
                      pl.BlockSpec((seq, HEAD_DIM), lambda b, g, i, ok: (b, 3 * KV_HEADS + g)),
                      pl.BlockSpec((seq, HEAD_DIM), lambda b, g, i, ok: (b, 4 * KV_HEADS + g)),
                      pl.BlockSpec((seq, HEAD_DIM), lambda b, g, i, ok: (b, 5 * KV_HEADS + g)),
                      pl.BlockSpec((tq, LANES), lambda b, g, i, ok: (b * nq + i, 0)),
                      const(pt.shape[1:]), const(bt.shape[1:]), const(wt.shape[1:])],
            out_specs=pl.BlockSpec((tq, Q_PER_KV * HEAD_DIM), lambda b, g, i, ok: (b * nq + i, g)),
            scratch_shapes=[pltpu.VMEM((rows, 1), F32), pltpu.VMEM((rows, 2 * HEAD_DIM), F32)]),
        out_shape=jax.ShapeDtypeStruct((t, BRANCH_W), BF16),
        compiler_params=_cparams(("arbitrary",) * 3, vmem),
        name="nsa_attention",
    )(ok, p_nsq, kvcm, kvcm, p_nskv, p_nskv, p_nskv, p_nskv, gates, pt, bt, wt)


def _diff_kernel(ok_ref, q_ref, k_ref, v_ref, dt_ref, lam_ref, og_ref, o_ref, m_sc, acc_sc,
                 *, tq, tk, lam_init):
    qi = pl.program_id(2)
    per = tk // tq
    kd = qi // per
    par = qi - kd * per
    q = q_ref[...]
    lane = lax.broadcasted_iota(jnp.int32, q.shape, 1)
    zero = jnp.zeros_like(q)
    q2 = jnp.concatenate([jnp.where(lane < DIFF_QK, q, zero), jnp.where(lane < DIFF_QK, zero, q)], axis=0)

    def scores(kj, bias):
        k0 = pl.multiple_of(kj * tk, tk)
        s = _dot_nt(q2, k_ref[pl.ds(k0, tk), :])
        if bias is not None:
            s = s + bias
        return s, v_ref[pl.ds(k0, tk), :]

    def attend(bounded):
        s, v = scores(kd, dt_ref[par])
        _softmax_first(s, v, m_sc, acc_sc, bounded)

        @pl.when(kd >= 1)
        def _():
            s, v = scores(kd - 1, dt_ref[per + par])
            _softmax_next(s, v, m_sc, acc_sc, bounded)

        def far(i, _):
            s, v = scores(kd - 2 - i, None)
            _softmax_next(s, v, m_sc, acc_sc, bounded)
            return 0

        lax.fori_loop(0, jnp.maximum(kd - 1, 0), far, 0)
        o2 = _softmax_result(acc_sc, bounded)
        lp = lam_ref[...]
        lam = (jnp.exp(jnp.sum(lp[0:1] * lp[1:2], axis=-1, keepdims=True))
               - jnp.exp(jnp.sum(lp[2:3] * lp[3:4], axis=-1, keepdims=True)) + lam_init)
        o = o2[:tq] - lam * o2[tq:]
        ms = jnp.mean(o * o, axis=-1, keepdims=True)
        o = o * lax.rsqrt(ms + NORM_EPS) * og_ref[...]
        o_ref[...] = (o * (1.0 - lam_init)).astype(o_ref.dtype)

    @pl.when(ok_ref[0] > 0)
    def _():
        attend(True)

    @pl.when(ok_ref[0] == 0)
    def _():
        attend(False)


def _diff_attention(p_df, rel_diff, lam_params, out_gain, q_gain, k_gain, lam_init, batch, seq):
    tq = min(256, seq)
    tk = min(512, seq)
    per = tk // tq
    nq = seq // tq
    t = batch * seq
    d = _toeplitz_dist(2 * per, tq, tq, tk)
    dt = _delta_bias(rel_diff, d, d >= 0)
    dt = jnp.concatenate([dt, dt], axis=2)
    ok = _logit_bound_ok(DIFF_QK, DIFF_QK ** -0.5, q_gain, k_gain, rel_diff)
    vmem = 4 * seq * HEAD_DIM * 2 + 2 * 2 * per * 2 * tq * tk * 4 + 12 * 2 * tq * tk * 4
    return pl.pallas_call(
        functools.partial(_diff_kernel, tq=tq, tk=tk, lam_init=lam_init),
        grid_spec=pltpu.PrefetchScalarGridSpec(
            num_scalar_prefetch=1,
            grid=(batch, N_HEADS, nq),
            in_specs=[pl.BlockSpec((tq, HEAD_DIM), lambda b, h, i, ok: (b * nq + i, h)),
                      pl.BlockSpec((seq, HEAD_DIM), lambda b, h, i, ok: (b, N_HEADS + h)),
                      pl.BlockSpec((seq, HEAD_DIM), lambda b, h, i, ok: (b, 2 * N_HEADS + h)),
                      pl.BlockSpec((None, 2 * per, 2 * tq, tk), lambda b, h, i, ok: (h, 0, 0, 0)),
                      pl.BlockSpec((4, DIFF_QK), lambda b, h, i, ok: (0, 0)),
                      pl.BlockSpec((1, HEAD_DIM), lambda b, h, i, ok: (0, 0))],
            out_specs=pl.BlockSpec((tq, HEAD_DIM), lambda b, h, i, ok: (b * nq + i, h)),
            scratch_shapes=[pltpu.VMEM((2 * tq, 1), F32), pltpu.VMEM((2 * tq, 2 * HEAD_DIM), F32)]),
        out_shape=jax.ShapeDtypeStruct((t, BRANCH_W), BF16),
        compiler_params=_cparams(("arbitrary",) * 3, vmem),
        name="diff_attention",
    )(ok, p_df, p_df, p_df, dt, lam_params, out_gain.reshape(1, HEAD_DIM))


def _merge_kernel(o0_ref, o1_ref, o2_ref, w_ref, g0_ref, g1_ref, g2_ref, out_ref, w_sc):
    @pl.when(pl.program_id(1) == 0)
    def _():
        w_sc[...] = w_ref[...].astype(BF16)

    acc = g0_ref[...].astype(F32) * _dot(o0_ref[...], w_sc[0])
    acc = acc + g1_ref[...].astype(F32) * _dot(o1_ref[...], w_sc[1])
    acc = acc + g2_ref[...].astype(F32) * _dot(o2_ref[...], w_sc[2])
    out_ref[...] = acc.astype(out_ref.dtype)


def _branch_merge(o_sb, o_ns, o_df, w_branch, layer, gates):
    t = o_sb.shape[0]
    d = w_branch.shape[3]
    tm = min(1024, t)
    tn = min(512, d)
    nj = d // tn
    o_spec = pl.BlockSpec((tm, BRANCH_W), lambda j, i: (i, 0))
    vmem = (2 * (3 * tm * BRANCH_W * 2 + 3 * BRANCH_W * tn * 4 + 3 * tm * tn * 2 + tm * tn * 2)
            + 3 * BRANCH_W * tn * 2 + 4 * tm * tn * 4)
    return pl.pallas_call(
        _merge_kernel,
        grid=(nj, t // tm),
        in_specs=[o_spec, o_spec, o_spec,
                  pl.BlockSpec((None, 3, BRANCH_W, tn), lambda j, i: (layer, 0, 0, j)),
                  pl.BlockSpec((tm, tn), lambda j, i: (i, j)),
                  pl.BlockSpec((tm, tn), lambda j, i: (i, nj + j)),
                  pl.BlockSpec((tm, tn), lambda j, i: (i, 2 * nj + j))],
        out_specs=pl.BlockSpec((tm, tn), lambda j, i: (i, j)),
        out_shape=jax.ShapeDtypeStruct((t, d), BF16),
        scratch_shapes=[pltpu.VMEM((3, BRANCH_W, tn), BF16)],
        compiler_params=_cparams(("arbitrary", "arbitrary"), vmem),
        name="branch_merge",
    )(o_sb, o_ns, o_df, w_branch, gates, gates, gates)


def _row_copy(src_ref, src_row, dst_ref, dst_row, sem):
    return pltpu.make_async_copy(src_ref.at[pl.ds(src_row, 1)], dst_ref.at[pl.ds(dst_row, 1)], sem)


def _gather_kernel(idx_ref, cnt_ref, src_ref, o_ref, buf, sem, *, tg):
    i = pl.program_id(0)
    base = i * tg
    n = cnt_ref[i]

    @pl.when(i == 0)
    def _():
        buf[...] = jnp.zeros(buf.shape, buf.dtype)

    for parity in range(2):
        def issue(r2, _, parity=parity):
            r = 2 * r2 + parity
            _row_copy(src_ref, idx_ref[base + r], buf, r, sem).start(priority=parity)
            return 0

        lax.fori_loop(0, (n + 1 - parity) // 2, issue, 0)

    def drain(r, _):
        _row_copy(src_ref, 0, buf, r, sem).wait()
        return 0

    lax.fori_loop(0, n, drain, 0)
    o_ref[...] = buf[...].astype(o_ref.dtype)


def _gather_rows(row_token, tile_rows, h2, n_rows, tg):
    d = h2.shape[1]
    return pl.pallas_call(
        functools.partial(_gather_kernel, tg=tg),
        grid_spec=pltpu.PrefetchScalarGridSpec(
            num_scalar_prefetch=2,
            grid=(n_rows // tg,),
            in_specs=[pl.BlockSpec(memory_space=pl.ANY)],
            out_specs=pl.BlockSpec((tg, d), lambda i, idx, cnt: (i, 0)),
            scratch_shapes=[pltpu.VMEM((tg, d), F32), pltpu.SemaphoreType.DMA(())]),
        out_shape=jax.ShapeDtypeStruct((n_rows, d), BF16),
        compiler_params=_cparams(("arbitrary",), tg * d * 4 + 2 * tg * d * 2),
        name="moe_gather",
    )(row_token, tile_rows, h2)


def _expert_changed(te_ref, t):
    return (t == 0) | (te_ref[t] != te_ref[jnp.maximum(t - 1, 0)])


def _expert_up_kernel(te_ref, tv_ref, x_ref, wg_ref, wu_ref, o_ref, wg_sc, wu_sc):
    t = pl.program_id(1)

    @pl.when(_expert_changed(te_ref, t))
    def _():
        wg_sc[...] = wg_ref[...].astype(BF16)
        wu_sc[...] = wu_ref[...].astype(BF16)

    @pl.when(tv_ref[t] > 0)
    def _():
        x = x_ref[...]
        g = _dot(x, wg_sc[...])
        u = _dot(x, wu_sc[...])
        o_ref[...] = (g * jax.nn.sigmoid(g) * u).astype(o_ref.dtype)

    @pl.when(tv_ref[t] == 0)
    def _():
        o_ref[...] = jnp.zeros(o_ref.shape, o_ref.dtype)


def _expert_down_kernel(te_ref, tv_ref, h_ref, wd_ref, o_ref, wd_sc):
    t = pl.program_id(1)

    @pl.when(_expert_changed(te_ref, t))
    def _():
        wd_sc[...] = wd_ref[...].astype(BF16)

    @pl.when(tv_ref[t] > 0)
    def _():
        o_ref[...] = _dot(h_ref[...], wd_sc[...])

    @pl.when(tv_ref[t] == 0)
    def _():
        o_ref[...] = jnp.zeros(o_ref.shape, o_ref.dtype)


def _expert_mlps(tile_expert, tile_valid, xs, layer, w_gate, w_up, w_down, tm):
    n_rows, d = xs.shape
    f = w_gate.shape[3]
    tf = min(512, f)
    td = min(2048, d)
    nt = n_rows // tm
    hidden = pl.pallas_call(
        _expert_up_kernel,
        grid_spec=pltpu.PrefetchScalarGridSpec(
            num_scalar_prefetch=2,
            grid=(f // tf, nt),
            in_specs=[pl.BlockSpec((tm, d), lambda j, t, te, tv: (t, 0)),
                      pl.BlockSpec((None, None, d, tf), lambda j, t, te, tv: (layer, te[t], 0, j)),
                      pl.BlockSpec((None, None, d, tf), lambda j, t, te, tv: (layer, te[t], 0, j))],
            out_specs=pl.BlockSpec((tm, tf), lambda j, t, te, tv: (t, j)),
            scratch_shapes=[pltpu.VMEM((d, tf), BF16), pltpu.VMEM((d, tf), BF16)]),
        out_shape=jax.ShapeDtypeStruct((n_rows, f), BF16),
        compiler_params=_cparams(("arbitrary", "arbitrary"),
                                 2 * (tm * d * 2 + 2 * d * tf * 4 + tm * tf * 2) + 2 * d * tf * 2 + 4 * tm * tf * 4),
        name="moe_gate_up",
    )(tile_expert, tile_valid, xs, w_gate, w_up)
    return pl.pallas_call(
        _expert_down_kernel,
        grid_spec=pltpu.PrefetchScalarGridSpec(
            num_scalar_prefetch=2,
            grid=(d // td, nt),
            in_specs=[pl.BlockSpec((tm, f), lambda j, t, te, tv: (t, 0)),
                      pl.BlockSpec((None, None, f, td), lambda j, t, te, tv: (layer, te[t], 0, j))],
            out_specs=pl.BlockSpec((tm, td), lambda j, t, te, tv: (t, j)),
            scratch_shapes=[pltpu.VMEM((f, td), BF16)]),
        out_shape=jax.ShapeDtypeStruct((n_rows, d), F32),
        compiler_params=_cparams(("arbitrary", "arbitrary"),
                                 2 * (tm * f * 2 + f * td * 4 + tm * td * 4) + f * td * 2 + tm * td * 4),
        name="moe_down",
    )(tile_expert, tile_valid, hidden, w_down)


def _combine_kernel(pos_ref, y_ref, x_ref, g_ref, route_ref, o_ref, buf0, buf1, sem,
                    *, tc, tiles_per_batch):
    i = pl.program_id(0)
    base = i * tc

    def issue(r, _):
        _row_copy(y_ref, pos_ref[2 * (base + r)], buf0, r, sem).start(priority=0)
        _row_copy(y_ref, pos_ref[2 * (base + r) + 1], buf1, r, sem).start(priority=1)
        return 0

    lax.fori_loop(0, tc, issue, 0)

    def drain(r, _):
        _row_copy(y_ref, 0, buf0, r, sem).wait()
        _row_copy(y_ref, 0, buf1, r, sem).wait()
        return 0

    lax.fori_loop(0, tc, drain, 0)
    route = route_ref[...]
    moe = route[:, 2:3] * buf0[...] + route[:, 3:4] * buf1[...]
    o_ref[...] = x_ref[...] + g_ref[pl.ds(i // tiles_per_batch, 1), :] * moe


def _combine(pos, y, x2, mod_l, g_chunk, route, seq):
    t, d = x2.shape
    tc = 128
    return pl.pallas_call(
        functools.partial(_combine_kernel, tc=tc, tiles_per_batch=seq // tc),
        grid_spec=pltpu.PrefetchScalarGridSpec(
            num_scalar_prefetch=1,
            grid=(t // tc,),
            in_specs=[pl.BlockSpec(memory_space=pl.ANY),
                      pl.BlockSpec((tc, d), lambda i, p: (i, 0)),
                      pl.BlockSpec((SUBLANES, d), lambda i, p: (0, g_chunk)),
                      pl.BlockSpec((tc, LANES), lambda i, p: (i, 0))],
            out_specs=pl.BlockSpec((tc, d), lambda i, p: (i, 0)),
            scratch_shapes=[pltpu.VMEM((tc, d), F32), pltpu.VMEM((tc, d), F32),
                            pltpu.SemaphoreType.DMA(())]),
        out_shape=jax.ShapeDtypeStruct((t, d), F32),
        compiler_params=_cparams(("arbitrary",), 2 * tc * d * 4 + 4 * tc * d * 4 + 4 * tc * d * 4),
        name="moe_combine",
    )(pos, y, x2, mod_l, route)


def _routing_tables(route, tm):
    t = route.shape[0]
    experts = route[:, :2].astype(jnp.int32).reshape(-1)
    onehot = (experts[:, None] == jnp.arange(N_EXPERTS)[None, :]).astype(jnp.int32)
    csum = jnp.cumsum(onehot, axis=0)
    rank = jnp.sum((csum - onehot) * onehot, axis=1)
    counts = csum[-1]
    padded = ((counts + tm - 1) // tm) * tm
    ends = jnp.cumsum(padded)
    starts = ends - padded
    dest = starts[experts] + rank
    n_rows = 2 * t + N_EXPERTS * tm
    row_token = jnp.zeros((n_rows,), jnp.int32).at[dest].set(jnp.arange(2 * t, dtype=jnp.int32) // 2)
    tile_start = jnp.arange(n_rows // tm, dtype=jnp.int32) * tm
    tile_expert = jnp.minimum(jnp.sum((tile_start[:, None] >= ends[None, :]).astype(jnp.int32), axis=1),
                              N_EXPERTS - 1)
    tile_valid = (tile_start < ends[-1]).astype(jnp.int32)
    real_end = (starts + counts)[tile_expert]
    tile_rows = jnp.clip(real_end - tile_start, 0, tm).astype(jnp.int32) * tile_valid
    return row_token, dest.astype(jnp.int32), tile_expert, tile_valid, tile_rows, n_rows


def _tiled(vec, reps):
    return jnp.tile(vec.astype(F32), reps)


def _mixer(h, seq, batch, layer, w_in_t, w_branch, w_out, rel_bias, cmp_pos, cmp_w1, cmp_w2,
           nsa_q_g, nsa_k_g, diff_q_g, diff_k_g, diff_lam, diff_out_g, lam_init, x2, mod_l, d):
    scale = HEAD_DIM ** -0.5
    ones = lambda n: jnp.ones((n,), F32)
    zeros = lambda n: jnp.zeros((n,), F32)
    proj = functools.partial(_matmul_w32, h, w_in_t, (layer,), tm=1024, tn=512, transposed=True)

    cs = jnp.concatenate([jnp.full((BRANCH_W,), scale, F32), ones(2 * BRANCH_W)]).reshape(1, -1)
    p_sb = proj(_C_SB, 3 * BRANCH_W, [(cs, "col")], _ep_scale, BF16, name="proj_sb")
    nf = ones(BRANCH_W).reshape(1, -1)
    cs = (_tiled(nsa_q_g, N_HEADS) * scale).reshape(1, -1)
    p_nsq = proj(_C_NSQ, BRANCH_W, [(nf, "col"), (cs, "col")],
                 functools.partial(_ep_norm, width=HEAD_DIM), BF16, name="proj_nsq")
    nf = jnp.concatenate([zeros(2 * KV_W), ones(KV_W), zeros(KV_W), ones(KV_W), zeros(KV_W)]).reshape(1, -1)
    cs = jnp.concatenate([ones(2 * KV_W), _tiled(nsa_k_g[1], KV_HEADS), ones(KV_W),
                          _tiled(nsa_k_g[2], KV_HEADS), ones(KV_W)]).reshape(1, -1)
    p_nskv = proj(_C_NSKV, 6 * KV_W, [(nf, "col"), (cs, "col")],
                  functools.partial(_ep_norm, width=HEAD_DIM), BF16, name="proj_nskv")
    ns_gates = proj(_C_GNS, LANES, [], _ep_sigmoid, F32, name="proj_nsgate")
    nf = jnp.concatenate([ones(2 * BRANCH_W), zeros(BRANCH_W)]).reshape(1, -1)
    cs = jnp.concatenate([_tiled(diff_q_g, 2 * N_HEADS) * DIFF_QK ** -0.5, _tiled(diff_k_g, 2 * N_HEADS),
                          ones(BRANCH_W)]).reshape(1, -1)
    p_df = proj(_C_DF, 3 * BRANCH_W, [(nf, "col"), (cs, "col")],
                functools.partial(_ep_norm, width=DIFF_QK), BF16, name="proj_diff")
    m_gates = proj(_C_MERGE, 3 * d, [], _ep_sigmoid, BF16, name="proj_merge_gate")

    o_sb = _sb_attention(p_sb, batch, seq)
    kvcm = _compress(p_nskv, cmp_pos, cmp_w1, cmp_w2, nsa_k_g[0], batch, seq)
    o_ns = _nsa_attention(p_nsq, kvcm, p_nskv, ns_gates, rel_bias[:, :N_HEADS], nsa_q_g, nsa_k_g, batch, seq)
    o_df = _diff_attention(p_df, rel_bias[:, N_HEADS:], diff_lam, diff_out_g, diff_q_g, diff_k_g, lam_init,
                           batch, seq)
    merged = _branch_merge(o_sb, o_ns, o_df, w_branch, layer, m_gates)
    g1 = lax.slice_in_dim(mod_l, 2 * d, 3 * d, axis=1)
    return _matmul_w32(merged, w_out, (layer,), 0, d, [(x2, "tile"), (g1, "batch")], _ep_residual, F32,
                       tm=1024, tn=512, name="proj_out", seq=seq)


def _moe(x2, seq, mod_l, norm_gain, w_router, router_bias, layer, w_gate, w_up, w_down):
    tm = 256
    h2, route = _norm_router(x2, norm_gain, mod_l, 3, 4, seq, w_router, router_bias)
    row_token, pos, tile_expert, tile_valid, tile_rows, n_rows = _routing_tables(route, tm)
    xs = _gather_rows(row_token, tile_rows, h2, n_rows, tm)
    y = _expert_mlps(tile_expert, tile_valid, xs, layer, w_gate, w_up, w_down, tm)
    return _combine(pos, y, x2, mod_l, 5, route, seq)


def kernel(x, c, rel_bias, w_router, router_bias, w_ada, b_ada, norm_mix, norm_ffn, w_in, nsa_cmp_pos,
           nsa_cmp_w1, nsa_cmp_w2, nsa_q_norm, nsa_k_norm, diff_q_norm, diff_k_norm, diff_lambda,
           diff_out_norm, w_branch, w_out, w_exp_gate, w_exp_up, w_exp_down):
    batch, seq, d = x.shape
    depth = w_ada.shape[0]
    mod = _adaln(c, w_ada, b_ada)
    x2 = x.reshape(batch * seq, d)
    w_in_t = jnp.swapaxes(w_in, 1, 2)
    for layer in range(depth):
        lam_init = 0.8 - 0.6 * math.exp(-0.3 * layer)
        mod_l = mod[layer]
        h = _norm_mod(x2, norm_mix[layer], mod_l, 0, 1, seq)
        x2 = _mixer(h, seq, batch, layer, w_in_t, w_branch, w_out, rel_bias,
                    nsa_cmp_pos[layer], nsa_cmp_w1[layer], nsa_cmp_w2[layer], nsa_q_norm[layer],
                    nsa_k_norm[layer], diff_q_norm[layer], diff_k_norm[layer], diff_lambda[layer],
                    diff_out_norm[layer], lam_init, x2, mod_l, d)
        x2 = _moe(x2, seq, mod_l, norm_ffn[layer], w_router, router_bias, layer, w_exp_gate,
                  w_exp_up, w_exp_down)
    return x2.reshape(batch, seq, d)
```

```python
import functools
import math

import numpy as np
import jax
import jax.numpy as jnp
from jax import lax
from jax.experimental import pallas as pl
from jax.experimental.pallas import tpu as pltpu

F32 = jnp.float32
BF16 = jnp.bfloat16

HEAD_DIM = 128
N_HEADS = 8
KV_HEADS = 2
Q_PER_KV = N_HEADS // KV_HEADS
DIFF_QK = HEAD_DIM // 2
BRANCH_W = N_HEADS * HEAD_DIM
KV_W = KV_HEADS * HEAD_DIM
CMP_LEN = 32
CMP_STRIDE = 16
SLC_LEN = 64
N_SLC = 16
N_LOCAL_SLC = 2
WINDOW = 512
REL_BUCKETS = 32
REL_MAX_DIST = 128
N_EXPERTS = 16
N_GROUPS = 4
EXP_PER_GROUP = N_EXPERTS // N_GROUPS
NORM_EPS = 1e-6
NEG_INF = -1e30
MASKED_BELOW = -1e29
FORCE_SCORE = 1e9
LANES = 128
SUBLANES = 8
VMEM_CAP = 56 * 1024 * 1024
SB_EXIT = 100.0

_C_SB = 0
_C_NSQ = 3 * BRANCH_W
_C_NSKV = _C_NSQ + BRANCH_W
_C_GNS = _C_NSKV + 6 * KV_W
_C_DF = _C_GNS + 3 * N_HEADS
_C_MERGE = _C_DF + 3 * BRANCH_W


def _cparams(sem, vmem_bytes):
    lim = int(min(VMEM_CAP, max(vmem_bytes * 5 // 4 + (2 << 20), 16 << 20)))
    return pltpu.CompilerParams(dimension_semantics=sem, vmem_limit_bytes=lim)


def _dot(a, b):
    return jnp.dot(a, b, preferred_element_type=F32)


def _dot_nt(a, b):
    return lax.dot_general(a, b, (((1,), (1,)), ((), ())), preferred_element_type=F32)


def _split_bf16(x):
    hi = x.astype(BF16)
    lo = (x - hi.astype(F32)).astype(BF16)
    return hi, lo


def _ada_kernel(cb_ref, w_ref, b_ref, o_ref, ca_sc, *, nb, rows_per_iter):
    k_dim, tn = w_ref.shape

    @pl.when((pl.program_id(0) == 0) & (pl.program_id(1) == 0))
    def _():
        cb = cb_ref[...]
        ca_sc[...] = cb * jax.nn.sigmoid(cb)

    reps = tn // LANES
    groups = rows_per_iter // SUBLANES

    def body(i, accs):
        r0 = pl.multiple_of(i * rows_per_iter, rows_per_iter)
        w3 = w_ref[pl.ds(r0, rows_per_iter), :].reshape(groups, SUBLANES, tn)
        new = []
        for b in range(nb):
            ca = ca_sc[b, pl.ds(r0, rows_per_iter), :].reshape(groups, SUBLANES, LANES)
            ca = jnp.concatenate([ca] * reps, axis=2)
            new.append(accs[b] + jnp.sum(w3 * ca, axis=0))
        return tuple(new)

    accs = lax.fori_loop(0, k_dim // rows_per_iter, body,
                         tuple(jnp.zeros((SUBLANES, tn), F32) for _ in range(nb)))
    rows = [jnp.sum(a, axis=0, keepdims=True) + b_ref[...] for a in accs]
    rows.append(jnp.zeros((SUBLANES - nb, tn), F32))
    o_ref[...] = jnp.concatenate(rows, axis=0)


def _adaln(c, w_ada, b_ada):
    depth, d, n = w_ada.shape
    nb = c.shape[0]
    tn = 512
    cb = jnp.broadcast_to(c[:, :, None], (nb, d, LANES))
    vmem = 2 * d * tn * 4 + 3 * nb * d * LANES * 4
    return pl.pallas_call(
        functools.partial(_ada_kernel, nb=nb, rows_per_iter=64),
        grid=(depth, n // tn),
        in_specs=[pl.BlockSpec((nb, d, LANES), lambda l, j: (0, 0, 0)),
                  pl.BlockSpec((None, d, tn), lambda l, j: (l, 0, j)),
                  pl.BlockSpec((None, 1, tn), lambda l, j: (l, 0, j))],
        out_specs=pl.BlockSpec((None, SUBLANES, tn), lambda l, j: (l, 0, j)),
        out_shape=jax.ShapeDtypeStruct((depth, SUBLANES, n), F32),
        scratch_shapes=[pltpu.VMEM((nb, d, LANES), F32)],
        compiler_params=_cparams(("arbitrary", "arbitrary"), vmem),
        name="adaln_mod",
    )(cb, w_ada, b_ada.reshape(depth, 1, n))


def _modulated_norm(x_ref, gain_ref, sc_ref, sh_ref, tiles_per_batch):
    b = pl.program_id(0) // tiles_per_batch
    x = x_ref[...]
    ms = jnp.mean(x * x, axis=-1, keepdims=True)
    y = x * lax.rsqrt(ms + NORM_EPS) * gain_ref[...]
    return y * (1.0 + sc_ref[pl.ds(b, 1), :]) + sh_ref[pl.ds(b, 1), :]


def _norm_kernel(x_ref, gain_ref, sc_ref, sh_ref, o_ref, *, tiles_per_batch):
    o_ref[...] = _modulated_norm(x_ref, gain_ref, sc_ref, sh_ref, tiles_per_batch).astype(o_ref.dtype)


def _norm_mod(x2, gain, mod_l, sh_chunk, sc_chunk, seq):
    t, d = x2.shape
    tm = 256
    return pl.pallas_call(
        functools.partial(_norm_kernel, tiles_per_batch=seq // tm),
        grid=(t // tm,),
        in_specs=[pl.BlockSpec((tm, d), lambda i: (i, 0)),
                  pl.BlockSpec((1, d), lambda i: (0, 0)),
                  pl.BlockSpec((SUBLANES, d), lambda i: (0, sc_chunk)),
                  pl.BlockSpec((SUBLANES, d), lambda i: (0, sh_chunk))],
        out_specs=pl.BlockSpec((tm, d), lambda i: (i, 0)),
        out_shape=jax.ShapeDtypeStruct((t, d), BF16),
        compiler_params=_cparams(("arbitrary",), 2 * tm * d * 6 + 4 * SUBLANES * d * 4),
        name="norm_mix",
    )(x2, gain.reshape(1, d), mod_l, mod_l)


def _norm_router_kernel(x_ref, gain_ref, sc_ref, sh_ref, wr_ref, rb_ref, h_ref, route_ref,
                        *, tiles_per_batch):
    h = _modulated_norm(x_ref, gain_ref, sc_ref, sh_ref, tiles_per_batch)
    h_hi, h_lo = _split_bf16(h)
    bits = pltpu.bitcast(h_hi.astype(F32), jnp.uint32)
    half = bits.shape[1] // 2
    h_ref[...] = (bits[:, half:] & jnp.uint32(0xFFFF0000)) | lax.shift_right_logical(bits[:, :half],
                                                                                     jnp.uint32(16))
    w_hi, w_lo = _split_bf16(wr_ref[...])
    logits = _dot(h_hi, w_hi) + _dot(h_hi, w_lo) + _dot(h_lo, w_hi)
    aff = jax.nn.sigmoid(logits)
    biased = aff + rb_ref[...]
    lane_i = lax.broadcasted_iota(jnp.int32, aff.shape, 1)
    lane = lane_i.astype(F32)
    low = jnp.float32(-3e38)
    big = jnp.float32(1 << 20)
    best = None
    for gi in range(N_GROUPS):
        in_group = (lane_i // EXP_PER_GROUP) == gi
        v = jnp.where(in_group, biased, low)
        m1 = jnp.max(v, axis=-1, keepdims=True)
        i1 = jnp.min(jnp.where(v == m1, lane, big), axis=-1, keepdims=True)
        v2 = jnp.where(lane == i1, low, v)
        m2 = jnp.max(v2, axis=-1, keepdims=True)
        i2 = jnp.min(jnp.where(v2 == m2, lane, big), axis=-1, keepdims=True)
        score = m1 + m2
        if best is None:
            best = (score, i1, i2)
        else:
            better = score > best[0]
            best = (jnp.where(better, score, best[0]), jnp.where(better, i1, best[1]),
                    jnp.where(better, i2, best[2]))
    _, e1, e2 = best
    a1 = jnp.sum(jnp.where(lane == e1, aff, 0.0), axis=-1, keepdims=True)
    a2 = jnp.sum(jnp.where(lane == e2, aff, 0.0), axis=-1, keepdims=True)
    den = a1 + a2
    out = jnp.where(lane_i == 0, e1,
                    jnp.where(lane_i == 1, e2,
                              jnp.where(lane_i == 2, a1 / den, jnp.where(lane_i == 3, a2 / den, 0.0))))
    route_ref[...] = out


def _norm_router(x2, gain, mod_l, sh_chunk, sc_chunk, seq, w_router, router_bias):
    t, d = x2.shape
    tm = 256
    wr = jnp.pad(w_router, ((0, 0), (0, LANES - N_EXPERTS)))
    rb = jnp.pad(router_bias, (0, LANES - N_EXPERTS)).reshape(1, LANES)
    return pl.pallas_call(
        functools.partial(_norm_router_kernel, tiles_per_batch=seq // tm),
        grid=(t // tm,),
        in_specs=[pl.BlockSpec((tm, d), lambda i: (i, 0)),
                  pl.BlockSpec((1, d), lambda i: (0, 0)),
                  pl.BlockSpec((SUBLANES, d), lambda i: (0, sc_chunk)),
                  pl.BlockSpec((SUBLANES, d), lambda i: (0, sh_chunk)),
                  pl.BlockSpec((d, LANES), lambda i: (0, 0)),
                  pl.BlockSpec((1, LANES), lambda i: (0, 0))],
        out_specs=[pl.BlockSpec((tm, d // 2), lambda i: (i, 0)),
                   pl.BlockSpec((tm, LANES), lambda i: (i, 0))],
        out_shape=[jax.ShapeDtypeStruct((t, d // 2), jnp.uint32),
                   jax.ShapeDtypeStruct((t, LANES), F32)],
        compiler_params=_cparams(("arbitrary",), 2 * tm * d * 8 + 4 * d * LANES * 4 + 8 * tm * d),
        name="norm_ffn_router",
    )(x2, gain.reshape(1, d), mod_l, mod_l, wr, rb)


def _mm_w32_kernel(a_ref, w_ref, *rest, epilogue, n_extra, tiles_per_batch, transposed):
    extras = rest[:n_extra]
    o_ref, w_sc = rest[n_extra], rest[n_extra + 1]

    @pl.when(pl.program_id(1) == 0)
    def _():
        w_sc[...] = w_ref[...].reshape(w_sc.shape).astype(BF16)

    acc = _dot_nt(a_ref[...], w_sc[...]) if transposed else _dot(a_ref[...], w_sc[...])
    o_ref[...] = epilogue(acc, extras, pl.program_id(1) // tiles_per_batch).astype(o_ref.dtype)


def _matmul_w32(a, w, lead, col_lo, n, extras, epilogue, out_dtype, *, tm, tn, name, seq=None,
                transposed=False):
    m, k = a.shape
    tm = min(tm, m)
    tn = min(tn, n)
    assert m % tm == 0 and n % tn == 0
    none = (None,) * len(lead)
    if transposed:
        assert col_lo % SUBLANES == 0
        w_spec = pl.BlockSpec(tuple(pl.Element(1) for _ in lead) + (pl.Element(tn), pl.Element(k)),
                              lambda j, i: lead + (pl.multiple_of(col_lo + j * tn, SUBLANES), 0))
    else:
        assert col_lo % tn == 0
        w_spec = pl.BlockSpec(none + (k, tn), lambda j, i: lead + (0, col_lo // tn + j))
    in_specs = [pl.BlockSpec((tm, k), lambda j, i: (i, 0)), w_spec]
    args = [a, w]
    vmem = 2 * tm * k * 2 + 2 * k * tn * 4 + k * tn * 2 + 2 * tm * tn * jnp.dtype(out_dtype).itemsize + 2 * tm * tn * 4
    for arr, kind in extras:
        if kind == "tile":
            in_specs.append(pl.BlockSpec((tm, tn), lambda j, i: (i, j)))
            vmem += 2 * tm * tn * arr.dtype.itemsize
        else:
            rows = 1 if kind == "col" else SUBLANES
            in_specs.append(pl.BlockSpec((rows, tn), lambda j, i: (0, j)))
            vmem += 2 * SUBLANES * tn * 4
        args.append(arr)
    tiles_per_batch = (seq // tm) if seq else 1
    return pl.pallas_call(
        functools.partial(_mm_w32_kernel, epilogue=epilogue, n_extra=len(extras),
                          tiles_per_batch=tiles_per_batch, transposed=transposed),
        grid=(n // tn, m // tm),
        in_specs=in_specs,
        out_specs=pl.BlockSpec((tm, tn), lambda j, i: (i, j)),
        out_shape=jax.ShapeDtypeStruct((m, n), out_dtype),
        scratch_shapes=[pltpu.VMEM((tn, k) if transposed else (k, tn), BF16)],
        compiler_params=_cparams(("arbitrary", "arbitrary"), vmem),
        name=name,
    )(*args)


def _ep_scale(acc, extras, b):
    return acc * extras[0][...]


def _ep_sigmoid(acc, extras, b):
    return jax.nn.sigmoid(acc)


def _ep_norm(acc, extras, b, *, width):
    nf_ref, cs_ref = extras
    outs = []
    for c in range(acc.shape[1] // LANES):
        sl = slice(c * LANES, (c + 1) * LANES)
        blk = acc[:, sl]
        sq = blk * blk
        if width == LANES:
            ms = jnp.mean(sq, axis=-1, keepdims=True)
        else:
            lane = lax.broadcasted_iota(jnp.int32, blk.shape, 1)
            first = lane < width
            lo = jnp.sum(jnp.where(first, sq, 0.0), axis=-1, keepdims=True) / width
            hi = jnp.sum(jnp.where(first, 0.0, sq), axis=-1, keepdims=True) / width
            ms = jnp.where(first, lo, hi)
        inv = lax.rsqrt(ms + NORM_EPS)
        f = jnp.where(nf_ref[:, sl] > 0.0, inv, 1.0)
        outs.append(blk * f * cs_ref[:, sl])
    return jnp.concatenate(outs, axis=1)


def _ep_residual(acc, extras, b):
    x_ref, g_ref = extras
    return x_ref[...] + g_ref[pl.ds(b, 1), :] * acc


def _bucket_np(dist):
    n = np.maximum(dist, 0)
    max_exact = REL_BUCKETS // 2
    nf = np.maximum(n, max_exact).astype(np.float32)
    large = max_exact + (np.log(nf / np.float32(max_exact))
                         / np.float32(math.log(REL_MAX_DIST / max_exact))
                         * np.float32(REL_BUCKETS - max_exact)).astype(np.int32)
    large = np.minimum(large, REL_BUCKETS - 1)
    return np.where(n < max_exact, n, large).astype(np.int32)


def _bucket_starts():
    b = _bucket_np(np.arange(4 * REL_MAX_DIST))
    return [int(np.argmax(b >= k)) for k in range(1, REL_BUCKETS)]


def _delta_bias(table, dist, valid):
    tb = table.T.astype(F32)
    shape = (tb.shape[0],) + (1,) * dist.ndim
    val = jnp.broadcast_to(tb[:, 0].reshape(shape), (tb.shape[0],) + dist.shape)
    for k, start in enumerate(_bucket_starts(), start=1):
        val = jnp.where(dist[None] >= start, tb[:, k].reshape(shape), val)
    val = val - tb[:, REL_BUCKETS - 1].reshape(shape)
    return jnp.where(valid[None], val, NEG_INF)


def _toeplitz_tiles(table, offs, nr, nc, lo, hi):
    dmin = -(nc - 1)
    d = jnp.arange(dmin, max(offs) + nr, dtype=jnp.int32)
    g = _delta_bias(table, d, (d >= lo) & (d < hi))
    nh = g.shape[0]
    w = nr + nc - 1
    tiles = []
    for off in offs:
        base = off - dmin
        ahead = g[:, base - nc + 1:base + 1][:, ::-1]
        behind = g[:, base + 1:base + nr][:, ::-1]
        ring = jnp.concatenate([ahead, behind], axis=1)
        flat = jnp.tile(ring, (1, nr))[:, :nr * (w - 1)]
        tiles.append(flat.reshape(nh, nr, w - 1)[:, :, :nc])
    return jnp.stack(tiles, axis=1)


def _stack_group_heads(tiles):
    nh, k, nr, nc = tiles.shape
    t = tiles.reshape(KV_HEADS, Q_PER_KV, k, nr, nc)
    return jnp.transpose(t, (0, 2, 1, 3, 4)).reshape(KV_HEADS, k, Q_PER_KV * nr, nc)


SB_HEADS_PER_STEP = 1


def _sb_kernel(q_ref, k_ref, v_ref, o_ref, *, tq):
    qi = pl.program_id(2)
    heads = [slice(h * HEAD_DIM, (h + 1) * HEAD_DIM) for h in range(SB_HEADS_PER_STEP)]
    qs = [q_ref[:, hs] for hs in heads]
    row = lax.broadcasted_iota(jnp.int32, (tq, tq), 0)
    col = lax.broadcasted_iota(jnp.int32, (tq, tq), 1)
    later = jnp.where(row > col, 1.0, 0.0).astype(BF16)
    causal = col < row

    def step(kj, state, diag):
        k0 = pl.multiple_of(kj * tq, tq)
        out = []
        for q, hs, (carry, acc) in zip(qs, heads, state):
            z = _dot_nt(q, k_ref[pl.ds(k0, tq), hs])
            log_succ = jnp.minimum(z, 0.0) - jnp.log1p(jnp.exp(-jnp.abs(z)))
            log_fail = log_succ - z
            if diag:
                log_fail = jnp.where(causal, log_fail, 0.0)
            hi, lo = _split_bf16(log_fail)
            suffix = _dot(hi, later) + _dot(lo, later)
            a = jnp.exp(log_succ + suffix + carry)
            if diag:
                a = jnp.where(causal, a, 0.0)
            acc = acc + _dot(a.astype(BF16), v_ref[pl.ds(k0, tq), hs])
            carry = carry + jnp.sum(log_fail, axis=-1, keepdims=True)
            out.append((carry, acc))
        return tuple(out)

    zero = (jnp.zeros((tq, 1), F32), jnp.zeros((tq, HEAD_DIM), F32))
    state = step(qi, (zero,) * SB_HEADS_PER_STEP, True)

    def cond(loop):
        kj, state = loop
        live = functools.reduce(jnp.maximum, [jnp.max(carry) for carry, _ in state])
        return (kj >= 0) & (live > -SB_EXIT)

    def body(loop):
        kj, state = loop
        return kj - 1, step(kj, state, False)

    _, state = lax.while_loop(cond, body, (qi - 1, state))
    for hs, (_, acc) in zip(heads, state):
        o_ref[:, hs] = acc.astype(o_ref.dtype)


def _sb_attention(p_sb, batch, seq):
    tq = min(256, seq)
    nq = seq // tq
    t = batch * seq
    hp = SB_HEADS_PER_STEP
    w = hp * HEAD_DIM
    groups = N_HEADS // hp
    vmem = 4 * seq * w * 2 + 4 * tq * w * 2 + 16 * hp * tq * tq * 4
    return pl.pallas_call(
        functools.partial(_sb_kernel, tq=tq),
        grid=(batch, groups, nq),
        in_specs=[pl.BlockSpec((tq, w), lambda b, h, i: (b * nq + i, h)),
                  pl.BlockSpec((seq, w), lambda b, h, i: (b, groups + h)),
                  pl.BlockSpec((seq, w), lambda b, h, i: (b, 2 * groups + h))],
        out_specs=pl.BlockSpec((tq, w), lambda b, h, i: (b * nq + i, h)),
        out_shape=jax.ShapeDtypeStruct((t, BRANCH_W), BF16),
        compiler_params=_cparams(("arbitrary",) * 3, vmem),
        name="sb_attention",
    )(p_sb, p_sb, p_sb)


LOGIT_CAP = 40.0


def _with_ones(v):
    return jnp.concatenate([v, jnp.ones_like(v)], axis=1)


def _softmax_first(s, v, m_sc, acc_sc, bounded):
    if bounded:
        acc_sc[...] = _dot(jnp.exp(s).astype(BF16), _with_ones(v))
        return
    m = jnp.max(s, axis=-1, keepdims=True)
    m_sc[...] = m
    acc_sc[...] = _dot(jnp.exp(s - m).astype(BF16), _with_ones(v))


def _softmax_next(s, v, m_sc, acc_sc, bounded):
    if bounded:
        acc_sc[...] = acc_sc[...] + _dot(jnp.exp(s).astype(BF16), _with_ones(v))
        return
    m_old = m_sc[...]
    m_new = jnp.maximum(m_old, jnp.max(s, axis=-1, keepdims=True))
    p = jnp.exp(s - m_new)
    acc_sc[...] = jnp.exp(m_old - m_new) * acc_sc[...] + _dot(p.astype(BF16), _with_ones(v))
    m_sc[...] = m_new


def _softmax_result(acc_sc, bounded):
    acc = acc_sc[...]
    total = acc[:, HEAD_DIM:]
    den = jnp.where(total > 0.0, total, 1.0) if bounded else jnp.maximum(total, 1.0)
    return acc[:, :HEAD_DIM] / den


def _logit_bound_ok(width, scale, q_gain, k_gains, table):
    qk = width * scale * jnp.max(jnp.abs(q_gain)) * jnp.max(jnp.abs(k_gains)) * 1.02
    delta = jnp.max(jnp.abs(table - table[REL_BUCKETS - 1:]))
    return (qk + delta <= LOGIT_CAP).astype(jnp.int32).reshape(1)


def _compress_kernel(x_ref, p_ref, w1a_ref, w1b_ref, w2_ref, gain_ref, o_ref):
    nc = x_ref.shape[0]
    is_key = pl.program_id(1) < KV_HEADS
    x = x_ref[...].astype(F32)
    first = _dot((x + p_ref[0:1, :]).astype(BF16), w1a_ref[...])
    second = _dot((x + p_ref[1:2, :]).astype(BF16), w1b_ref[...])
    pre = first + pltpu.roll(second, nc - 1, 0)
    hid = pre * jax.nn.sigmoid(pre)
    out = _dot(hid.astype(BF16), w2_ref[...])
    ms = jnp.mean(out * out, axis=-1, keepdims=True)
    normed = out * lax.rsqrt(ms + NORM_EPS) * gain_ref[...]
    o_ref[...] = jnp.where(is_key, normed, out).astype(o_ref.dtype)


def _compress(p_nskv, cmp_pos, cmp_w1, cmp_w2, k_gain0, batch, seq):
    nc = seq // CMP_STRIDE
    half = CMP_STRIDE * HEAD_DIM
    x = p_nskv[:, :2 * KV_W].reshape(batch, nc, CMP_STRIDE, 2 * KV_HEADS, HEAD_DIM)
    x = jnp.transpose(x, (0, 3, 1, 2, 4)).reshape(batch, 2 * KV_HEADS, nc, half)
    pos = cmp_pos.reshape(2, 2, half)
    w1 = cmp_w1.astype(BF16).reshape(2, 2, half, HEAD_DIM)
    w2 = cmp_w2.astype(BF16)
    return pl.pallas_call(
        _compress_kernel,
        grid=(batch, 2 * KV_HEADS),
        in_specs=[pl.BlockSpec((None, None, nc, half), lambda b, j: (b, j, 0, 0)),
                  pl.BlockSpec((None, 2, half), lambda b, j: (j // KV_HEADS, 0, 0)),
                  pl.BlockSpec((None, None, half, HEAD_DIM), lambda b, j: (j // KV_HEADS, 0, 0, 0)),
                  pl.BlockSpec((None, None, half, HEAD_DIM), lambda b, j: (j // KV_HEADS, 1, 0, 0)),
                  pl.BlockSpec((None, HEAD_DIM, HEAD_DIM), lambda b, j: (j // KV_HEADS, 0, 0)),
                  pl.BlockSpec((1, HEAD_DIM), lambda b, j: (0, 0))],
        out_specs=pl.BlockSpec((None, None, nc, HEAD_DIM), lambda b, j: (b, j, 0, 0)),
        out_shape=jax.ShapeDtypeStruct((batch, 2 * KV_HEADS, nc, HEAD_DIM), BF16),
        compiler_params=_cparams(("arbitrary", "arbitrary"), 8 * nc * half * 2 + 8 * half * HEAD_DIM * 2),
        name="nsa_compress",
    )(x, pos, w1, w1, w2, k_gain0.reshape(1, HEAD_DIM))


def _nsa_kernel(ok_ref, q_ref, kcm_ref, vcm_ref, ks_ref, vs_ref, kw_ref, vw_ref, gate_ref,
                pt_ref, bt_ref, wt_ref, o_ref, m_sc, acc_sc, *, tq, tk, tw, n_blk, k_sel):
    qi = pl.program_id(2)
    q0 = qi * tq
    rows = Q_PER_KV * tq
    nc = kcm_ref.shape[0]
    q4 = jnp.concatenate([q_ref[:, r * HEAD_DIM:(r + 1) * HEAD_DIM] for r in range(Q_PER_KV)], axis=0)

    per_tile = tq // CMP_STRIDE
    lead = 2 * per_tile
    shift = lax.rem(qi * per_tile + (nc - lead), nc)
    cb = pltpu.roll(pt_ref[...], shift, 1)
    rel = lax.broadcasted_iota(jnp.int32, (rows, nc), 1) - qi * per_tile
    cb = jnp.where(rel < -lead, 0.0, jnp.where(rel >= per_tile, NEG_INF, cb))
    s = _dot_nt(q4, kcm_ref[...]) + cb
    valid = s > MASKED_BELOW
    s = jnp.where(valid, s, NEG_INF)
    p = jnp.where(valid, jnp.exp(s - jnp.max(s, axis=-1, keepdims=True)), 0.0)
    p = p / jnp.maximum(jnp.sum(p, axis=-1, keepdims=True), 1.0)
    o_cmp = _dot(p.astype(BF16), vcm_ref[...])

    psum = p[0:tq]
    for r in range(1, Q_PER_KV):
        psum = psum + p[r * tq:(r + 1) * tq]
    n_id = lax.broadcasted_iota(jnp.int32, (nc, LANES), 0)
    j_id = lax.broadcasted_iota(jnp.int32, (nc, LANES), 1)
    overlap = ((CMP_STRIDE * n_id < SLC_LEN * j_id + SLC_LEN)
               & (CMP_STRIDE * n_id + CMP_LEN > SLC_LEN * j_id) & (j_id < n_blk))
    overlap = jnp.where(overlap, 1.0, 0.0).astype(BF16)
    p_hi, p_lo = _split_bf16(psum)
    imp = _dot(p_hi, overlap) + _dot(p_lo, overlap)
    nr = -(-n_blk // SUBLANES) * SUBLANES
    imp_t = jnp.transpose(imp)[:nr]
    blk = lax.broadcasted_iota(jnp.int32, (nr, tq), 0)
    cur = (q0 + lax.broadcasted_iota(jnp.int32, (nr, tq), 1)) // SLC_LEN
    forced = (blk == 0) | ((blk <= cur) & (blk > cur - N_LOCAL_SLC))
    imp_t = jnp.where(forced, FORCE_SCORE, jnp.where(blk > cur, NEG_INF, imp_t))
    imp_t = jnp.where(blk < n_blk, imp_t, -3e38)
    rank = jnp.zeros((nr, tq), F32)
    for j in range(n_blk):
        cj = imp_t[j:j + 1, :]
        ahead = jnp.where(blk > j, jnp.where(cj >= imp_t, 1.0, 0.0), jnp.where(cj > imp_t, 1.0, 0.0))
        rank = rank + ahead
    sel_t = jnp.where((rank < k_sel) & (blk < n_blk), 1.0, 0.0)
    if nr < LANES:
        sel_t = jnp.concatenate([sel_t, jnp.zeros((LANES - nr, tq), F32)], axis=0)
    sel = jnp.transpose(sel_t).astype(BF16)

    per = tk // tq
    kd = qi // per
    par = qi - kd * per
    e_row = lax.broadcasted_iota(jnp.int32, (LANES, tk), 0)
    e_col = lax.broadcasted_iota(jnp.int32, (LANES, tk), 1)

    def slc_scores(kj, bias):
        k0 = pl.multiple_of(kj * tk, tk)
        expand = jnp.where((k0 + e_col) // SLC_LEN == e_row, 1.0, 0.0).astype(BF16)
        chosen = _dot(sel, expand)
        s = _dot_nt(q4, ks_ref[pl.ds(k0, tk), :])
        if bias is not None:
            s = s + bias
        s = jnp.where(chosen[None] > 0.5, s.reshape(Q_PER_KV, tq, tk), NEG_INF).reshape(rows, tk)
        return s, vs_ref[pl.ds(k0, tk), :]

    def attend(bounded):
        s, v = slc_scores(kd, bt_ref[par])
        _softmax_first(s, v, m_sc, acc_sc, bounded)

        @pl.when(kd >= 1)
        def _():
            s, v = slc_scores(kd - 1, bt_ref[per + par])
            _softmax_next(s, v, m_sc, acc_sc, bounded)

        def slc_far(i, _):
            s, v = slc_scores(kd - 2 - i, None)
            _softmax_next(s, v, m_sc, acc_sc, bounded)
            return 0

        lax.fori_loop(0, jnp.maximum(kd - 1, 0), slc_far, 0)
        o_slc = _softmax_result(acc_sc, bounded)

        start = pl.multiple_of(jnp.maximum(q0 - WINDOW, 0), tq)
        s = _dot_nt(q4, kw_ref[pl.ds(start, tw), :]) + wt_ref[jnp.minimum(qi, WINDOW // tq)]
        _softmax_first(s, vw_ref[pl.ds(start, tw), :], m_sc, acc_sc, bounded)
        o_win = _softmax_result(acc_sc, bounded)

        gate = gate_ref[...]
        first_group = pl.program_id(1) == 0

        def gate_col(branch, r):
            lo = branch * N_HEADS + r
            hi = lo + Q_PER_KV
            return jnp.where(first_group, gate[:, lo:lo + 1], gate[:, hi:hi + 1])

        for r in range(Q_PER_KV):
            sl = slice(r * tq, (r + 1) * tq)
            o = gate_col(0, r) * o_cmp[sl] + gate_col(1, r) * o_slc[sl] + gate_col(2, r) * o_win[sl]
            o_ref[:, r * HEAD_DIM:(r + 1) * HEAD_DIM] = o.astype(o_ref.dtype)

    @pl.when(ok_ref[0] > 0)
    def _():
        attend(True)

    @pl.when(ok_ref[0] == 0)
    def _():
        attend(False)


def _nsa_attention(p_nsq, kvcm, p_nskv, gates, rel_nsa, q_gain, k_gains, batch, seq):
    tq = 128
    tk = min(512, seq)
    tw = WINDOW + tq
    nq = seq // tq
    nc = seq // CMP_STRIDE
    n_blk = seq // SLC_LEN
    k_sel = min(N_SLC, n_blk)
    t = batch * seq
    rows = Q_PER_KV * tq
    per = tk // tq
    assert n_blk <= LANES and nc % LANES == 0 and WINDOW % tq == 0 and tw <= seq and tq == LANES

    far = 1 << 30
    offs = [i * tq for i in range(2 * per)]
    bt = _stack_group_heads(_toeplitz_tiles(rel_nsa, offs, tq, tk, 0, far))
    offs = [i * tq for i in range(WINDOW // tq + 1)]
    wt = _stack_group_heads(_toeplitz_tiles(rel_nsa, offs, tq, tw, 0, WINDOW))
    per_tile = tq // CMP_STRIDE
    d = (lax.broadcasted_iota(jnp.int32, (1, tq, nc), 1)
         - CMP_STRIDE * (lax.broadcasted_iota(jnp.int32, (1, tq, nc), 2) - 2 * per_tile) - (CMP_LEN - 1))
    pt = _stack_group_heads(_delta_bias(rel_nsa, d, d >= 0))[:, 0]

    ok = _logit_bound_ok(HEAD_DIM, HEAD_DIM ** -0.5, q_gain, k_gains[1:], rel_nsa)
    const = lambda shape: pl.BlockSpec((None,) + shape, lambda b, g, i, ok: (g,) + (0,) * len(shape),
                                       pipeline_mode=pl.Buffered(1))
    vmem = (8 * seq * HEAD_DIM * 2 + (bt.size + wt.size + pt.size) // KV_HEADS * 4
            + 10 * rows * tw * 4 + 8 * rows * nc * 4)
    return pl.pallas_call(
        functools.partial(_nsa_kernel, tq=tq, tk=tk, tw=tw, n_blk=n_blk, k_sel=k_sel),
        grid_spec=pltpu.PrefetchScalarGridSpec(
            num_scalar_prefetch=1,
            grid=(batch, KV_HEADS, nq),
            in_specs=[pl.BlockSpec((tq, Q_PER_KV * HEAD_DIM), lambda b, g, i, ok: (b * nq + i, g)),
                      pl.BlockSpec((None, None, nc, HEAD_DIM), lambda b, g, i, ok: (b, g, 0, 0)),
                      pl.BlockSpec((None, None, nc, HEAD_DIM), lambda b, g, i, ok: (b, KV_HEADS + g, 0, 0)),
                      pl.BlockSpec((seq, HEAD_DIM), lambda b, g, i, ok: (b, 2 * KV_HEADS + g)),
                      pl.BlockSpec((seq, HEAD_DIM), lambda b, g, i, ok: (b, 3 * KV_HEADS + g)),
                      pl.BlockSpec((seq, HEAD_DIM), lambda b, g, i, ok: (b, 4 * KV_HEADS + g)),
                      pl.BlockSpec((seq, HEAD_DIM), lambda b, g, i, ok: (b, 5 * KV_HEADS + g)),
                      pl.BlockSpec((tq, LANES), lambda b, g, i, ok: (b * nq + i, 0)),
                      const(pt.shape[1:]), const(bt.shape[1:]), const(wt.shape[1:])],
            out_specs=pl.BlockSpec((tq, Q_PER_KV * HEAD_DIM), lambda b, g, i, ok: (b * nq + i, g)),
            scratch_shapes=[pltpu.VMEM((rows, 1), F32), pltpu.VMEM((rows, 2 * HEAD_DIM), F32)]),
        out_shape=jax.ShapeDtypeStruct((t, BRANCH_W), BF16),
        compiler_params=_cparams(("arbitrary",) * 3, vmem),
        name="nsa_attention",
    )(ok, p_nsq, kvcm, kvcm, p_nskv, p_nskv, p_nskv, p_nskv, gates, pt, bt, wt)


def _diff_kernel(ok_ref, q_ref, k_ref, v_ref, dt_ref, lam_ref, og_ref, o_ref, m_sc, acc_sc,
                 *, tq, tk, lam_init):
    qi = pl.program_id(2)
    per = tk // tq
    kd = qi // per
    par = qi - kd * per
    q = q_ref[...]
    lane = lax.broadcasted_iota(jnp.int32, q.shape, 1)
    zero = jnp.zeros_like(q)
    q2 = jnp.concatenate([jnp.where(lane < DIFF_QK, q, zero), jnp.where(lane < DIFF_QK, zero, q)], axis=0)

    def scores(kj, bias):
        k0 = pl.multiple_of(kj * tk, tk)
        s = _dot_nt(q2, k_ref[pl.ds(k0, tk), :])
        if bias is not None:
            s = s + jnp.concatenate([bias, bias], axis=0)
        return s, v_ref[pl.ds(k0, tk), :]

    def attend(bounded):
        s, v = scores(kd, dt_ref[par])
        _softmax_first(s, v, m_sc, acc_sc, bounded)

        @pl.when(kd >= 1)
        def _():
            s, v = scores(kd - 1, dt_ref[per + par])
            _softmax_next(s, v, m_sc, acc_sc, bounded)

        def far(i, _):
            s, v = scores(kd - 2 - i, None)
            _softmax_next(s, v, m_sc, acc_sc, bounded)
            return 0

        lax.fori_loop(0, jnp.maximum(kd - 1, 0), far, 0)
        o2 = _softmax_result(acc_sc, bounded)
        lp = lam_ref[...]
        lam = (jnp.exp(jnp.sum(lp[0:1] * lp[1:2], axis=-1, keepdims=True))
               - jnp.exp(jnp.sum(lp[2:3] * lp[3:4], axis=-1, keepdims=True)) + lam_init)
        o = o2[:tq] - lam * o2[tq:]
        ms = jnp.mean(o * o, axis=-1, keepdims=True)
        o = o * lax.rsqrt(ms + NORM_EPS) * og_ref[...]
        o_ref[...] = (o * (1.0 - lam_init)).astype(o_ref.dtype)

    @pl.when(ok_ref[0] > 0)
    def _():
        attend(True)

    @pl.when(ok_ref[0] == 0)
    def _():
        attend(False)


def _diff_attention(p_df, rel_diff, lam_params, out_gain, q_gain, k_gain, lam_init, batch, seq):
    tq = min(256, seq)
    tk = min(512, seq)
    per = tk // tq
    nq = seq // tq
    t = batch * seq
    dt = _toeplitz_tiles(rel_diff, [i * tq for i in range(2 * per)], tq, tk, 0, 1 << 30)
    ok = _logit_bound_ok(DIFF_QK, DIFF_QK ** -0.5, q_gain, k_gain, rel_diff)
    vmem = 4 * seq * HEAD_DIM * 2 + 2 * 2 * per * 2 * tq * tk * 4 + 12 * 2 * tq * tk * 4
    return pl.pallas_call(
        functools.partial(_diff_kernel, tq=tq, tk=tk, lam_init=lam_init),
        grid_spec=pltpu.PrefetchScalarGridSpec(
            num_scalar_prefetch=1,
            grid=(batch, N_HEADS, nq),
            in_specs=[pl.BlockSpec((tq, HEAD_DIM), lambda b, h, i, ok: (b * nq + i, h)),
                      pl.BlockSpec((seq, HEAD_DIM), lambda b, h, i, ok: (b, N_HEADS + h)),
                      pl.BlockSpec((seq, HEAD_DIM), lambda b, h, i, ok: (b, 2 * N_HEADS + h)),
                      pl.BlockSpec((None, 2 * per, tq, tk), lambda b, h, i, ok: (h, 0, 0, 0)),
                      pl.BlockSpec((4, DIFF_QK), lambda b, h, i, ok: (0, 0)),
                      pl.BlockSpec((1, HEAD_DIM), lambda b, h, i, ok: (0, 0))],
            out_specs=pl.BlockSpec((tq, HEAD_DIM), lambda b, h, i, ok: (b * nq + i, h)),
            scratch_shapes=[pltpu.VMEM((2 * tq, 1), F32), pltpu.VMEM((2 * tq, 2 * HEAD_DIM), F32)]),
        out_shape=jax.ShapeDtypeStruct((t, BRANCH_W), BF16),
        compiler_params=_cparams(("arbitrary",) * 3, vmem),
        name="diff_attention",
    )(ok, p_df, p_df, p_df, dt, lam_params, out_gain.reshape(1, HEAD_DIM))


def _merge_kernel(o0_ref, o1_ref, o2_ref, w_ref, g0_ref, g1_ref, g2_ref, out_ref, w_sc):
    @pl.when(pl.program_id(1) == 0)
    def _():
        w_sc[...] = w_ref[...].astype(BF16)

    acc = g0_ref[...].astype(F32) * _dot(o0_ref[...], w_sc[0])
    acc = acc + g1_ref[...].astype(F32) * _dot(o1_ref[...], w_sc[1])
    acc = acc + g2_ref[...].astype(F32) * _dot(o2_ref[...], w_sc[2])
    out_ref[...] = acc.astype(out_ref.dtype)


def _branch_merge(o_sb, o_ns, o_df, w_branch, layer, gates):
    t = o_sb.shape[0]
    d = w_branch.shape[3]
    tm = min(1024, t)
    tn = min(512, d)
    nj = d // tn
    o_spec = pl.BlockSpec((tm, BRANCH_W), lambda j, i: (i, 0))
    vmem = (2 * (3 * tm * BRANCH_W * 2 + 3 * BRANCH_W * tn * 4 + 3 * tm * tn * 2 + tm * tn * 2)
            + 3 * BRANCH_W * tn * 2 + 4 * tm * tn * 4)
    return pl.pallas_call(
        _merge_kernel,
        grid=(nj, t // tm),
        in_specs=[o_spec, o_spec, o_spec,
                  pl.BlockSpec((None, 3, BRANCH_W, tn), lambda j, i: (layer, 0, 0, j)),
                  pl.BlockSpec((tm, tn), lambda j, i: (i, j)),
                  pl.BlockSpec((tm, tn), lambda j, i: (i, nj + j)),
                  pl.BlockSpec((tm, tn), lambda j, i: (i, 2 * nj + j))],
        out_specs=pl.BlockSpec((tm, tn), lambda j, i: (i, j)),
        out_shape=jax.ShapeDtypeStruct((t, d), BF16),
        scratch_shapes=[pltpu.VMEM((3, BRANCH_W, tn), BF16)],
        compiler_params=_cparams(("arbitrary", "arbitrary"), vmem),
        name="branch_merge",
    )(o_sb, o_ns, o_df, w_branch, gates, gates, gates)


def _row_copy(src_ref, src_row, dst_ref, dst_row, sem):
    return pltpu.make_async_copy(src_ref.at[pl.ds(src_row, 1)], dst_ref.at[pl.ds(dst_row, 1)], sem)


def _gather_kernel(idx_ref, cnt_ref, src_ref, o_ref, buf, sem, *, tg):
    i = pl.program_id(0)
    base = i * tg
    n = cnt_ref[i]

    @pl.when(i == 0)
    def _():
        buf[...] = jnp.zeros(buf.shape, buf.dtype)

    for parity in range(2):
        def issue(r2, _, parity=parity):
            r = 2 * r2 + parity
            _row_copy(src_ref, idx_ref[base + r], buf, r, sem).start(priority=parity)
            return 0

        lax.fori_loop(0, (n + 1 - parity) // 2, issue, 0)

    def drain(r, _):
        _row_copy(src_ref, 0, buf, r, sem).wait()
        return 0

    lax.fori_loop(0, n, drain, 0)
    words = buf[...]
    low = pltpu.bitcast(lax.shift_left(words, jnp.uint32(16)), F32)
    high = pltpu.bitcast(words & jnp.uint32(0xFFFF0000), F32)
    o_ref[...] = jnp.concatenate([low, high], axis=1).astype(o_ref.dtype)


def _gather_rows(row_token, tile_rows, h2_words, n_rows, tg):
    half = h2_words.shape[1]
    return pl.pallas_call(
        functools.partial(_gather_kernel, tg=tg),
        grid_spec=pltpu.PrefetchScalarGridSpec(
            num_scalar_prefetch=2,
            grid=(n_rows // tg,),
            in_specs=[pl.BlockSpec(memory_space=pl.ANY)],
            out_specs=pl.BlockSpec((tg, 2 * half), lambda i, idx, cnt: (i, 0)),
            scratch_shapes=[pltpu.VMEM((tg, half), jnp.uint32), pltpu.SemaphoreType.DMA(())]),
        out_shape=jax.ShapeDtypeStruct((n_rows, 2 * half), BF16),
        compiler_params=_cparams(("arbitrary",), tg * half * 4 + 6 * tg * half * 4),
        name="moe_gather",
    )(row_token, tile_rows, h2_words)


def _expert_changed(te_ref, t):
    return (t == 0) | (te_ref[t] != te_ref[jnp.maximum(t - 1, 0)])


def _expert_up_kernel(te_ref, tv_ref, x_ref, wg_ref, wu_ref, o_ref, wg_sc, wu_sc):
    t = pl.program_id(1)

    @pl.when(_expert_changed(te_ref, t))
    def _():
        wg_sc[...] = wg_ref[...].astype(BF16)
        wu_sc[...] = wu_ref[...].astype(BF16)

    @pl.when(tv_ref[t] > 0)
    def _():
        x = x_ref[...]
        g = _dot(x, wg_sc[...])
        u = _dot(x, wu_sc[...])
        o_ref[...] = (g * jax.nn.sigmoid(g) * u).astype(o_ref.dtype)

    @pl.when(tv_ref[t] == 0)
    def _():
        o_ref[...] = jnp.zeros(o_ref.shape, o_ref.dtype)


def _expert_down_kernel(te_ref, tv_ref, h_ref, wd_ref, o_ref, wd_sc):
    t = pl.program_id(1)

    @pl.when(_expert_changed(te_ref, t))
    def _():
        wd_sc[...] = wd_ref[...].astype(BF16)

    @pl.when(tv_ref[t] > 0)
    def _():
        o_ref[...] = _dot(h_ref[...], wd_sc[...])

    @pl.when(tv_ref[t] == 0)
    def _():
        o_ref[...] = jnp.zeros(o_ref.shape, o_ref.dtype)


def _expert_mlps(tile_expert, tile_valid, xs, layer, w_gate, w_up, w_down, tm):
    n_rows, d = xs.shape
    f = w_gate.shape[3]
    tf = min(512, f)
    td = min(2048, d)
    nt = n_rows // tm
    hidden = pl.pallas_call(
        _expert_up_kernel,
        grid_spec=pltpu.PrefetchScalarGridSpec(
            num_scalar_prefetch=2,
            grid=(f // tf, nt),
            in_specs=[pl.BlockSpec((tm, d), lambda j, t, te, tv: (t, 0)),
                      pl.BlockSpec((None, None, d, tf), lambda j, t, te, tv: (layer, te[t], 0, j)),
                      pl.BlockSpec((None, None, d, tf), lambda j, t, te, tv: (layer, te[t], 0, j))],
            out_specs=pl.BlockSpec((tm, tf), lambda j, t, te, tv: (t, j)),
            scratch_shapes=[pltpu.VMEM((d, tf), BF16), pltpu.VMEM((d, tf), BF16)]),
        out_shape=jax.ShapeDtypeStruct((n_rows, f), BF16),
        compiler_params=_cparams(("arbitrary", "arbitrary"),
                                 2 * (tm * d * 2 + 2 * d * tf * 4 + tm * tf * 2) + 2 * d * tf * 2 + 4 * tm * tf * 4),
        name="moe_gate_up",
    )(tile_expert, tile_valid, xs, w_gate, w_up)
    return pl.pallas_call(
        _expert_down_kernel,
        grid_spec=pltpu.PrefetchScalarGridSpec(
            num_scalar_prefetch=2,
            grid=(d // td, nt),
            in_specs=[pl.BlockSpec((tm, f), lambda j, t, te, tv: (t, 0)),
                      pl.BlockSpec((None, None, f, td), lambda j, t, te, tv: (layer, te[t], 0, j))],
            out_specs=pl.BlockSpec((tm, td), lambda j, t, te, tv: (t, j)),
            scratch_shapes=[pltpu.VMEM((f, td), BF16)]),
        out_shape=jax.ShapeDtypeStruct((n_rows, d), F32),
        compiler_params=_cparams(("arbitrary", "arbitrary"),
                                 2 * (tm * f * 2 + f * td * 4 + tm * td * 4) + f * td * 2 + tm * td * 4),
        name="moe_down",
    )(tile_expert, tile_valid, hidden, w_down)


def _combine_kernel(pos_ref, y_ref, x_ref, g_ref, route_ref, o_ref, buf0, buf1, sem,
                    *, tc, tiles_per_batch):
    i = pl.program_id(0)
    base = i * tc

    def issue(r, _):
        _row_copy(y_ref, pos_ref[2 * (base + r)], buf0, r, sem).start(priority=0)
        _row_copy(y_ref, pos_ref[2 * (base + r) + 1], buf1, r, sem).start(priority=1)
        return 0

    lax.fori_loop(0, tc, issue, 0)

    def drain(r, _):
        _row_copy(y_ref, 0, buf0, r, sem).wait()
        _row_copy(y_ref, 0, buf1, r, sem).wait()
        return 0

    lax.fori_loop(0, tc, drain, 0)
    route = route_ref[...]
    moe = route[:, 2:3] * buf0[...] + route[:, 3:4] * buf1[...]
    o_ref[...] = x_ref[...] + g_ref[pl.ds(i // tiles_per_batch, 1), :] * moe


def _combine(pos, y, x2, mod_l, g_chunk, route, seq):
    t, d = x2.shape
    tc = 128
    return pl.pallas_call(
        functools.partial(_combine_kernel, tc=tc, tiles_per_batch=seq // tc),
        grid_spec=pltpu.PrefetchScalarGridSpec(
            num_scalar_prefetch=1,
            grid=(t // tc,),
            in_specs=[pl.BlockSpec(memory_space=pl.ANY),
                      pl.BlockSpec((tc, d), lambda i, p: (i, 0)),
                      pl.BlockSpec((SUBLANES, d), lambda i, p: (0, g_chunk)),
                      pl.BlockSpec((tc, LANES), lambda i, p: (i, 0))],
            out_specs=pl.BlockSpec((tc, d), lambda i, p: (i, 0)),
            scratch_shapes=[pltpu.VMEM((tc, d), F32), pltpu.VMEM((tc, d), F32),
                            pltpu.SemaphoreType.DMA(())]),
        out_shape=jax.ShapeDtypeStruct((t, d), F32),
        compiler_params=_cparams(("arbitrary",), 2 * tc * d * 4 + 4 * tc * d * 4 + 4 * tc * d * 4),
        name="moe_combine",
    )(pos, y, x2, mod_l, route)


def _routing_tables(route, tm):
    t = route.shape[0]
    experts = route[:, :2].astype(jnp.int32).reshape(-1)
    onehot = (experts[:, None] == jnp.arange(N_EXPERTS)[None, :]).astype(jnp.int32)
    csum = jnp.cumsum(onehot, axis=0)
    rank = jnp.sum((csum - onehot) * onehot, axis=1)
    counts = csum[-1]
    padded = ((counts + tm - 1) // tm) * tm
    ends = jnp.cumsum(padded)
    starts = ends - padded
    dest = starts[experts] + rank
    n_rows = 2 * t + N_EXPERTS * tm
    row_token = jnp.zeros((n_rows,), jnp.int32).at[dest].set(jnp.arange(2 * t, dtype=jnp.int32) // 2)
    tile_start = jnp.arange(n_rows // tm, dtype=jnp.int32) * tm
    tile_expert = jnp.minimum(jnp.sum((tile_start[:, None] >= ends[None, :]).astype(jnp.int32), axis=1),
                              N_EXPERTS - 1)
    tile_valid = (tile_start < ends[-1]).astype(jnp.int32)
    real_end = (starts + counts)[tile_expert]
    tile_rows = jnp.clip(real_end - tile_start, 0, tm).astype(jnp.int32) * tile_valid
    return row_token, dest.astype(jnp.int32), tile_expert, tile_valid, tile_rows, n_rows


def _tiled(vec, reps):
    return jnp.tile(vec.astype(F32), reps)


def _mixer(h, seq, batch, layer, w_in_t, w_branch, w_out, rel_bias, cmp_pos, cmp_w1, cmp_w2,
           nsa_q_g, nsa_k_g, diff_q_g, diff_k_g, diff_lam, diff_out_g, lam_init, x2, mod_l, d):
    scale = HEAD_DIM ** -0.5
    ones = lambda n: jnp.ones((n,), F32)
    zeros = lambda n: jnp.zeros((n,), F32)
    proj = functools.partial(_matmul_w32, h, w_in_t, (layer,), tm=1024, tn=512, transposed=True)

    cs = jnp.concatenate([jnp.full((BRANCH_W,), scale, F32), ones(2 * BRANCH_W)]).reshape(1, -1)
    p_sb = proj(_C_SB, 3 * BRANCH_W, [(cs, "col")], _ep_scale, BF16, name="proj_sb")
    nf = ones(BRANCH_W).reshape(1, -1)
    cs = (_tiled(nsa_q_g, N_HEADS) * scale).reshape(1, -1)
    p_nsq = proj(_C_NSQ, BRANCH_W, [(nf, "col"), (cs, "col")],
                 functools.partial(_ep_norm, width=HEAD_DIM), BF16, name="proj_nsq")
    nf = jnp.concatenate([zeros(2 * KV_W), ones(KV_W), zeros(KV_W), ones(KV_W), zeros(KV_W)]).reshape(1, -1)
    cs = jnp.concatenate([ones(2 * KV_W), _tiled(nsa_k_g[1], KV_HEADS), ones(KV_W),
                          _tiled(nsa_k_g[2], KV_HEADS), ones(KV_W)]).reshape(1, -1)
    p_nskv = proj(_C_NSKV, 6 * KV_W, [(nf, "col"), (cs, "col")],
                  functools.partial(_ep_norm, width=HEAD_DIM), BF16, name="proj_nskv")
    ns_gates = proj(_C_GNS, LANES, [], _ep_sigmoid, F32, name="proj_nsgate")
    nf = jnp.concatenate([ones(2 * BRANCH_W), zeros(BRANCH_W)]).reshape(1, -1)
    cs = jnp.concatenate([_tiled(diff_q_g, 2 * N_HEADS) * DIFF_QK ** -0.5, _tiled(diff_k_g, 2 * N_HEADS),
                          ones(BRANCH_W)]).reshape(1, -1)
    p_df = proj(_C_DF, 3 * BRANCH_W, [(nf, "col"), (cs, "col")],
                functools.partial(_ep_norm, width=DIFF_QK), BF16, name="proj_diff")
    m_gates = proj(_C_MERGE, 3 * d, [], _ep_sigmoid, BF16, name="proj_merge_gate")

    o_sb = _sb_attention(p_sb, batch, seq)
    kvcm = _compress(p_nskv, cmp_pos, cmp_w1, cmp_w2, nsa_k_g[0], batch, seq)
    o_ns = _nsa_attention(p_nsq, kvcm, p_nskv, ns_gates, rel_bias[:, :N_HEADS], nsa_q_g, nsa_k_g, batch, seq)
    o_df = _diff_attention(p_df, rel_bias[:, N_HEADS:], diff_lam, diff_out_g, diff_q_g, diff_k_g, lam_init,
                           batch, seq)
    merged = _branch_merge(o_sb, o_ns, o_df, w_branch, layer, m_gates)
    g1 = lax.slice_in_dim(mod_l, 2 * d, 3 * d, axis=1)
    return _matmul_w32(merged, w_out, (layer,), 0, d, [(x2, "tile"), (g1, "batch")], _ep_residual, F32,
                       tm=1024, tn=512, name="proj_out", seq=seq)


def _moe(x2, seq, mod_l, norm_gain, w_router, router_bias, layer, w_gate, w_up, w_down):
    tm = 256
    h2, route = _norm_router(x2, norm_gain, mod_l, 3, 4, seq, w_router, router_bias)
    row_token, pos, tile_expert, tile_valid, tile_rows, n_rows = _routing_tables(route, tm)
    xs = _gather_rows(row_token, tile_rows, h2, n_rows, tm)
    y = _expert_mlps(tile_expert, tile_valid, xs, layer, w_gate, w_up, w_down, tm)
    return _combine(pos, y, x2, mod_l, 5, route, seq)


def kernel(x, c, rel_bias, w_router, router_bias, w_ada, b_ada, norm_mix, norm_ffn, w_in, nsa_cmp_pos,
           nsa_cmp_w1, nsa_cmp_w2, nsa_q_norm, nsa_k_norm, diff_q_norm, diff_k_norm, diff_lambda,
           diff_out_norm, w_branch, w_out, w_exp_gate, w_exp_up, w_exp_down):
    batch, seq, d = x.shape
    depth = w_ada.shape[0]
    mod = _adaln(c, w_ada, b_ada)
    x2 = x.reshape(batch * seq, d)
    w_in_t = jnp.swapaxes(w_in, 1, 2)
    for layer in range(depth):
        lam_init = 0.8 - 0.6 * math.exp(-0.3 * layer)
        mod_l = mod[layer]
        h = _norm_mod(x2, norm_mix[layer], mod_l, 0, 1, seq)
        x2 = _mixer(h, seq, batch, layer, w_in_t, w_branch, w_out, rel_bias,
                    nsa_cmp_pos[layer], nsa_cmp_w1[layer], nsa_cmp_w2[layer], nsa_q_norm[layer],
                    nsa_k_norm[layer], diff_q_norm[layer], diff_k_norm[layer], diff_lambda[layer],
                    diff_out_norm[layer], lam_init, x2, mod_l, d)
        x2 = _moe(x2, seq, mod_l, norm_ffn[layer], w_router, router_bias, layer, w_exp_gate,
                  w_exp_up, w_exp_down)
    return x2.reshape(batch, seq, d)
```

```python
import functools
import math

import numpy as np
import jax
import jax.numpy as jnp
from jax import lax
from jax.experimental import pallas as pl
from jax.experimental.pallas import tpu as pltpu

F32 = jnp.float32
BF16 = jnp.bfloat16

HEAD_DIM = 128
N_HEADS = 8
KV_HEADS = 2
Q_PER_KV = N_HEADS // KV_HEADS
DIFF_QK = HEAD_DIM // 2
BRANCH_W = N_HEADS * HEAD_DIM
KV_W = KV_HEADS * HEAD_DIM
CMP_LEN = 32
CMP_STRIDE = 16
SLC_LEN = 64
N_SLC = 16
N_LOCAL_SLC = 2
WINDOW = 512
REL_BUCKETS = 32
REL_MAX_DIST = 128
N_EXPERTS = 16
N_GROUPS = 4
EXP_PER_GROUP = N_EXPERTS // N_GROUPS
NORM_EPS = 1e-6
NEG_INF = -1e30
MASKED_BELOW = -1e29
FORCE_SCORE = 1e9
LANES = 128
SUBLANES = 8
VMEM_CAP = 56 * 1024 * 1024
SB_EXIT = 100.0

_C_SB = 0
_C_NSQ = 3 * BRANCH_W
_C_NSKV = _C_NSQ + BRANCH_W
_C_GNS = _C_NSKV + 6 * KV_W
_C_DF = _C_GNS + 3 * N_HEADS
_C_MERGE = _C_DF + 3 * BRANCH_W


def _cparams(sem, vmem_bytes):
    lim = int(min(VMEM_CAP, max(vmem_bytes * 5 // 4 + (2 << 20), 16 << 20)))
    return pltpu.CompilerParams(dimension_semantics=sem, vmem_limit_bytes=lim)


def _dot(a, b):
    return jnp.dot(a, b, preferred_element_type=F32)


def _dot_nt(a, b):
    return lax.dot_general(a, b, (((1,), (1,)), ((), ())), preferred_element_type=F32)


def _split_bf16(x):
    hi = x.astype(BF16)
    lo = (x - hi.astype(F32)).astype(BF16)
    return hi, lo


def _ada_kernel(cb_ref, w_ref, b_ref, o_ref, ca_sc, *, nb, rows_per_iter):
    k_dim, tn = w_ref.shape

    @pl.when((pl.program_id(0) == 0) & (pl.program_id(1) == 0))
    def _():
        cb = cb_ref[...]
        ca_sc[...] = cb * jax.nn.sigmoid(cb)

    reps = tn // LANES
    groups = rows_per_iter // SUBLANES

    def body(i, accs):
        r0 = pl.multiple_of(i * rows_per_iter, rows_per_iter)
        w3 = w_ref[pl.ds(r0, rows_per_iter), :].reshape(groups, SUBLANES, tn)
        new = []
        for b in range(nb):
            ca = ca_sc[b, pl.ds(r0, rows_per_iter), :].reshape(groups, SUBLANES, LANES)
            ca = jnp.concatenate([ca] * reps, axis=2)
            new.append(accs[b] + jnp.sum(w3 * ca, axis=0))
        return tuple(new)

    accs = lax.fori_loop(0, k_dim // rows_per_iter, body,
                         tuple(jnp.zeros((SUBLANES, tn), F32) for _ in range(nb)))
    rows = [jnp.sum(a, axis=0, keepdims=True) + b_ref[...] for a in accs]
    rows.append(jnp.zeros((SUBLANES - nb, tn), F32))
    o_ref[...] = jnp.concatenate(rows, axis=0)


def _adaln(c, w_ada, b_ada):
    depth, d, n = w_ada.shape
    nb = c.shape[0]
    tn = 512
    cb = jnp.broadcast_to(c[:, :, None], (nb, d, LANES))
    vmem = 2 * d * tn * 4 + 3 * nb * d * LANES * 4
    return pl.pallas_call(
        functools.partial(_ada_kernel, nb=nb, rows_per_iter=64),
        grid=(depth, n // tn),
        in_specs=[pl.BlockSpec((nb, d, LANES), lambda l, j: (0, 0, 0)),
                  pl.BlockSpec((None, d, tn), lambda l, j: (l, 0, j)),
                  pl.BlockSpec((None, 1, tn), lambda l, j: (l, 0, j))],
        out_specs=pl.BlockSpec((None, SUBLANES, tn), lambda l, j: (l, 0, j)),
        out_shape=jax.ShapeDtypeStruct((depth, SUBLANES, n), F32),
        scratch_shapes=[pltpu.VMEM((nb, d, LANES), F32)],
        compiler_params=_cparams(("arbitrary", "arbitrary"), vmem),
        name="adaln_mod",
    )(cb, w_ada, b_ada.reshape(depth, 1, n))


def _modulated_norm(x_ref, gain_ref, sc_ref, sh_ref, tiles_per_batch):
    b = pl.program_id(0) // tiles_per_batch
    x = x_ref[...]
    ms = jnp.mean(x * x, axis=-1, keepdims=True)
    y = x * lax.rsqrt(ms + NORM_EPS) * gain_ref[...]
    return y * (1.0 + sc_ref[pl.ds(b, 1), :]) + sh_ref[pl.ds(b, 1), :]


def _norm_kernel(x_ref, gain_ref, sc_ref, sh_ref, o_ref, *, tiles_per_batch):
    o_ref[...] = _modulated_norm(x_ref, gain_ref, sc_ref, sh_ref, tiles_per_batch).astype(o_ref.dtype)


def _norm_mod(x2, gain, mod_l, sh_chunk, sc_chunk, seq):
    t, d = x2.shape
    tm = 256
    return pl.pallas_call(
        functools.partial(_norm_kernel, tiles_per_batch=seq // tm),
        grid=(t // tm,),
        in_specs=[pl.BlockSpec((tm, d), lambda i: (i, 0)),
                  pl.BlockSpec((1, d), lambda i: (0, 0)),
                  pl.BlockSpec((SUBLANES, d), lambda i: (0, sc_chunk)),
                  pl.BlockSpec((SUBLANES, d), lambda i: (0, sh_chunk))],
        out_specs=pl.BlockSpec((tm, d), lambda i: (i, 0)),
        out_shape=jax.ShapeDtypeStruct((t, d), BF16),
        compiler_params=_cparams(("arbitrary",), 2 * tm * d * 6 + 4 * SUBLANES * d * 4),
        name="norm_mix",
    )(x2, gain.reshape(1, d), mod_l, mod_l)


def _norm_router_kernel(x_ref, gain_ref, sc_ref, sh_ref, wr_ref, rb_ref, h_ref, route_ref,
                        *, tiles_per_batch):
    h = _modulated_norm(x_ref, gain_ref, sc_ref, sh_ref, tiles_per_batch)
    h_hi, h_lo = _split_bf16(h)
    bits = pltpu.bitcast(h_hi.astype(F32), jnp.uint32)
    half = bits.shape[1] // 2
    h_ref[...] = (bits[:, half:] & jnp.uint32(0xFFFF0000)) | lax.shift_right_logical(bits[:, :half],
                                                                                     jnp.uint32(16))
    w_hi, w_lo = _split_bf16(wr_ref[...])
    logits = _dot(h_hi, w_hi) + _dot(h_hi, w_lo) + _dot(h_lo, w_hi)
    aff = jax.nn.sigmoid(logits)
    biased = aff + rb_ref[...]
    lane_i = lax.broadcasted_iota(jnp.int32, aff.shape, 1)
    lane = lane_i.astype(F32)
    low = jnp.float32(-3e38)
    big = jnp.float32(1 << 20)
    best = None
    for gi in range(N_GROUPS):
        in_group = (lane_i // EXP_PER_GROUP) == gi
        v = jnp.where(in_group, biased, low)
        m1 = jnp.max(v, axis=-1, keepdims=True)
        i1 = jnp.min(jnp.where(v == m1, lane, big), axis=-1, keepdims=True)
        v2 = jnp.where(lane == i1, low, v)
        m2 = jnp.max(v2, axis=-1, keepdims=True)
        i2 = jnp.min(jnp.where(v2 == m2, lane, big), axis=-1, keepdims=True)
        score = m1 + m2
        if best is None:
            best = (score, i1, i2)
        else:
            better = score > best[0]
            best = (jnp.where(better, score, best[0]), jnp.where(better, i1, best[1]),
                    jnp.where(better, i2, best[2]))
    _, e1, e2 = best
    a1 = jnp.sum(jnp.where(lane == e1, aff, 0.0), axis=-1, keepdims=True)
    a2 = jnp.sum(jnp.where(lane == e2, aff, 0.0), axis=-1, keepdims=True)
    den = a1 + a2
    out = jnp.where(lane_i == 0, e1,
                    jnp.where(lane_i == 1, e2,
                              jnp.where(lane_i == 2, a1 / den, jnp.where(lane_i == 3, a2 / den, 0.0))))
    route_ref[...] = out


def _norm_router(x2, gain, mod_l, sh_chunk, sc_chunk, seq, w_router, router_bias):
    t, d = x2.shape
    tm = 256
    wr = jnp.pad(w_router, ((0, 0), (0, LANES - N_EXPERTS)))
    rb = jnp.pad(router_bias, (0, LANES - N_EXPERTS)).reshape(1, LANES)
    return pl.pallas_call(
        functools.partial(_norm_router_kernel, tiles_per_batch=seq // tm),
        grid=(t // tm,),
        in_specs=[pl.BlockSpec((tm, d), lambda i: (i, 0)),
                  pl.BlockSpec((1, d), lambda i: (0, 0)),
                  pl.BlockSpec((SUBLANES, d), lambda i: (0, sc_chunk)),
                  pl.BlockSpec((SUBLANES, d), lambda i: (0, sh_chunk)),
                  pl.BlockSpec((d, LANES), lambda i: (0, 0)),
                  pl.BlockSpec((1, LANES), lambda i: (0, 0))],
        out_specs=[pl.BlockSpec((tm, d // 2), lambda i: (i, 0)),
                   pl.BlockSpec((tm, LANES), lambda i: (i, 0))],
        out_shape=[jax.ShapeDtypeStruct((t, d // 2), jnp.uint32),
                   jax.ShapeDtypeStruct((t, LANES), F32)],
        compiler_params=_cparams(("arbitrary",), 2 * tm * d * 8 + 4 * d * LANES * 4 + 8 * tm * d),
        name="norm_ffn_router",
    )(x2, gain.reshape(1, d), mod_l, mod_l, wr, rb)


def _mm_w32_kernel(a_ref, w_ref, *rest, epilogue, n_extra, tiles_per_batch, transposed):
    extras = rest[:n_extra]
    o_ref, w_sc = rest[n_extra], rest[n_extra + 1]

    @pl.when(pl.program_id(1) == 0)
    def _():
        w_sc[...] = w_ref[...].reshape(w_sc.shape).astype(BF16)

    acc = _dot_nt(a_ref[...], w_sc[...]) if transposed else _dot(a_ref[...], w_sc[...])
    o_ref[...] = epilogue(acc, extras, pl.program_id(1) // tiles_per_batch).astype(o_ref.dtype)


def _matmul_w32(a, w, lead, col_lo, n, extras, epilogue, out_dtype, *, tm, tn, name, seq=None,
                transposed=False):
    m, k = a.shape
    tm = min(tm, m)
    tn = min(tn, n)
    assert m % tm == 0 and n % tn == 0
    none = (None,) * len(lead)
    if transposed:
        assert col_lo % SUBLANES == 0
        w_spec = pl.BlockSpec(tuple(pl.Element(1) for _ in lead) + (pl.Element(tn), pl.Element(k)),
                              lambda j, i: lead + (pl.multiple_of(col_lo + j * tn, SUBLANES), 0))
    else:
        assert col_lo % tn == 0
        w_spec = pl.BlockSpec(none + (k, tn), lambda j, i: lead + (0, col_lo // tn + j))
    in_specs = [pl.BlockSpec((tm, k), lambda j, i: (i, 0)), w_spec]
    args = [a, w]
    vmem = 2 * tm * k * 2 + 2 * k * tn * 4 + k * tn * 2 + 2 * tm * tn * jnp.dtype(out_dtype).itemsize + 2 * tm * tn * 4
    for arr, kind in extras:
        if kind == "tile":
            in_specs.append(pl.BlockSpec((tm, tn), lambda j, i: (i, j)))
            vmem += 2 * tm * tn * arr.dtype.itemsize
        else:
            rows = 1 if kind == "col" else SUBLANES
            in_specs.append(pl.BlockSpec((rows, tn), lambda j, i: (0, j)))
            vmem += 2 * SUBLANES * tn * 4
        args.append(arr)
    tiles_per_batch = (seq // tm) if seq else 1
    return pl.pallas_call(
        functools.partial(_mm_w32_kernel, epilogue=epilogue, n_extra=len(extras),
                          tiles_per_batch=tiles_per_batch, transposed=transposed),
        grid=(n // tn, m // tm),
        in_specs=in_specs,
        out_specs=pl.BlockSpec((tm, tn), lambda j, i: (i, j)),
        out_shape=jax.ShapeDtypeStruct((m, n), out_dtype),
        scratch_shapes=[pltpu.VMEM((tn, k) if transposed else (k, tn), BF16)],
        compiler_params=_cparams(("arbitrary", "arbitrary"), vmem),
        name=name,
    )(*args)


def _ep_scale(acc, extras, b):
    return acc * extras[0][...]


def _ep_sigmoid(acc, extras, b):
    return 0.5 * jnp.tanh(0.5 * acc) + 0.5


def _ep_norm(acc, extras, b, *, width):
    nf_ref, cs_ref = extras
    outs = []
    for c in range(acc.shape[1] // LANES):
        sl = slice(c * LANES, (c + 1) * LANES)
        blk = acc[:, sl]
        sq = blk * blk
        if width == LANES:
            ms = jnp.mean(sq, axis=-1, keepdims=True)
        else:
            lane = lax.broadcasted_iota(jnp.int32, blk.shape, 1)
            first = lane < width
            lo = jnp.sum(jnp.where(first, sq, 0.0), axis=-1, keepdims=True) / width
            hi = jnp.sum(jnp.where(first, 0.0, sq), axis=-1, keepdims=True) / width
            ms = jnp.where(first, lo, hi)
        inv = lax.rsqrt(ms + NORM_EPS)
        f = jnp.where(nf_ref[:, sl] > 0.0, inv, 1.0)
        outs.append(blk * f * cs_ref[:, sl])
    return jnp.concatenate(outs, axis=1)


def _ep_residual(acc, extras, b):
    x_ref, g_ref = extras
    return x_ref[...] + g_ref[pl.ds(b, 1), :] * acc


def _bucket_np(dist):
    n = np.maximum(dist, 0)
    max_exact = REL_BUCKETS // 2
    nf = np.maximum(n, max_exact).astype(np.float32)
    large = max_exact + (np.log(nf / np.float32(max_exact))
                         / np.float32(math.log(REL_MAX_DIST / max_exact))
                         * np.float32(REL_BUCKETS - max_exact)).astype(np.int32)
    large = np.minimum(large, REL_BUCKETS - 1)
    return np.where(n < max_exact, n, large).astype(np.int32)


def _bucket_starts():
    b = _bucket_np(np.arange(4 * REL_MAX_DIST))
    return [int(np.argmax(b >= k)) for k in range(1, REL_BUCKETS)]


def _delta_bias(table, dist, valid):
    tb = table.T.astype(F32)
    shape = (tb.shape[0],) + (1,) * dist.ndim
    val = jnp.broadcast_to(tb[:, 0].reshape(shape), (tb.shape[0],) + dist.shape)
    for k, start in enumerate(_bucket_starts(), start=1):
        val = jnp.where(dist[None] >= start, tb[:, k].reshape(shape), val)
    val = val - tb[:, REL_BUCKETS - 1].reshape(shape)
    return jnp.where(valid[None], val, NEG_INF)


def _toeplitz_tiles(table, offs, nr, nc, lo, hi):
    dmin = -(nc - 1)
    d = jnp.arange(dmin, max(offs) + nr, dtype=jnp.int32)
    g = _delta_bias(table, d, (d >= lo) & (d < hi))
    nh = g.shape[0]
    w = nr + nc - 1
    tiles = []
    for off in offs:
        base = off - dmin
        ahead = g[:, base - nc + 1:base + 1][:, ::-1]
        behind = g[:, base + 1:base + nr][:, ::-1]
        ring = jnp.concatenate([ahead, behind], axis=1)
        flat = jnp.tile(ring, (1, nr))[:, :nr * (w - 1)]
        tiles.append(flat.reshape(nh, nr, w - 1)[:, :, :nc])
    return jnp.stack(tiles, axis=1)


def _stack_group_heads(tiles):
    nh, k, nr, nc = tiles.shape
    t = tiles.reshape(KV_HEADS, Q_PER_KV, k, nr, nc)
    return jnp.transpose(t, (0, 2, 1, 3, 4)).reshape(KV_HEADS, k, Q_PER_KV * nr, nc)


SB_HEADS_PER_STEP = 2


def _sb_kernel(q_ref, k_ref, v_ref, o_ref, *, tq):
    qi = pl.program_id(2)
    heads = [slice(h * HEAD_DIM, (h + 1) * HEAD_DIM) for h in range(SB_HEADS_PER_STEP)]
    qs = [q_ref[:, hs] for hs in heads]
    row = lax.broadcasted_iota(jnp.int32, (tq, tq), 0)
    col = lax.broadcasted_iota(jnp.int32, (tq, tq), 1)
    later = jnp.where(row > col, 1.0, 0.0).astype(BF16)
    causal = col < row

    def step(kj, state, diag):
        k0 = pl.multiple_of(kj * tq, tq)
        out = []
        for q, hs, (carry, acc) in zip(qs, heads, state):
            z = _dot_nt(q, k_ref[pl.ds(k0, tq), hs])
            log_succ = jnp.minimum(z, 0.0) - jnp.log1p(jnp.exp(-jnp.abs(z)))
            log_fail = log_succ - z
            if diag:
                log_fail = jnp.where(causal, log_fail, 0.0)
            hi, lo = _split_bf16(log_fail)
            suffix = _dot(hi, later) + _dot(lo, later)
            a = jnp.exp(log_succ + suffix + carry)
            if diag:
                a = jnp.where(causal, a, 0.0)
            acc = acc + _dot(a.astype(BF16), v_ref[pl.ds(k0, tq), hs])
            carry = carry + jnp.sum(log_fail, axis=-1, keepdims=True)
            out.append((carry, acc))
        return tuple(out)

    zero = (jnp.zeros((tq, 1), F32), jnp.zeros((tq, HEAD_DIM), F32))
    state = step(qi, (zero,) * SB_HEADS_PER_STEP, True)

    def cond(loop):
        kj, state = loop
        live = functools.reduce(jnp.maximum, [jnp.max(carry) for carry, _ in state])
        return (kj >= 0) & (live > -SB_EXIT)

    def body(loop):
        kj, state = loop
        return kj - 1, step(kj, state, False)

    _, state = lax.while_loop(cond, body, (qi - 1, state))
    for hs, (_, acc) in zip(heads, state):
        o_ref[:, hs] = acc.astype(o_ref.dtype)


def _sb_attention(p_sb, batch, seq):
    tq = min(256, seq)
    nq = seq // tq
    t = batch * seq
    hp = SB_HEADS_PER_STEP
    w = hp * HEAD_DIM
    groups = N_HEADS // hp
    vmem = 4 * seq * w * 2 + 4 * tq * w * 2 + 16 * hp * tq * tq * 4
    return pl.pallas_call(
        functools.partial(_sb_kernel, tq=tq),
        grid=(batch, groups, nq),
        in_specs=[pl.BlockSpec((tq, w), lambda b, h, i: (b * nq + i, h)),
                  pl.BlockSpec((seq, w), lambda b, h, i: (b, groups + h)),
                  pl.BlockSpec((seq, w), lambda b, h, i: (b, 2 * groups + h))],
        out_specs=pl.BlockSpec((tq, w), lambda b, h, i: (b * nq + i, h)),
        out_shape=jax.ShapeDtypeStruct((t, BRANCH_W), BF16),
        compiler_params=_cparams(("arbitrary",) * 3, vmem),
        name="sb_attention",
    )(p_sb, p_sb, p_sb)


LOGIT_CAP = 40.0


def _with_ones(v):
    return jnp.concatenate([v, jnp.ones_like(v)], axis=1)


def _softmax_first(s, v, m_sc, acc_sc, bounded):
    if bounded:
        acc_sc[...] = _dot(jnp.exp(s).astype(BF16), _with_ones(v))
        return
    m = jnp.max(s, axis=-1, keepdims=True)
    m_sc[...] = m
    acc_sc[...] = _dot(jnp.exp(s - m).astype(BF16), _with_ones(v))


def _softmax_next(s, v, m_sc, acc_sc, bounded):
    if bounded:
        acc_sc[...] = acc_sc[...] + _dot(jnp.exp(s).astype(BF16), _with_ones(v))
        return
    m_old = m_sc[...]
    m_new = jnp.maximum(m_old, jnp.max(s, axis=-1, keepdims=True))
    p = jnp.exp(s - m_new)
    acc_sc[...] = jnp.exp(m_old - m_new) * acc_sc[...] + _dot(p.astype(BF16), _with_ones(v))
    m_sc[...] = m_new


def _softmax_result(acc_sc, bounded):
    acc = acc_sc[...]
    total = acc[:, HEAD_DIM:]
    den = jnp.where(total > 0.0, total, 1.0) if bounded else jnp.maximum(total, 1.0)
    return acc[:, :HEAD_DIM] / den


def _logit_bound_ok(width, scale, q_gain, k_gains, table):
    qk = width * scale * jnp.max(jnp.abs(q_gain)) * jnp.max(jnp.abs(k_gains)) * 1.02
    delta = jnp.max(jnp.abs(table - table[REL_BUCKETS - 1:]))
    return (qk + delta <= LOGIT_CAP).astype(jnp.int32).reshape(1)


def _compress_kernel(x_ref, p_ref, w1a_ref, w1b_ref, w2_ref, gain_ref, o_ref):
    nc = x_ref.shape[0]
    is_key = pl.program_id(1) < KV_HEADS
    x = x_ref[...].astype(F32)
    first = _dot((x + p_ref[0:1, :]).astype(BF16), w1a_ref[...])
    second = _dot((x + p_ref[1:2, :]).astype(BF16), w1b_ref[...])
    pre = first + pltpu.roll(second, nc - 1, 0)
    hid = pre * jax.nn.sigmoid(pre)
    out = _dot(hid.astype(BF16), w2_ref[...])
    ms = jnp.mean(out * out, axis=-1, keepdims=True)
    normed = out * lax.rsqrt(ms + NORM_EPS) * gain_ref[...]
    o_ref[...] = jnp.where(is_key, normed, out).astype(o_ref.dtype)


def _compress(p_nskv, cmp_pos, cmp_w1, cmp_w2, k_gain0, batch, seq):
    nc = seq // CMP_STRIDE
    half = CMP_STRIDE * HEAD_DIM
    x = p_nskv[:, :2 * KV_W].reshape(batch, nc, CMP_STRIDE, 2 * KV_HEADS, HEAD_DIM)
    x = jnp.transpose(x, (0, 3, 1, 2, 4)).reshape(batch, 2 * KV_HEADS, nc, half)
    pos = cmp_pos.reshape(2, 2, half)
    w1 = cmp_w1.astype(BF16).reshape(2, 2, half, HEAD_DIM)
    w2 = cmp_w2.astype(BF16)
    return pl.pallas_call(
        _compress_kernel,
        grid=(batch, 2 * KV_HEADS),
        in_specs=[pl.BlockSpec((None, None, nc, half), lambda b, j: (b, j, 0, 0)),
                  pl.BlockSpec((None, 2, half), lambda b, j: (j // KV_HEADS, 0, 0)),
                  pl.BlockSpec((None, None, half, HEAD_DIM), lambda b, j: (j // KV_HEADS, 0, 0, 0)),
                  pl.BlockSpec((None, None, half, HEAD_DIM), lambda b, j: (j // KV_HEADS, 1, 0, 0)),
                  pl.BlockSpec((None, HEAD_DIM, HEAD_DIM), lambda b, j: (j // KV_HEADS, 0, 0)),
                  pl.BlockSpec((1, HEAD_DIM), lambda b, j: (0, 0))],
        out_specs=pl.BlockSpec((None, None, nc, HEAD_DIM), lambda b, j: (b, j, 0, 0)),
        out_shape=jax.ShapeDtypeStruct((batch, 2 * KV_HEADS, nc, HEAD_DIM), BF16),
        compiler_params=_cparams(("arbitrary", "arbitrary"), 8 * nc * half * 2 + 8 * half * HEAD_DIM * 2),
        name="nsa_compress",
    )(x, pos, w1, w1, w2, k_gain0.reshape(1, HEAD_DIM))


def _nsa_kernel(ok_ref, q_ref, kcm_ref, vcm_ref, ks_ref, vs_ref, kw_ref, vw_ref, gate_ref,
                pt_ref, bt_ref, wt_ref, o_ref, m_sc, acc_sc, *, tq, tk, tw, n_blk, k_sel):
    qi = pl.program_id(2)
    q0 = qi * tq
    rows = Q_PER_KV * tq
    nc = kcm_ref.shape[0]
    q4 = jnp.concatenate([q_ref[:, r * HEAD_DIM:(r + 1) * HEAD_DIM] for r in range(Q_PER_KV)], axis=0)

    per_tile = tq // CMP_STRIDE
    lead = 2 * per_tile
    shift = lax.rem(qi * per_tile + (nc - lead), nc)
    cb = pltpu.roll(pt_ref[...], shift, 1)
    rel = lax.broadcasted_iota(jnp.int32, (rows, nc), 1) - qi * per_tile
    cb = jnp.where(rel < -lead, 0.0, jnp.where(rel >= per_tile, NEG_INF, cb))
    s = _dot_nt(q4, kcm_ref[...]) + cb
    valid = s > MASKED_BELOW
    s = jnp.where(valid, s, NEG_INF)
    p = jnp.where(valid, jnp.exp(s - jnp.max(s, axis=-1, keepdims=True)), 0.0)
    p = p / jnp.maximum(jnp.sum(p, axis=-1, keepdims=True), 1.0)
    o_cmp = _dot(p.astype(BF16), vcm_ref[...])

    psum = p[0:tq]
    for r in range(1, Q_PER_KV):
        psum = psum + p[r * tq:(r + 1) * tq]
    n_id = lax.broadcasted_iota(jnp.int32, (nc, LANES), 0)
    j_id = lax.broadcasted_iota(jnp.int32, (nc, LANES), 1)
    overlap = ((CMP_STRIDE * n_id < SLC_LEN * j_id + SLC_LEN)
               & (CMP_STRIDE * n_id + CMP_LEN > SLC_LEN * j_id) & (j_id < n_blk))
    overlap = jnp.where(overlap, 1.0, 0.0).astype(BF16)
    p_hi, p_lo = _split_bf16(psum)
    imp = _dot(p_hi, overlap) + _dot(p_lo, overlap)
    nr = -(-n_blk // SUBLANES) * SUBLANES
    imp_t = jnp.transpose(imp)[:nr]
    blk = lax.broadcasted_iota(jnp.int32, (nr, tq), 0)
    cur = (q0 + lax.broadcasted_iota(jnp.int32, (nr, tq), 1)) // SLC_LEN
    forced = (blk == 0) | ((blk <= cur) & (blk > cur - N_LOCAL_SLC))
    imp_t = jnp.where(forced, FORCE_SCORE, jnp.where(blk > cur, NEG_INF, imp_t))
    imp_t = jnp.where(blk < n_blk, imp_t, -3e38)
    rank = jnp.zeros((nr, tq), F32)
    for j in range(n_blk):
        cj = imp_t[j:j + 1, :]
        ahead = jnp.where(blk > j, jnp.where(cj >= imp_t, 1.0, 0.0), jnp.where(cj > imp_t, 1.0, 0.0))
        rank = rank + ahead
    sel_t = jnp.where((rank < k_sel) & (blk < n_blk), 1.0, 0.0)
    if nr < LANES:
        sel_t = jnp.concatenate([sel_t, jnp.zeros((LANES - nr, tq), F32)], axis=0)
    sel = jnp.transpose(sel_t).astype(BF16)

    per = tk // tq
    kd = qi // per
    par = qi - kd * per
    e_row = lax.broadcasted_iota(jnp.int32, (LANES, tk), 0)
    e_col = lax.broadcasted_iota(jnp.int32, (LANES, tk), 1)

    def slc_scores(kj, bias):
        k0 = pl.multiple_of(kj * tk, tk)
        expand = jnp.where((k0 + e_col) // SLC_LEN == e_row, 1.0, 0.0).astype(BF16)
        chosen = _dot(sel, expand)
        s = _dot_nt(q4, ks_ref[pl.ds(k0, tk), :])
        if bias is not None:
            s = s + bias
        s = jnp.where(chosen[None] > 0.5, s.reshape(Q_PER_KV, tq, tk), NEG_INF).reshape(rows, tk)
        return s, vs_ref[pl.ds(k0, tk), :]

    def attend(bounded):
        s, v = slc_scores(kd, bt_ref[par])
        _softmax_first(s, v, m_sc, acc_sc, bounded)

        @pl.when(kd >= 1)
        def _():
            s, v = slc_scores(kd - 1, bt_ref[per + par])
            _softmax_next(s, v, m_sc, acc_sc, bounded)

        def slc_far(i, _):
            s, v = slc_scores(kd - 2 - i, None)
            _softmax_next(s, v, m_sc, acc_sc, bounded)
            return 0

        lax.fori_loop(0, jnp.maximum(kd - 1, 0), slc_far, 0)
        o_slc = _softmax_result(acc_sc, bounded)

        start = pl.multiple_of(jnp.maximum(q0 - WINDOW, 0), tq)
        s = _dot_nt(q4, kw_ref[pl.ds(start, tw), :]) + wt_ref[jnp.minimum(qi, WINDOW // tq)]
        _softmax_first(s, vw_ref[pl.ds(start, tw), :], m_sc, acc_sc, bounded)
        o_win = _softmax_result(acc_sc, bounded)

        gate = gate_ref[...]
        first_group = pl.program_id(1) == 0

        def gate_col(branch, r):
            lo = branch * N_HEADS + r
            hi = lo + Q_PER_KV
            return jnp.where(first_group, gate[:, lo:lo + 1], gate[:, hi:hi + 1])

        for r in range(Q_PER_KV):
            sl = slice(r * tq, (r + 1) * tq)
            o = gate_col(0, r) * o_cmp[sl] + gate_col(1, r) * o_slc[sl] + gate_col(2, r) * o_win[sl]
            o_ref[:, r * HEAD_DIM:(r + 1) * HEAD_DIM] = o.astype(o_ref.dtype)

    @pl.when(ok_ref[0] > 0)
    def _():
        attend(True)

    @pl.when(ok_ref[0] == 0)
    def _():
        attend(False)


def _nsa_attention(p_nsq, kvcm, p_nskv, gates, rel_nsa, q_gain, k_gains, batch, seq):
    tq = 128
    tk = min(512, seq)
    tw = WINDOW + tq
    nq = seq // tq
    nc = seq // CMP_STRIDE
    n_blk = seq // SLC_LEN
    k_sel = min(N_SLC, n_blk)
    t = batch * seq
    rows = Q_PER_KV * tq
    per = tk // tq
    assert n_blk <= LANES and nc % LANES == 0 and WINDOW % tq == 0 and tw <= seq and tq == LANES

    far = 1 << 30
    offs = [i * tq for i in range(2 * per)]
    bt = _stack_group_heads(_toeplitz_tiles(rel_nsa, offs, tq, tk, 0, far))
    offs = [i * tq for i in range(WINDOW // tq + 1)]
    wt = _stack_group_heads(_toeplitz_tiles(rel_nsa, offs, tq, tw, 0, WINDOW))
    per_tile = tq // CMP_STRIDE
    d = (lax.broadcasted_iota(jnp.int32, (1, tq, nc), 1)
         - CMP_STRIDE * (lax.broadcasted_iota(jnp.int32, (1, tq, nc), 2) - 2 * per_tile) - (CMP_LEN - 1))
    pt = _stack_group_heads(_delta_bias(rel_nsa, d, d >= 0))[:, 0]

    ok = _logit_bound_ok(HEAD_DIM, HEAD_DIM ** -0.5, q_gain, k_gains[1:], rel_nsa)
    const = lambda shape: pl.BlockSpec((None,) + shape, lambda b, g, i, ok: (g,) + (0,) * len(shape),
                                       pipeline_mode=pl.Buffered(1))
    vmem = (8 * seq * HEAD_DIM * 2 + (bt.size + wt.size + pt.size) // KV_HEADS * 4
            + 10 * rows * tw * 4 + 8 * rows * nc * 4)
    return pl.pallas_call(
        functools.partial(_nsa_kernel, tq=tq, tk=tk, tw=tw, n_blk=n_blk, k_sel=k_sel),
        grid_spec=pltpu.PrefetchScalarGridSpec(
            num_scalar_prefetch=1,
            grid=(batch, KV_HEADS, nq),
            in_specs=[pl.BlockSpec((tq, Q_PER_KV * HEAD_DIM), lambda b, g, i, ok: (b * nq + i, g)),
                      pl.BlockSpec((None, None, nc, HEAD_DIM), lambda b, g, i, ok: (b, g, 0, 0)),
                      pl.BlockSpec((None, None, nc, HEAD_DIM), lambda b, g, i, ok: (b, KV_HEADS + g, 0, 0)),
                      pl.BlockSpec((seq, HEAD_DIM), lambda b, g, i, ok: (b, 2 * KV_HEADS + g)),
                      pl.BlockSpec((seq, HEAD_DIM), lambda b, g, i, ok: (b, 3 * KV_HEADS + g)),
                      pl.BlockSpec((seq, HEAD_DIM), lambda b, g, i, ok: (b, 4 * KV_HEADS + g)),
                      pl.BlockSpec((seq, HEAD_DIM), lambda b, g, i, ok: (b, 5 * KV_HEADS + g)),
                      pl.BlockSpec((tq, LANES), lambda b, g, i, ok: (b * nq + i, 0)),
                      const(pt.shape[1:]), const(bt.shape[1:]), const(wt.shape[1:])],
            out_specs=pl.BlockSpec((tq, Q_PER_KV * HEAD_DIM), lambda b, g, i, ok: (b * nq + i, g)),
            scratch_shapes=[pltpu.VMEM((rows, 1), F32), pltpu.VMEM((rows, 2 * HEAD_DIM), F32)]),
        out_shape=jax.ShapeDtypeStruct((t, BRANCH_W), BF16),
        compiler_params=_cparams(("arbitrary",) * 3, vmem),
        name="nsa_attention",
    )(ok, p_nsq, kvcm, kvcm, p_nskv, p_nskv, p_nskv, p_nskv, gates, pt, bt, wt)


def _diff_kernel(ok_ref, q_ref, k_ref, v_ref, dt_ref, lam_ref, og_ref, o_ref, m_sc, acc_sc,
                 *, tq, tk, lam_init):
    qi = pl.program_id(2)
    per = tk // tq
    kd = qi // per
    par = qi - kd * per
    q = q_ref[...]
    lane = lax.broadcasted_iota(jnp.int32, q.shape, 1)
    zero = jnp.zeros_like(q)
    q2 = jnp.concatenate([jnp.where(lane < DIFF_QK, q, zero), jnp.where(lane < DIFF_QK, zero, q)], axis=0)

    def scores(kj, bias):
        k0 = pl.multiple_of(kj * tk, tk)
        s = _dot_nt(q2, k_ref[pl.ds(k0, tk), :])
        if bias is not None:
            s = s + jnp.concatenate([bias, bias], axis=0)
        return s, v_ref[pl.ds(k0, tk), :]

    def attend(bounded):
        s, v = scores(kd, dt_ref[par])
        _softmax_first(s, v, m_sc, acc_sc, bounded)

        @pl.when(kd >= 1)
        def _():
            s, v = scores(kd - 1, dt_ref[per + par])
            _softmax_next(s, v, m_sc, acc_sc, bounded)

        def far(i, _):
            s, v = scores(kd - 2 - i, None)
            _softmax_next(s, v, m_sc, acc_sc, bounded)
            return 0

        lax.fori_loop(0, jnp.maximum(kd - 1, 0), far, 0)
        o2 = _softmax_result(acc_sc, bounded)
        lp = lam_ref[...]
        lam = (jnp.exp(jnp.sum(lp[0:1] * lp[1:2], axis=-1, keepdims=True))
               - jnp.exp(jnp.sum(lp[2:3] * lp[3:4], axis=-1, keepdims=True)) + lam_init)
        o = o2[:tq] - lam * o2[tq:]
        ms = jnp.mean(o * o, axis=-1, keepdims=True)
        o = o * lax.rsqrt(ms + NORM_EPS) * og_ref[...]
        o_ref[...] = (o * (1.0 - lam_init)).astype(o_ref.dtype)

    @pl.when(ok_ref[0] > 0)
    def _():
        attend(True)

    @pl.when(ok_ref[0] == 0)
    def _():
        attend(False)


def _diff_attention(p_df, rel_diff, lam_params, out_gain, q_gain, k_gain, lam_init, batch, seq):
    tq = min(256, seq)
    tk = min(512, seq)
    per = tk // tq
    nq = seq // tq
    t = batch * seq
    dt = _toeplitz_tiles(rel_diff, [i * tq for i in range(2 * per)], tq, tk, 0, 1 << 30)
    ok = _logit_bound_ok(DIFF_QK, DIFF_QK ** -0.5, q_gain, k_gain, rel_diff)
    vmem = 4 * seq * HEAD_DIM * 2 + 2 * 2 * per * 2 * tq * tk * 4 + 12 * 2 * tq * tk * 4
    return pl.pallas_call(
        functools.partial(_diff_kernel, tq=tq, tk=tk, lam_init=lam_init),
        grid_spec=pltpu.PrefetchScalarGridSpec(
            num_scalar_prefetch=1,
            grid=(batch, N_HEADS, nq),
            in_specs=[pl.BlockSpec((tq, HEAD_DIM), lambda b, h, i, ok: (b * nq + i, h)),
                      pl.BlockSpec((seq, HEAD_DIM), lambda b, h, i, ok: (b, N_HEADS + h)),
                      pl.BlockSpec((seq, HEAD_DIM), lambda b, h, i, ok: (b, 2 * N_HEADS + h)),
                      pl.BlockSpec((None, 2 * per, tq, tk), lambda b, h, i, ok: (h, 0, 0, 0)),
                      pl.BlockSpec((4, DIFF_QK), lambda b, h, i, ok: (0, 0)),
                      pl.BlockSpec((1, HEAD_DIM), lambda b, h, i, ok: (0, 0))],
            out_specs=pl.BlockSpec((tq, HEAD_DIM), lambda b, h, i, ok: (b * nq + i, h)),
            scratch_shapes=[pltpu.VMEM((2 * tq, 1), F32), pltpu.VMEM((2 * tq, 2 * HEAD_DIM), F32)]),
        out_shape=jax.ShapeDtypeStruct((t, BRANCH_W), BF16),
        compiler_params=_cparams(("arbitrary",) * 3, vmem),
        name="diff_attention",
    )(ok, p_df, p_df, p_df, dt, lam_params, out_gain.reshape(1, HEAD_DIM))


def _merge_kernel(o0_ref, o1_ref, o2_ref, w_ref, g0_ref, g1_ref, g2_ref, out_ref, w_sc):
    @pl.when(pl.program_id(1) == 0)
    def _():
        w_sc[...] = w_ref[...].astype(BF16)

    acc = g0_ref[...].astype(F32) * _dot(o0_ref[...], w_sc[0])
    acc = acc + g1_ref[...].astype(F32) * _dot(o1_ref[...], w_sc[1])
    acc = acc + g2_ref[...].astype(F32) * _dot(o2_ref[...], w_sc[2])
    out_ref[...] = acc.astype(out_ref.dtype)


def _branch_merge(o_sb, o_ns, o_df, w_branch, layer, gates):
    t = o_sb.shape[0]
    d = w_branch.shape[3]
    tm = min(1024, t)
    tn = min(512, d)
    nj = d // tn
    o_spec = pl.BlockSpec((tm, BRANCH_W), lambda j, i: (i, 0))
    vmem = (2 * (3 * tm * BRANCH_W * 2 + 3 * BRANCH_W * tn * 4 + 3 * tm * tn * 2 + tm * tn * 2)
            + 3 * BRANCH_W * tn * 2 + 4 * tm * tn * 4)
    return pl.pallas_call(
        _merge_kernel,
        grid=(nj, t // tm),
        in_specs=[o_spec, o_spec, o_spec,
                  pl.BlockSpec((None, 3, BRANCH_W, tn), lambda j, i: (layer, 0, 0, j)),
                  pl.BlockSpec((tm, tn), lambda j, i: (i, j)),
                  pl.BlockSpec((tm, tn), lambda j, i: (i, nj + j)),
                  pl.BlockSpec((tm, tn), lambda j, i: (i, 2 * nj + j))],
        out_specs=pl.BlockSpec((tm, tn), lambda j, i: (i, j)),
        out_shape=jax.ShapeDtypeStruct((t, d), BF16),
        scratch_shapes=[pltpu.VMEM((3, BRANCH_W, tn), BF16)],
        compiler_params=_cparams(("arbitrary", "arbitrary"), vmem),
        name="branch_merge",
    )(o_sb, o_ns, o_df, w_branch, gates, gates, gates)


def _row_copy(src_ref, src_row, dst_ref, dst_row, sem):
    return pltpu.make_async_copy(src_ref.at[pl.ds(src_row, 1)], dst_ref.at[pl.ds(dst_row, 1)], sem)


def _gather_kernel(idx_ref, cnt_ref, src_ref, o_ref, buf, sem, *, tg):
    i = pl.program_id(0)
    base = i * tg
    n = cnt_ref[i]

    @pl.when(i == 0)
    def _():
        buf[...] = jnp.zeros(buf.shape, buf.dtype)

    for parity in range(2):
        def issue(r2, _, parity=parity):
            r = 2 * r2 + parity
            _row_copy(src_ref, idx_ref[base + r], buf, r, sem).start(priority=parity)
            return 0

        lax.fori_loop(0, (n + 1 - parity) // 2, issue, 0)

    def drain(r, _):
        _row_copy(src_ref, 0, buf, r, sem).wait()
        return 0

    lax.fori_loop(0, n, drain, 0)
    words = buf[...]
    low = pltpu.bitcast(lax.shift_left(words, jnp.uint32(16)), F32)
    high = pltpu.bitcast(words & jnp.uint32(0xFFFF0000), F32)
    o_ref[...] = jnp.concatenate([low, high], axis=1).astype(o_ref.dtype)


def _gather_rows(row_token, tile_rows, h2_words, n_rows, tg):
    half = h2_words.shape[1]
    return pl.pallas_call(
        functools.partial(_gather_kernel, tg=tg),
        grid_spec=pltpu.PrefetchScalarGridSpec(
            num_scalar_prefetch=2,
            grid=(n_rows // tg,),
            in_specs=[pl.BlockSpec(memory_space=pl.ANY)],
            out_specs=pl.BlockSpec((tg, 2 * half), lambda i, idx, cnt: (i, 0)),
            scratch_shapes=[pltpu.VMEM((tg, half), jnp.uint32), pltpu.SemaphoreType.DMA(())]),
        out_shape=jax.ShapeDtypeStruct((n_rows, 2 * half), BF16),
        compiler_params=_cparams(("arbitrary",), tg * half * 4 + 6 * tg * half * 4),
        name="moe_gather",
    )(row_token, tile_rows, h2_words)


def _expert_changed(te_ref, t):
    return (t == 0) | (te_ref[t] != te_ref[jnp.maximum(t - 1, 0)])


def _expert_up_kernel(te_ref, tv_ref, x_ref, wg_ref, wu_ref, o_ref, wg_sc, wu_sc):
    t = pl.program_id(1)

    @pl.when(_expert_changed(te_ref, t))
    def _():
        wg_sc[...] = wg_ref[...].astype(BF16)
        wu_sc[...] = wu_ref[...].astype(BF16)

    @pl.when(tv_ref[t] > 0)
    def _():
        x = x_ref[...]
        g = _dot(x, wg_sc[...])
        u = _dot(x, wu_sc[...])
        o_ref[...] = (g * _ep_sigmoid(g, (), 0) * u).astype(o_ref.dtype)

    @pl.when(tv_ref[t] == 0)
    def _():
        o_ref[...] = jnp.zeros(o_ref.shape, o_ref.dtype)


def _expert_down_kernel(te_ref, tv_ref, h_ref, wd_ref, o_ref, wd_sc):
    t = pl.program_id(1)

    @pl.when(_expert_changed(te_ref, t))
    def _():
        wd_sc[...] = wd_ref[...].astype(BF16)

    @pl.when(tv_ref[t] > 0)
    def _():
        o_ref[...] = _dot(h_ref[...], wd_sc[...])

    @pl.when(tv_ref[t] == 0)
    def _():
        o_ref[...] = jnp.zeros(o_ref.shape, o_ref.dtype)


def _expert_mlps(tile_expert, tile_valid, xs, layer, w_gate, w_up, w_down, tm):
    n_rows, d = xs.shape
    f = w_gate.shape[3]
    tf = min(512, f)
    td = min(2048, d)
    nt = n_rows // tm
    hidden = pl.pallas_call(
        _expert_up_kernel,
        grid_spec=pltpu.PrefetchScalarGridSpec(
            num_scalar_prefetch=2,
            grid=(f // tf, nt),
            in_specs=[pl.BlockSpec((tm, d), lambda j, t, te, tv: (t, 0)),
                      pl.BlockSpec((None, None, d, tf), lambda j, t, te, tv: (layer, te[t], 0, j)),
                      pl.BlockSpec((None, None, d, tf), lambda j, t, te, tv: (layer, te[t], 0, j))],
            out_specs=pl.BlockSpec((tm, tf), lambda j, t, te, tv: (t, j)),
            scratch_shapes=[pltpu.VMEM((d, tf), BF16), pltpu.VMEM((d, tf), BF16)]),
        out_shape=jax.ShapeDtypeStruct((n_rows, f), BF16),
        compiler_params=_cparams(("arbitrary", "arbitrary"),
                                 2 * (tm * d * 2 + 2 * d * tf * 4 + tm * tf * 2) + 2 * d * tf * 2 + 4 * tm * tf * 4),
        name="moe_gate_up",
    )(tile_expert, tile_valid, xs, w_gate, w_up)
    return pl.pallas_call(
        _expert_down_kernel,
        grid_spec=pltpu.PrefetchScalarGridSpec(
            num_scalar_prefetch=2,
            grid=(d // td, nt),
            in_specs=[pl.BlockSpec((tm, f), lambda j, t, te, tv: (t, 0)),
                      pl.BlockSpec((None, None, f, td), lambda j, t, te, tv: (layer, te[t], 0, j))],
            out_specs=pl.BlockSpec((tm, td), lambda j, t, te, tv: (t, j)),
            scratch_shapes=[pltpu.VMEM((f, td), BF16)]),
        out_shape=jax.ShapeDtypeStruct((n_rows, d), F32),
        compiler_params=_cparams(("arbitrary", "arbitrary"),
                                 2 * (tm * f * 2 + f * td * 4 + tm * td * 4) + f * td * 2 + tm * td * 4),
        name="moe_down",
    )(tile_expert, tile_valid, hidden, w_down)


def _combine_kernel(pos_ref, y_ref, x_ref, g_ref, route_ref, o_ref, buf0, buf1, sem,
                    *, tc, tiles_per_batch):
    i = pl.program_id(0)
    base = i * tc

    def issue(r, _):
        _row_copy(y_ref, pos_ref[2 * (base + r)], buf0, r, sem).start(priority=0)
        _row_copy(y_ref, pos_ref[2 * (base + r) + 1], buf1, r, sem).start(priority=1)
        return 0

    lax.fori_loop(0, tc, issue, 0)

    def drain(r, _):
        _row_copy(y_ref, 0, buf0, r, sem).wait()
        _row_copy(y_ref, 0, buf1, r, sem).wait()
        return 0

    lax.fori_loop(0, tc, drain, 0)
    route = route_ref[...]
    moe = route[:, 2:3] * buf0[...] + route[:, 3:4] * buf1[...]
    o_ref[...] = x_ref[...] + g_ref[pl.ds(i // tiles_per_batch, 1), :] * moe


def _combine(pos, y, x2, mod_l, g_chunk, route, seq):
    t, d = x2.shape
    tc = 128
    return pl.pallas_call(
        functools.partial(_combine_kernel, tc=tc, tiles_per_batch=seq // tc),
        grid_spec=pltpu.PrefetchScalarGridSpec(
            num_scalar_prefetch=1,
            grid=(t // tc,),
            in_specs=[pl.BlockSpec(memory_space=pl.ANY),
                      pl.BlockSpec((tc, d), lambda i, p: (i, 0)),
                      pl.BlockSpec((SUBLANES, d), lambda i, p: (0, g_chunk)),
                      pl.BlockSpec((tc, LANES), lambda i, p: (i, 0))],
            out_specs=pl.BlockSpec((tc, d), lambda i, p: (i, 0)),
            scratch_shapes=[pltpu.VMEM((tc, d), F32), pltpu.VMEM((tc, d), F32),
                            pltpu.SemaphoreType.DMA(())]),
        out_shape=jax.ShapeDtypeStruct((t, d), F32),
        compiler_params=_cparams(("arbitrary",), 2 * tc * d * 4 + 4 * tc * d * 4 + 4 * tc * d * 4),
        name="moe_combine",
    )(pos, y, x2, mod_l, route)


def _routing_tables(route, tm):
    t = route.shape[0]
    experts = route[:, :2].astype(jnp.int32).reshape(-1)
    onehot = (experts[:, None] == jnp.arange(N_EXPERTS)[None, :]).astype(jnp.int32)
    csum = jnp.cumsum(onehot, axis=0)
    rank = jnp.sum((csum - onehot) * onehot, axis=1)
    counts = csum[-1]
    padded = ((counts + tm - 1) // tm) * tm
    ends = jnp.cumsum(padded)
    starts = ends - padded
    dest = starts[experts] + rank
    n_rows = 2 * t + N_EXPERTS * tm
    row_token = jnp.zeros((n_rows,), jnp.int32).at[dest].set(jnp.arange(2 * t, dtype=jnp.int32) // 2)
    tile_start = jnp.arange(n_rows // tm, dtype=jnp.int32) * tm
    tile_expert = jnp.minimum(jnp.sum((tile_start[:, None] >= ends[None, :]).astype(jnp.int32), axis=1),
                              N_EXPERTS - 1)
    tile_valid = (tile_start < ends[-1]).astype(jnp.int32)
    real_end = (starts + counts)[tile_expert]
    tile_rows = jnp.clip(real_end - tile_start, 0, tm).astype(jnp.int32) * tile_valid
    return row_token, dest.astype(jnp.int32), tile_expert, tile_valid, tile_rows, n_rows


def _tiled(vec, reps):
    return jnp.tile(vec.astype(F32), reps)


def _mixer(h, seq, batch, layer, w_in_t, w_branch, w_out, rel_bias, cmp_pos, cmp_w1, cmp_w2,
           nsa_q_g, nsa_k_g, diff_q_g, diff_k_g, diff_lam, diff_out_g, lam_init, x2, mod_l, d):
    scale = HEAD_DIM ** -0.5
    ones = lambda n: jnp.ones((n,), F32)
    zeros = lambda n: jnp.zeros((n,), F32)
    proj = functools.partial(_matmul_w32, h, w_in_t, (layer,), tm=1024, tn=512, transposed=True)

    cs = jnp.concatenate([jnp.full((BRANCH_W,), scale, F32), ones(2 * BRANCH_W)]).reshape(1, -1)
    p_sb = proj(_C_SB, 3 * BRANCH_W, [(cs, "col")], _ep_scale, BF16, name="proj_sb")
    nf = ones(BRANCH_W).reshape(1, -1)
    cs = (_tiled(nsa_q_g, N_HEADS) * scale).reshape(1, -1)
    p_nsq = proj(_C_NSQ, BRANCH_W, [(nf, "col"), (cs, "col")],
                 functools.partial(_ep_norm, width=HEAD_DIM), BF16, name="proj_nsq")
    nf = jnp.concatenate([zeros(2 * KV_W), ones(KV_W), zeros(KV_W), ones(KV_W), zeros(KV_W)]).reshape(1, -1)
    cs = jnp.concatenate([ones(2 * KV_W), _tiled(nsa_k_g[1], KV_HEADS), ones(KV_W),
                          _tiled(nsa_k_g[2], KV_HEADS), ones(KV_W)]).reshape(1, -1)
    p_nskv = proj(_C_NSKV, 6 * KV_W, [(nf, "col"), (cs, "col")],
                  functools.partial(_ep_norm, width=HEAD_DIM), BF16, name="proj_nskv")
    ns_gates = proj(_C_GNS, LANES, [], _ep_sigmoid, F32, name="proj_nsgate")
    nf = jnp.concatenate([ones(2 * BRANCH_W), zeros(BRANCH_W)]).reshape(1, -1)
    cs = jnp.concatenate([_tiled(diff_q_g, 2 * N_HEADS) * DIFF_QK ** -0.5, _tiled(diff_k_g, 2 * N_HEADS),
                          ones(BRANCH_W)]).reshape(1, -1)
    p_df = proj(_C_DF, 3 * BRANCH_W, [(nf, "col"), (cs, "col")],
                functools.partial(_ep_norm, width=DIFF_QK), BF16, name="proj_diff")
    m_gates = proj(_C_MERGE, 3 * d, [], _ep_sigmoid, BF16, name="proj_merge_gate")

    o_sb = _sb_attention(p_sb, batch, seq)
    kvcm = _compress(p_nskv, cmp_pos, cmp_w1, cmp_w2, nsa_k_g[0], batch, seq)
    o_ns = _nsa_attention(p_nsq, kvcm, p_nskv, ns_gates, rel_bias[:, :N_HEADS], nsa_q_g, nsa_k_g, batch, seq)
    o_df = _diff_attention(p_df, rel_bias[:, N_HEADS:], diff_lam, diff_out_g, diff_q_g, diff_k_g, lam_init,
                           batch, seq)
    merged = _branch_merge(o_sb, o_ns, o_df, w_branch, layer, m_gates)
    g1 = lax.slice_in_dim(mod_l, 2 * d, 3 * d, axis=1)
    return _matmul_w32(merged, w_out, (layer,), 0, d, [(x2, "tile"), (g1, "batch")], _ep_residual, F32,
                       tm=1024, tn=512, name="proj_out", seq=seq)


def _moe(x2, seq, mod_l, norm_gain, w_router, router_bias, layer, w_gate, w_up, w_down):
    tm = 256
    h2, route = _norm_router(x2, norm_gain, mod_l, 3, 4, seq, w_router, router_bias)
    row_token, pos, tile_expert, tile_valid, tile_rows, n_rows = _routing_tables(route, tm)
    xs = _gather_rows(row_token, tile_rows, h2, n_rows, tm)
    y = _expert_mlps(tile_expert, tile_valid, xs, layer, w_gate, w_up, w_down, tm)
    return _combine(pos, y, x2, mod_l, 5, route, seq)


def kernel(x, c, rel_bias, w_router, router_bias, w_ada, b_ada, norm_mix, norm_ffn, w_in, nsa_cmp_pos,
           nsa_cmp_w1, nsa_cmp_w2, nsa_q_norm, nsa_k_norm, diff_q_norm, diff_k_norm, diff_lambda,
           diff_out_norm, w_branch, w_out, w_exp_gate, w_exp_up, w_exp_down):
    batch, seq, d = x.shape
    depth = w_ada.shape[0]
    mod = _adaln(c, w_ada, b_ada)
    x2 = x.reshape(batch * seq, d)
    w_in_t = jnp.swapaxes(w_in, 1, 2)
    for layer in range(depth):
        lam_init = 0.8 - 0.6 * math.exp(-0.3 * layer)
        mod_l = mod[layer]
        h = _norm_mod(x2, norm_mix[layer], mod_l, 0, 1, seq)
        x2 = _mixer(h, seq, batch, layer, w_in_t, w_branch, w_out, rel_bias,
                    nsa_cmp_pos[layer], nsa_cmp_w1[layer], nsa_cmp_w2[layer], nsa_q_norm[layer],
                    nsa_k_norm[layer], diff_q_norm[layer], diff_k_norm[layer], diff_lambda[layer],
                    diff_out_norm[layer], lam_init, x2, mod_l, d)
        x2 = _moe(x2, seq, mod_l, norm_ffn[layer], w_router, router_bias, layer, w_exp_gate,
                  w_exp_up, w_exp_down)
    return x2.reshape(batch, seq, d)
```

```python
import functools
import math

import numpy as np
import jax
import jax.numpy as jnp
from jax import lax
from jax.experimental import pallas as pl
from jax.experimental.pallas import tpu as pltpu

F32 = jnp.float32
BF16 = jnp.bfloat16

HEAD_DIM = 128
N_HEADS = 8
KV_HEADS = 2
Q_PER_KV = N_HEADS // KV_HEADS
DIFF_QK = HEAD_DIM // 2
BRANCH_W = N_HEADS * HEAD_DIM
KV_W = KV_HEADS * HEAD_DIM
CMP_LEN = 32
CMP_STRIDE = 16
SLC_LEN = 64
N_SLC = 16
N_LOCAL_SLC = 2
WINDOW = 512
REL_BUCKETS = 32
REL_MAX_DIST = 128
N_EXPERTS = 16
N_GROUPS = 4
EXP_PER_GROUP = N_EXPERTS // N_GROUPS
NORM_EPS = 1e-6
NEG_INF = -1e30
MASKED_BELOW = -1e29
FORCE_SCORE = 1e9
LANES = 128
SUBLANES = 8
VMEM_CAP = 56 * 1024 * 1024
SB_EXIT = 100.0

_C_SB = 0
_C_NSQ = 3 * BRANCH_W
_C_NSKV = _C_NSQ + BRANCH_W
_C_GNS = _C_NSKV + 6 * KV_W
_C_DF = _C_GNS + 3 * N_HEADS
_C_MERGE = _C_DF + 3 * BRANCH_W


def _cparams(sem, vmem_bytes):
    lim = int(min(VMEM_CAP, max(vmem_bytes * 5 // 4 + (2 << 20), 16 << 20)))
    return pltpu.CompilerParams(dimension_semantics=sem, vmem_limit_bytes=lim)


def _dot(a, b):
    return jnp.dot(a, b, preferred_element_type=F32)


def _dot_nt(a, b):
    return lax.dot_general(a, b, (((1,), (1,)), ((), ())), preferred_element_type=F32)


def _split_bf16(x):
    hi = x.astype(BF16)
    lo = (x - hi.astype(F32)).astype(BF16)
    return hi, lo


def _ada_kernel(cb_ref, w_ref, b_ref, o_ref, ca_sc, *, nb, rows_per_iter):
    k_dim, tn = w_ref.shape

    @pl.when((pl.program_id(0) == 0) & (pl.program_id(1) == 0))
    def _():
        cb = cb_ref[...]
        ca_sc[...] = cb * jax.nn.sigmoid(cb)

    reps = tn // LANES
    groups = rows_per_iter // SUBLANES

    def body(i, accs):
        r0 = pl.multiple_of(i * rows_per_iter, rows_per_iter)
        w3 = w_ref[pl.ds(r0, rows_per_iter), :].reshape(groups, SUBLANES, tn)
        new = []
        for b in range(nb):
            ca = ca_sc[b, pl.ds(r0, rows_per_iter), :].reshape(groups, SUBLANES, LANES)
            ca = jnp.concatenate([ca] * reps, axis=2)
            new.append(accs[b] + jnp.sum(w3 * ca, axis=0))
        return tuple(new)

    accs = lax.fori_loop(0, k_dim // rows_per_iter, body,
                         tuple(jnp.zeros((SUBLANES, tn), F32) for _ in range(nb)))
    rows = [jnp.sum(a, axis=0, keepdims=True) + b_ref[...] for a in accs]
    rows.append(jnp.zeros((SUBLANES - nb, tn), F32))
    o_ref[...] = jnp.concatenate(rows, axis=0)


def _adaln(c, w_ada, b_ada):
    depth, d, n = w_ada.shape
    nb = c.shape[0]
    tn = 512
    cb = jnp.broadcast_to(c[:, :, None], (nb, d, LANES))
    vmem = 2 * d * tn * 4 + 3 * nb * d * LANES * 4
    return pl.pallas_call(
        functools.partial(_ada_kernel, nb=nb, rows_per_iter=64),
        grid=(depth, n // tn),
        in_specs=[pl.BlockSpec((nb, d, LANES), lambda l, j: (0, 0, 0)),
                  pl.BlockSpec((None, d, tn), lambda l, j: (l, 0, j)),
                  pl.BlockSpec((None, 1, tn), lambda l, j: (l, 0, j))],
        out_specs=pl.BlockSpec((None, SUBLANES, tn), lambda l, j: (l, 0, j)),
        out_shape=jax.ShapeDtypeStruct((depth, SUBLANES, n), F32),
        scratch_shapes=[pltpu.VMEM((nb, d, LANES), F32)],
        compiler_params=_cparams(("arbitrary", "arbitrary"), vmem),
        name="adaln_mod",
    )(cb, w_ada, b_ada.reshape(depth, 1, n))


def _modulated_norm(x_ref, gain_ref, sc_ref, sh_ref, tiles_per_batch):
    b = pl.program_id(0) // tiles_per_batch
    x = x_ref[...]
    ms = jnp.mean(x * x, axis=-1, keepdims=True)
    y = x * lax.rsqrt(ms + NORM_EPS) * gain_ref[...]
    return y * (1.0 + sc_ref[pl.ds(b, 1), :]) + sh_ref[pl.ds(b, 1), :]


def _norm_kernel(x_ref, gain_ref, sc_ref, sh_ref, o_ref, *, tiles_per_batch):
    o_ref[...] = _modulated_norm(x_ref, gain_ref, sc_ref, sh_ref, tiles_per_batch).astype(o_ref.dtype)


def _norm_mod(x2, gain, mod_l, sh_chunk, sc_chunk, seq):
    t, d = x2.shape
    tm = 256
    return pl.pallas_call(
        functools.partial(_norm_kernel, tiles_per_batch=seq // tm),
        grid=(t // tm,),
        in_specs=[pl.BlockSpec((tm, d), lambda i: (i, 0)),
                  pl.BlockSpec((1, d), lambda i: (0, 0)),
                  pl.BlockSpec((SUBLANES, d), lambda i: (0, sc_chunk)),
                  pl.BlockSpec((SUBLANES, d), lambda i: (0, sh_chunk))],
        out_specs=pl.BlockSpec((tm, d), lambda i: (i, 0)),
        out_shape=jax.ShapeDtypeStruct((t, d), BF16),
        compiler_params=_cparams(("arbitrary",), 2 * tm * d * 6 + 4 * SUBLANES * d * 4),
        name="norm_mix",
    )(x2, gain.reshape(1, d), mod_l, mod_l)


def _norm_router_kernel(x_ref, gain_ref, sc_ref, sh_ref, wr_ref, rb_ref, h_ref, route_ref,
                        *, tiles_per_batch):
    h = _modulated_norm(x_ref, gain_ref, sc_ref, sh_ref, tiles_per_batch)
    h_hi, h_lo = _split_bf16(h)
    bits = pltpu.bitcast(h_hi.astype(F32), jnp.uint32)
    half = bits.shape[1] // 2
    h_ref[...] = (bits[:, half:] & jnp.uint32(0xFFFF0000)) | lax.shift_right_logical(bits[:, :half],
                                                                                     jnp.uint32(16))
    w_hi, w_lo = _split_bf16(wr_ref[...])
    logits = _dot(h_hi, w_hi) + _dot(h_hi, w_lo) + _dot(h_lo, w_hi)
    aff = jax.nn.sigmoid(logits)
    biased = aff + rb_ref[...]
    lane_i = lax.broadcasted_iota(jnp.int32, aff.shape, 1)
    lane = lane_i.astype(F32)
    low = jnp.float32(-3e38)
    big = jnp.float32(1 << 20)
    best = None
    for gi in range(N_GROUPS):
        in_group = (lane_i // EXP_PER_GROUP) == gi
        v = jnp.where(in_group, biased, low)
        m1 = jnp.max(v, axis=-1, keepdims=True)
        i1 = jnp.min(jnp.where(v == m1, lane, big), axis=-1, keepdims=True)
        v2 = jnp.where(lane == i1, low, v)
        m2 = jnp.max(v2, axis=-1, keepdims=True)
        i2 = jnp.min(jnp.where(v2 == m2, lane, big), axis=-1, keepdims=True)
        score = m1 + m2
        if best is None:
            best = (score, i1, i2)
        else:
            better = score > best[0]
            best = (jnp.where(better, score, best[0]), jnp.where(better, i1, best[1]),
                    jnp.where(better, i2, best[2]))
    _, e1, e2 = best
    a1 = jnp.sum(jnp.where(lane == e1, aff, 0.0), axis=-1, keepdims=True)
    a2 = jnp.sum(jnp.where(lane == e2, aff, 0.0), axis=-1, keepdims=True)
    den = a1 + a2
    out = jnp.where(lane_i == 0, e1,
                    jnp.where(lane_i == 1, e2,
                              jnp.where(lane_i == 2, a1 / den, jnp.where(lane_i == 3, a2 / den, 0.0))))
    route_ref[...] = out


def _norm_router(x2, gain, mod_l, sh_chunk, sc_chunk, seq, w_router, router_bias):
    t, d = x2.shape
    tm = 256
    wr = jnp.pad(w_router, ((0, 0), (0, LANES - N_EXPERTS)))
    rb = jnp.pad(router_bias, (0, LANES - N_EXPERTS)).reshape(1, LANES)
    return pl.pallas_call(
        functools.partial(_norm_router_kernel, tiles_per_batch=seq // tm),
        grid=(t // tm,),
        in_specs=[pl.BlockSpec((tm, d), lambda i: (i, 0)),
                  pl.BlockSpec((1, d), lambda i: (0, 0)),
                  pl.BlockSpec((SUBLANES, d), lambda i: (0, sc_chunk)),
                  pl.BlockSpec((SUBLANES, d), lambda i: (0, sh_chunk)),
                  pl.BlockSpec((d, LANES), lambda i: (0, 0)),
                  pl.BlockSpec((1, LANES), lambda i: (0, 0))],
        out_specs=[pl.BlockSpec((tm, d // 2), lambda i: (i, 0)),
                   pl.BlockSpec((tm, LANES), lambda i: (i, 0))],
        out_shape=[jax.ShapeDtypeStruct((t, d // 2), jnp.uint32),
                   jax.ShapeDtypeStruct((t, LANES), F32)],
        compiler_params=_cparams(("arbitrary",), 2 * tm * d * 8 + 4 * d * LANES * 4 + 8 * tm * d),
        name="norm_ffn_router",
    )(x2, gain.reshape(1, d), mod_l, mod_l, wr, rb)


def _mm_w32_kernel(a_ref, w_ref, *rest, epilogue, n_extra, tiles_per_batch, transposed):
    extras = rest[:n_extra]
    o_ref, w_sc = rest[n_extra], rest[n_extra + 1]

    @pl.when(pl.program_id(1) == 0)
    def _():
        w_sc[...] = w_ref[...].reshape(w_sc.shape).astype(BF16)

    acc = _dot_nt(a_ref[...], w_sc[...]) if transposed else _dot(a_ref[...], w_sc[...])
    o_ref[...] = epilogue(acc, extras, pl.program_id(1) // tiles_per_batch).astype(o_ref.dtype)


def _matmul_w32(a, w, lead, col_lo, n, extras, epilogue, out_dtype, *, tm, tn, name, seq=None,
                transposed=False):
    m, k = a.shape
    tm = min(tm, m)
    tn = min(tn, n)
    assert m % tm == 0 and n % tn == 0
    none = (None,) * len(lead)
    if transposed:
        assert col_lo % SUBLANES == 0
        w_spec = pl.BlockSpec(tuple(pl.Element(1) for _ in lead) + (pl.Element(tn), pl.Element(k)),
                              lambda j, i: lead + (pl.multiple_of(col_lo + j * tn, SUBLANES), 0))
    else:
        assert col_lo % tn == 0
        w_spec = pl.BlockSpec(none + (k, tn), lambda j, i: lead + (0, col_lo // tn + j))
    in_specs = [pl.BlockSpec((tm, k), lambda j, i: (i, 0)), w_spec]
    args = [a, w]
    vmem = 2 * tm * k * 2 + 2 * k * tn * 4 + k * tn * 2 + 2 * tm * tn * jnp.dtype(out_dtype).itemsize + 2 * tm * tn * 4
    for arr, kind in extras:
        if kind == "tile":
            in_specs.append(pl.BlockSpec((tm, tn), lambda j, i: (i, j)))
            vmem += 2 * tm * tn * arr.dtype.itemsize
        else:
            rows = 1 if kind == "col" else SUBLANES
            in_specs.append(pl.BlockSpec((rows, tn), lambda j, i: (0, j)))
            vmem += 2 * SUBLANES * tn * 4
        args.append(arr)
    tiles_per_batch = (seq // tm) if seq else 1
    return pl.pallas_call(
        functools.partial(_mm_w32_kernel, epilogue=epilogue, n_extra=len(extras),
                          tiles_per_batch=tiles_per_batch, transposed=transposed),
        grid=(n // tn, m // tm),
        in_specs=in_specs,
        out_specs=pl.BlockSpec((tm, tn), lambda j, i: (i, j)),
        out_shape=jax.ShapeDtypeStruct((m, n), out_dtype),
        scratch_shapes=[pltpu.VMEM((tn, k) if transposed else (k, tn), BF16)],
        compiler_params=_cparams(("arbitrary", "arbitrary"), vmem),
        name=name,
    )(*args)


def _ep_scale(acc, extras, b):
    return acc * extras[0][...]


def _ep_sigmoid(acc, extras, b):
    return 0.5 * jnp.tanh(0.5 * acc) + 0.5


def _ep_norm(acc, extras, b, *, width):
    nf_ref, cs_ref = extras
    outs = []
    for c in range(acc.shape[1] // LANES):
        sl = slice(c * LANES, (c + 1) * LANES)
        blk = acc[:, sl]
        sq = blk * blk
        if width == LANES:
            ms = jnp.mean(sq, axis=-1, keepdims=True)
        else:
            lane = lax.broadcasted_iota(jnp.int32, blk.shape, 1)
            first = lane < width
            lo = jnp.sum(jnp.where(first, sq, 0.0), axis=-1, keepdims=True) / width
            hi = jnp.sum(jnp.where(first, 0.0, sq), axis=-1, keepdims=True) / width
            ms = jnp.where(first, lo, hi)
        inv = lax.rsqrt(ms + NORM_EPS)
        f = jnp.where(nf_ref[:, sl] > 0.0, inv, 1.0)
        outs.append(blk * f * cs_ref[:, sl])
    return jnp.concatenate(outs, axis=1)


def _ep_residual(acc, extras, b):
    x_ref, g_ref = extras
    return x_ref[...] + g_ref[pl.ds(b, 1), :] * acc


def _bucket_np(dist):
    n = np.maximum(dist, 0)
    max_exact = REL_BUCKETS // 2
    nf = np.maximum(n, max_exact).astype(np.float32)
    large = max_exact + (np.log(nf / np.float32(max_exact))
                         / np.float32(math.log(REL_MAX_DIST / max_exact))
                         * np.float32(REL_BUCKETS - max_exact)).astype(np.int32)
    large = np.minimum(large, REL_BUCKETS - 1)
    return np.where(n < max_exact, n, large).astype(np.int32)


def _bucket_starts():
    b = _bucket_np(np.arange(4 * REL_MAX_DIST))
    return [int(np.argmax(b >= k)) for k in range(1, REL_BUCKETS)]


def _delta_bias(table, dist, valid):
    tb = table.T.astype(F32)
    shape = (tb.shape[0],) + (1,) * dist.ndim
    val = jnp.broadcast_to(tb[:, 0].reshape(shape), (tb.shape[0],) + dist.shape)
    for k, start in enumerate(_bucket_starts(), start=1):
        val = jnp.where(dist[None] >= start, tb[:, k].reshape(shape), val)
    val = val - tb[:, REL_BUCKETS - 1].reshape(shape)
    return jnp.where(valid[None], val, NEG_INF)


def _toeplitz_tiles(table, offs, nr, nc, lo, hi):
    dmin = -(nc - 1)
    d = jnp.arange(dmin, max(offs) + nr, dtype=jnp.int32)
    g = _delta_bias(table, d, (d >= lo) & (d < hi))
    nh = g.shape[0]
    w = nr + nc - 1
    tiles = []
    for off in offs:
        base = off - dmin
        ahead = g[:, base - nc + 1:base + 1][:, ::-1]
        behind = g[:, base + 1:base + nr][:, ::-1]
        ring = jnp.concatenate([ahead, behind], axis=1)
        flat = jnp.tile(ring, (1, nr))[:, :nr * (w - 1)]
        tiles.append(flat.reshape(nh, nr, w - 1)[:, :, :nc])
    return jnp.stack(tiles, axis=1)


def _stack_group_heads(tiles):
    nh, k, nr, nc = tiles.shape
    t = tiles.reshape(KV_HEADS, Q_PER_KV, k, nr, nc)
    return jnp.transpose(t, (0, 2, 1, 3, 4)).reshape(KV_HEADS, k, Q_PER_KV * nr, nc)


SB_HEADS_PER_STEP = 4


def _sb_kernel(q_ref, k_ref, v_ref, o_ref, *, tq):
    qi = pl.program_id(2)
    heads = [slice(h * HEAD_DIM, (h + 1) * HEAD_DIM) for h in range(SB_HEADS_PER_STEP)]
    qs = [q_ref[:, hs] for hs in heads]
    row = lax.broadcasted_iota(jnp.int32, (tq, tq), 0)
    col = lax.broadcasted_iota(jnp.int32, (tq, tq), 1)
    later = jnp.where(row > col, 1.0, 0.0).astype(BF16)
    causal = col < row

    def step(kj, state, diag):
        k0 = pl.multiple_of(kj * tq, tq)
        out = []
        for q, hs, (carry, acc) in zip(qs, heads, state):
            z = _dot_nt(q, k_ref[pl.ds(k0, tq), hs])
            log_succ = jnp.minimum(z, 0.0) - jnp.log1p(jnp.exp(-jnp.abs(z)))
            log_fail = log_succ - z
            if diag:
                log_fail = jnp.where(causal, log_fail, 0.0)
            hi, lo = _split_bf16(log_fail)
            suffix = _dot(hi, later) + _dot(lo, later)
            a = jnp.exp(log_succ + suffix + carry)
            if diag:
                a = jnp.where(causal, a, 0.0)
            acc = acc + _dot(a.astype(BF16), v_ref[pl.ds(k0, tq), hs])
            carry = carry + jnp.sum(log_fail, axis=-1, keepdims=True)
            out.append((carry, acc))
        return tuple(out)

    zero = (jnp.zeros((tq, 1), F32), jnp.zeros((tq, HEAD_DIM), F32))
    state = step(qi, (zero,) * SB_HEADS_PER_STEP, True)

    def cond(loop):
        kj, state = loop
        live = functools.reduce(jnp.maximum, [jnp.max(carry) for carry, _ in state])
        return (kj >= 0) & (live > -SB_EXIT)

    def body(loop):
        kj, state = loop
        return kj - 1, step(kj, state, False)

    _, state = lax.while_loop(cond, body, (qi - 1, state))
    for hs, (_, acc) in zip(heads, state):
        o_ref[:, hs] = acc.astype(o_ref.dtype)


def _sb_attention(p_sb, batch, seq):
    tq = min(256, seq)
    nq = seq // tq
    t = batch * seq
    hp = SB_HEADS_PER_STEP
    w = hp * HEAD_DIM
    groups = N_HEADS // hp
    vmem = 4 * seq * w * 2 + 4 * tq * w * 2 + 16 * hp * tq * tq * 4
    return pl.pallas_call(
        functools.partial(_sb_kernel, tq=tq),
        grid=(batch, groups, nq),
        in_specs=[pl.BlockSpec((tq, w), lambda b, h, i: (b * nq + i, h)),
                  pl.BlockSpec((seq, w), lambda b, h, i: (b, groups + h)),
                  pl.BlockSpec((seq, w), lambda b, h, i: (b, 2 * groups + h))],
        out_specs=pl.BlockSpec((tq, w), lambda b, h, i: (b * nq + i, h)),
        out_shape=jax.ShapeDtypeStruct((t, BRANCH_W), BF16),
        compiler_params=_cparams(("arbitrary",) * 3, vmem),
        name="sb_attention",
    )(p_sb, p_sb, p_sb)


LOGIT_CAP = 40.0


def _with_ones(v):
    return jnp.concatenate([v, jnp.ones_like(v)], axis=1)


def _softmax_first(s, v, m_sc, acc_sc, bounded):
    if bounded:
        acc_sc[...] = _dot(jnp.exp(s).astype(BF16), _with_ones(v))
        return
    m = jnp.max(s, axis=-1, keepdims=True)
    m_sc[...] = m
    acc_sc[...] = _dot(jnp.exp(s - m).astype(BF16), _with_ones(v))


def _softmax_next(s, v, m_sc, acc_sc, bounded):
    if bounded:
        acc_sc[...] = acc_sc[...] + _dot(jnp.exp(s).astype(BF16), _with_ones(v))
        return
    m_old = m_sc[...]
    m_new = jnp.maximum(m_old, jnp.max(s, axis=-1, keepdims=True))
    p = jnp.exp(s - m_new)
    acc_sc[...] = jnp.exp(m_old - m_new) * acc_sc[...] + _dot(p.astype(BF16), _with_ones(v))
    m_sc[...] = m_new


def _softmax_result(acc_sc, bounded):
    acc = acc_sc[...]
    total = acc[:, HEAD_DIM:]
    den = jnp.where(total > 0.0, total, 1.0) if bounded else jnp.maximum(total, 1.0)
    return acc[:, :HEAD_DIM] / den


def _logit_bound_ok(width, scale, q_gain, k_gains, table):
    qk = width * scale * jnp.max(jnp.abs(q_gain)) * jnp.max(jnp.abs(k_gains)) * 1.02
    delta = jnp.max(jnp.abs(table - table[REL_BUCKETS - 1:]))
    return (qk + delta <= LOGIT_CAP).astype(jnp.int32).reshape(1)


def _compress_kernel(x_ref, p_ref, w1a_ref, w1b_ref, w2_ref, gain_ref, o_ref):
    nc = x_ref.shape[0]
    is_key = pl.program_id(1) < KV_HEADS
    x = x_ref[...].astype(F32)
    first = _dot((x + p_ref[0:1, :]).astype(BF16), w1a_ref[...])
    second = _dot((x + p_ref[1:2, :]).astype(BF16), w1b_ref[...])
    pre = first + pltpu.roll(second, nc - 1, 0)
    hid = pre * jax.nn.sigmoid(pre)
    out = _dot(hid.astype(BF16), w2_ref[...])
    ms = jnp.mean(out * out, axis=-1, keepdims=True)
    normed = out * lax.rsqrt(ms + NORM_EPS) * gain_ref[...]
    o_ref[...] = jnp.where(is_key, normed, out).astype(o_ref.dtype)


def _compress(p_nskv, cmp_pos, cmp_w1, cmp_w2, k_gain0, batch, seq):
    nc = seq // CMP_STRIDE
    half = CMP_STRIDE * HEAD_DIM
    x = p_nskv[:, :2 * KV_W].reshape(batch, nc, CMP_STRIDE, 2 * KV_HEADS, HEAD_DIM)
    x = jnp.transpose(x, (0, 3, 1, 2, 4)).reshape(batch, 2 * KV_HEADS, nc, half)
    pos = cmp_pos.reshape(2, 2, half)
    w1 = cmp_w1.astype(BF16).reshape(2, 2, half, HEAD_DIM)
    w2 = cmp_w2.astype(BF16)
    return pl.pallas_call(
        _compress_kernel,
        grid=(batch, 2 * KV_HEADS),
        in_specs=[pl.BlockSpec((None, None, nc, half), lambda b, j: (b, j, 0, 0)),
                  pl.BlockSpec((None, 2, half), lambda b, j: (j // KV_HEADS, 0, 0)),
                  pl.BlockSpec((None, None, half, HEAD_DIM), lambda b, j: (j // KV_HEADS, 0, 0, 0)),
                  pl.BlockSpec((None, None, half, HEAD_DIM), lambda b, j: (j // KV_HEADS, 1, 0, 0)),
                  pl.BlockSpec((None, HEAD_DIM, HEAD_DIM), lambda b, j: (j // KV_HEADS, 0, 0)),
                  pl.BlockSpec((1, HEAD_DIM), lambda b, j: (0, 0))],
        out_specs=pl.BlockSpec((None, None, nc, HEAD_DIM), lambda b, j: (b, j, 0, 0)),
        out_shape=jax.ShapeDtypeStruct((batch, 2 * KV_HEADS, nc, HEAD_DIM), BF16),
        compiler_params=_cparams(("arbitrary", "arbitrary"), 8 * nc * half * 2 + 8 * half * HEAD_DIM * 2),
        name="nsa_compress",
    )(x, pos, w1, w1, w2, k_gain0.reshape(1, HEAD_DIM))


def _nsa_kernel(ok_ref, q_ref, kcm_ref, vcm_ref, ks_ref, vs_ref, kw_ref, vw_ref, gate_ref,
                pt_ref, bt_ref, wt_ref, o_ref, m_sc, acc_sc, *, tq, tk, tw, n_blk, k_sel):
    qi = pl.program_id(2)
    q0 = qi * tq
    rows = Q_PER_KV * tq
    nc = kcm_ref.shape[0]
    q4 = jnp.concatenate([q_ref[:, r * HEAD_DIM:(r + 1) * HEAD_DIM] for r in range(Q_PER_KV)], axis=0)

    per_tile = tq // CMP_STRIDE
    lead = 2 * per_tile
    shift = lax.rem(qi * per_tile + (nc - lead), nc)
    cb = pltpu.roll(pt_ref[...], shift, 1)
    rel = lax.broadcasted_iota(jnp.int32, (rows, nc), 1) - qi * per_tile
    cb = jnp.where(rel < -lead, 0.0, jnp.where(rel >= per_tile, NEG_INF, cb))
    s = _dot_nt(q4, kcm_ref[...]) + cb
    valid = s > MASKED_BELOW
    s = jnp.where(valid, s, NEG_INF)
    p = jnp.where(valid, jnp.exp(s - jnp.max(s, axis=-1, keepdims=True)), 0.0)
    p = p / jnp.maximum(jnp.sum(p, axis=-1, keepdims=True), 1.0)
    o_cmp = _dot(p.astype(BF16), vcm_ref[...])

    psum = p[0:tq]
    for r in range(1, Q_PER_KV):
        psum = psum + p[r * tq:(r + 1) * tq]
    n_id = lax.broadcasted_iota(jnp.int32, (nc, LANES), 0)
    j_id = lax.broadcasted_iota(jnp.int32, (nc, LANES), 1)
    overlap = ((CMP_STRIDE * n_id < SLC_LEN * j_id + SLC_LEN)
               & (CMP_STRIDE * n_id + CMP_LEN > SLC_LEN * j_id) & (j_id < n_blk))
    overlap = jnp.where(overlap, 1.0, 0.0).astype(BF16)
    p_hi, p_lo = _split_bf16(psum)
    imp = _dot(p_hi, overlap) + _dot(p_lo, overlap)
    nr = -(-n_blk // SUBLANES) * SUBLANES
    imp_t = jnp.transpose(imp)[:nr]
    blk = lax.broadcasted_iota(jnp.int32, (nr, tq), 0)
    cur = (q0 + lax.broadcasted_iota(jnp.int32, (nr, tq), 1)) // SLC_LEN
    forced = (blk == 0) | ((blk <= cur) & (blk > cur - N_LOCAL_SLC))
    imp_t = jnp.where(forced, FORCE_SCORE, jnp.where(blk > cur, NEG_INF, imp_t))
    imp_t = jnp.where(blk < n_blk, imp_t, -3e38)
    rank = jnp.zeros((nr, tq), F32)
    for j in range(n_blk):
        cj = imp_t[j:j + 1, :]
        ahead = jnp.where(blk > j, jnp.where(cj >= imp_t, 1.0, 0.0), jnp.where(cj > imp_t, 1.0, 0.0))
        rank = rank + ahead
    sel_t = jnp.where((rank < k_sel) & (blk < n_blk), 1.0, 0.0)
    if nr < LANES:
        sel_t = jnp.concatenate([sel_t, jnp.zeros((LANES - nr, tq), F32)], axis=0)
    sel = jnp.transpose(sel_t).astype(BF16)

    per = tk // tq
    kd = qi // per
    par = qi - kd * per
    e_row = lax.broadcasted_iota(jnp.int32, (LANES, tk), 0)
    e_col = lax.broadcasted_iota(jnp.int32, (LANES, tk), 1)

    def slc_scores(kj, bias):
        k0 = pl.multiple_of(kj * tk, tk)
        expand = jnp.where((k0 + e_col) // SLC_LEN == e_row, 1.0, 0.0).astype(BF16)
        chosen = _dot(sel, expand)
        s = _dot_nt(q4, ks_ref[pl.ds(k0, tk), :])
        if bias is not None:
            s = s + bias
        s = jnp.where(chosen[None] > 0.5, s.reshape(Q_PER_KV, tq, tk), NEG_INF).reshape(rows, tk)
        return s, vs_ref[pl.ds(k0, tk), :]

    def attend(bounded):
        s, v = slc_scores(kd, bt_ref[par])
        _softmax_first(s, v, m_sc, acc_sc, bounded)

        @pl.when(kd >= 1)
        def _():
            s, v = slc_scores(kd - 1, bt_ref[per + par])
            _softmax_next(s, v, m_sc, acc_sc, bounded)

        def slc_far(i, _):
            s, v = slc_scores(kd - 2 - i, None)
            _softmax_next(s, v, m_sc, acc_sc, bounded)
            return 0

        lax.fori_loop(0, jnp.maximum(kd - 1, 0), slc_far, 0)
        o_slc = _softmax_result(acc_sc, bounded)

        start = pl.multiple_of(jnp.maximum(q0 - WINDOW, 0), tq)
        s = _dot_nt(q4, kw_ref[pl.ds(start, tw), :]) + wt_ref[jnp.minimum(qi, WINDOW // tq)]
        _softmax_first(s, vw_ref[pl.ds(start, tw), :], m_sc, acc_sc, bounded)
        o_win = _softmax_result(acc_sc, bounded)

        gate = gate_ref[...]
        first_group = pl.program_id(1) == 0

        def gate_col(branch, r):
            lo = branch * N_HEADS + r
            hi = lo + Q_PER_KV
            return jnp.where(first_group, gate[:, lo:lo + 1], gate[:, hi:hi + 1])

        for r in range(Q_PER_KV):
            sl = slice(r * tq, (r + 1) * tq)
            o = gate_col(0, r) * o_cmp[sl] + gate_col(1, r) * o_slc[sl] + gate_col(2, r) * o_win[sl]
            o_ref[:, r * HEAD_DIM:(r + 1) * HEAD_DIM] = o.astype(o_ref.dtype)

    @pl.when(ok_ref[0] > 0)
    def _():
        attend(True)

    @pl.when(ok_ref[0] == 0)
    def _():
        attend(False)


def _nsa_attention(p_nsq, kvcm, p_nskv, gates, rel_nsa, q_gain, k_gains, batch, seq):
    tq = 128
    tk = min(512, seq)
    tw = WINDOW + tq
    nq = seq // tq
    nc = seq // CMP_STRIDE
    n_blk = seq // SLC_LEN
    k_sel = min(N_SLC, n_blk)
    t = batch * seq
    rows = Q_PER_KV * tq
    per = tk // tq
    assert n_blk <= LANES and nc % LANES == 0 and WINDOW % tq == 0 and tw <= seq and tq == LANES

    far = 1 << 30
    offs = [i * tq for i in range(2 * per)]
    bt = _stack_group_heads(_toeplitz_tiles(rel_nsa, offs, tq, tk, 0, far))
    offs = [i * tq for i in range(WINDOW // tq + 1)]
    wt = _stack_group_heads(_toeplitz_tiles(rel_nsa, offs, tq, tw, 0, WINDOW))
    per_tile = tq // CMP_STRIDE
    d = (lax.broadcasted_iota(jnp.int32, (1, tq, nc), 1)
         - CMP_STRIDE * (lax.broadcasted_iota(jnp.int32, (1, tq, nc), 2) - 2 * per_tile) - (CMP_LEN - 1))
    pt = _stack_group_heads(_delta_bias(rel_nsa, d, d >= 0))[:, 0]

    ok = _logit_bound_ok(HEAD_DIM, HEAD_DIM ** -0.5, q_gain, k_gains[1:], rel_nsa)
    const = lambda shape: pl.BlockSpec((None,) + shape, lambda b, g, i, ok: (g,) + (0,) * len(shape),
                                       pipeline_mode=pl.Buffered(1))
    vmem = (8 * seq * HEAD_DIM * 2 + (bt.size + wt.size + pt.size) // KV_HEADS * 4
            + 10 * rows * tw * 4 + 8 * rows * nc * 4)
    return pl.pallas_call(
        functools.partial(_nsa_kernel, tq=tq, tk=tk, tw=tw, n_blk=n_blk, k_sel=k_sel),
        grid_spec=pltpu.PrefetchScalarGridSpec(
            num_scalar_prefetch=1,
            grid=(batch, KV_HEADS, nq),
            in_specs=[pl.BlockSpec((tq, Q_PER_KV * HEAD_DIM), lambda b, g, i, ok: (b * nq + i, g)),
                      pl.BlockSpec((None, None, nc, HEAD_DIM), lambda b, g, i, ok: (b, g, 0, 0)),
                      pl.BlockSpec((None, None, nc, HEAD_DIM), lambda b, g, i, ok: (b, KV_HEADS + g, 0, 0)),
                      pl.BlockSpec((seq, HEAD_DIM), lambda b, g, i, ok: (b, 2 * KV_HEADS + g)),
                      pl.BlockSpec((seq, HEAD_DIM), lambda b, g, i, ok: (b, 3 * KV_HEADS + g)),
                      pl.BlockSpec((seq, HEAD_DIM), lambda b, g, i, ok: (b, 4 * KV_HEADS + g)),
                      pl.BlockSpec((seq, HEAD_DIM), lambda b, g, i, ok: (b, 5 * KV_HEADS + g)),
                      pl.BlockSpec((tq, LANES), lambda b, g, i, ok: (b * nq + i, 0)),
                      const(pt.shape[1:]), const(bt.shape[1:]), const(wt.shape[1:])],
            out_specs=pl.BlockSpec((tq, Q_PER_KV * HEAD_DIM), lambda b, g, i, ok: (b * nq + i, g)),
            scratch_shapes=[pltpu.VMEM((rows, 1), F32), pltpu.VMEM((rows, 2 * HEAD_DIM), F32)]),
        out_shape=jax.ShapeDtypeStruct((t, BRANCH_W), BF16),
        compiler_params=_cparams(("arbitrary",) * 3, vmem),
        name="nsa_attention",
    )(ok, p_nsq, kvcm, kvcm, p_nskv, p_nskv, p_nskv, p_nskv, gates, pt, bt, wt)


def _diff_kernel(ok_ref, q_ref, k_ref, v_ref, dt_ref, lam_ref, og_ref, o_ref, m_sc, acc_sc,
                 *, tq, tk, lam_init):
    qi = pl.program_id(2)
    per = tk // tq
    kd = qi // per
    par = qi - kd * per
    q = q_ref[...]
    lane = lax.broadcasted_iota(jnp.int32, q.shape, 1)
    zero = jnp.zeros_like(q)
    q2 = jnp.concatenate([jnp.where(lane < DIFF_QK, q, zero), jnp.where(lane < DIFF_QK, zero, q)], axis=0)

    def scores(kj, bias):
        k0 = pl.multiple_of(kj * tk, tk)
        s = _dot_nt(q2, k_ref[pl.ds(k0, tk), :])
        if bias is not None:
            s = s + jnp.concatenate([bias, bias], axis=0)
        return s, v_ref[pl.ds(k0, tk), :]

    def attend(bounded):
        s, v = scores(kd, dt_ref[par])
        _softmax_first(s, v, m_sc, acc_sc, bounded)

        @pl.when(kd >= 1)
        def _():
            s, v = scores(kd - 1, dt_ref[per + par])
            _softmax_next(s, v, m_sc, acc_sc, bounded)

        def far(i, _):
            s, v = scores(kd - 2 - i, None)
            _softmax_next(s, v, m_sc, acc_sc, bounded)
            return 0

        lax.fori_loop(0, jnp.maximum(kd - 1, 0), far, 0)
        o2 = _softmax_result(acc_sc, bounded)
        lp = lam_ref[...]
        lam = (jnp.exp(jnp.sum(lp[0:1] * lp[1:2], axis=-1, keepdims=True))
               - jnp.exp(jnp.sum(lp[2:3] * lp[3:4], axis=-1, keepdims=True)) + lam_init)
        o = o2[:tq] - lam * o2[tq:]
        ms = jnp.mean(o * o, axis=-1, keepdims=True)
        o = o * lax.rsqrt(ms + NORM_EPS) * og_ref[...]
        o_ref[...] = (o * (1.0 - lam_init)).astype(o_ref.dtype)

    @pl.when(ok_ref[0] > 0)
    def _():
        attend(True)

    @pl.when(ok_ref[0] == 0)
    def _():
        attend(False)


def _diff_attention(p_df, rel_diff, lam_params, out_gain, q_gain, k_gain, lam_init, batch, seq):
    tq = min(256, seq)
    tk = min(512, seq)
    per = tk // tq
    nq = seq // tq
    t = batch * seq
    dt = _toeplitz_tiles(rel_diff, [i * tq for i in range(2 * per)], tq, tk, 0, 1 << 30)
    ok = _logit_bound_ok(DIFF_QK, DIFF_QK ** -0.5, q_gain, k_gain, rel_diff)
    vmem = 4 * seq * HEAD_DIM * 2 + 2 * 2 * per * 2 * tq * tk * 4 + 12 * 2 * tq * tk * 4
    return pl.pallas_call(
        functools.partial(_diff_kernel, tq=tq, tk=tk, lam_init=lam_init),
        grid_spec=pltpu.PrefetchScalarGridSpec(
            num_scalar_prefetch=1,
            grid=(batch, N_HEADS, nq),
            in_specs=[pl.BlockSpec((tq, HEAD_DIM), lambda b, h, i, ok: (b * nq + i, h)),
                      pl.BlockSpec((seq, HEAD_DIM), lambda b, h, i, ok: (b, N_HEADS + h)),
                      pl.BlockSpec((seq, HEAD_DIM), lambda b, h, i, ok: (b, 2 * N_HEADS + h)),
                      pl.BlockSpec((None, 2 * per, tq, tk), lambda b, h, i, ok: (h, 0, 0, 0)),
                      pl.BlockSpec((4, DIFF_QK), lambda b, h, i, ok: (0, 0)),
                      pl.BlockSpec((1, HEAD_DIM), lambda b, h, i, ok: (0, 0))],
            out_specs=pl.BlockSpec((tq, HEAD_DIM), lambda b, h, i, ok: (b * nq + i, h)),
            scratch_shapes=[pltpu.VMEM((2 * tq, 1), F32), pltpu.VMEM((2 * tq, 2 * HEAD_DIM), F32)]),
        out_shape=jax.ShapeDtypeStruct((t, BRANCH_W), BF16),
        compiler_params=_cparams(("arbitrary",) * 3, vmem),
        name="diff_attention",
    )(ok, p_df, p_df, p_df, dt, lam_params, out_gain.reshape(1, HEAD_DIM))


def _merge_kernel(o0_ref, o1_ref, o2_ref, w_ref, g0_ref, g1_ref, g2_ref, out_ref, w_sc):
    @pl.when(pl.program_id(1) == 0)
    def _():
        w_sc[...] = w_ref[...].astype(BF16)

    acc = g0_ref[...].astype(F32) * _dot(o0_ref[...], w_sc[0])
    acc = acc + g1_ref[...].astype(F32) * _dot(o1_ref[...], w_sc[1])
    acc = acc + g2_ref[...].astype(F32) * _dot(o2_ref[...], w_sc[2])
    out_ref[...] = acc.astype(out_ref.dtype)


def _branch_merge(o_sb, o_ns, o_df, w_branch, layer, gates):
    t = o_sb.shape[0]
    d = w_branch.shape[3]
    tm = min(1024, t)
    tn = min(512, d)
    nj = d // tn
    o_spec = pl.BlockSpec((tm, BRANCH_W), lambda j, i: (i, 0))
    vmem = (2 * (3 * tm * BRANCH_W * 2 + 3 * BRANCH_W * tn * 4 + 3 * tm * tn * 2 + tm * tn * 2)
            + 3 * BRANCH_W * tn * 2 + 4 * tm * tn * 4)
    return pl.pallas_call(
        _merge_kernel,
        grid=(nj, t // tm),
        in_specs=[o_spec, o_spec, o_spec,
                  pl.BlockSpec((None, 3, BRANCH_W, tn), lambda j, i: (layer, 0, 0, j)),
                  pl.BlockSpec((tm, tn), lambda j, i: (i, j)),
                  pl.BlockSpec((tm, tn), lambda j, i: (i, nj + j)),
                  pl.BlockSpec((tm, tn), lambda j, i: (i, 2 * nj + j))],
        out_specs=pl.BlockSpec((tm, tn), lambda j, i: (i, j)),
        out_shape=jax.ShapeDtypeStruct((t, d), BF16),
        scratch_shapes=[pltpu.VMEM((3, BRANCH_W, tn), BF16)],
        compiler_params=_cparams(("arbitrary", "arbitrary"), vmem),
        name="branch_merge",
    )(o_sb, o_ns, o_df, w_branch, gates, gates, gates)


def _row_copy(src_ref, src_row, dst_ref, dst_row, sem):
    return pltpu.make_async_copy(src_ref.at[pl.ds(src_row, 1)], dst_ref.at[pl.ds(dst_row, 1)], sem)


def _gather_kernel(idx_ref, cnt_ref, src_ref, o_ref, buf, sem, *, tg):
    i = pl.program_id(0)
    base = i * tg
    n = cnt_ref[i]

    @pl.when(i == 0)
    def _():
        buf[...] = jnp.zeros(buf.shape, buf.dtype)

    for parity in range(2):
        def issue(r2, _, parity=parity):
            r = 2 * r2 + parity
            _row_copy(src_ref, idx_ref[base + r], buf, r, sem).start(priority=parity)
            return 0

        lax.fori_loop(0, (n + 1 - parity) // 2, issue, 0)

    def drain(r, _):
        _row_copy(src_ref, 0, buf, r, sem).wait()
        return 0

    lax.fori_loop(0, n, drain, 0)
    words = buf[...]
    low = pltpu.bitcast(lax.shift_left(words, jnp.uint32(16)), F32)
    high = pltpu.bitcast(words & jnp.uint32(0xFFFF0000), F32)
    o_ref[...] = jnp.concatenate([low, high], axis=1).astype(o_ref.dtype)


def _gather_rows(row_token, tile_rows, h2_words, n_rows, tg):
    half = h2_words.shape[1]
    return pl.pallas_call(
        functools.partial(_gather_kernel, tg=tg),
        grid_spec=pltpu.PrefetchScalarGridSpec(
            num_scalar_prefetch=2,
            grid=(n_rows // tg,),
            in_specs=[pl.BlockSpec(memory_space=pl.ANY)],
            out_specs=pl.BlockSpec((tg, 2 * half), lambda i, idx, cnt: (i, 0)),
            scratch_shapes=[pltpu.VMEM((tg, half), jnp.uint32), pltpu.SemaphoreType.DMA(())]),
        out_shape=jax.ShapeDtypeStruct((n_rows, 2 * half), BF16),
        compiler_params=_cparams(("arbitrary",), tg * half * 4 + 6 * tg * half * 4),
        name="moe_gather",
    )(row_token, tile_rows, h2_words)


def _expert_changed(te_ref, t):
    return (t == 0) | (te_ref[t] != te_ref[jnp.maximum(t - 1, 0)])


def _expert_up_kernel(te_ref, tv_ref, x_ref, wg_ref, wu_ref, o_ref, wg_sc, wu_sc):
    t = pl.program_id(1)

    @pl.when(_expert_changed(te_ref, t))
    def _():
        wg_sc[...] = wg_ref[...].astype(BF16)
        wu_sc[...] = wu_ref[...].astype(BF16)

    @pl.when(tv_ref[t] > 0)
    def _():
        x = x_ref[...]
        g = _dot(x, wg_sc[...])
        u = _dot(x, wu_sc[...])
        o_ref[...] = (g * _ep_sigmoid(g, (), 0) * u).astype(o_ref.dtype)

    @pl.when(tv_ref[t] == 0)
    def _():
        o_ref[...] = jnp.zeros(o_ref.shape, o_ref.dtype)


def _expert_down_kernel(te_ref, tv_ref, h_ref, wd_ref, o_ref, wd_sc):
    t = pl.program_id(1)

    @pl.when(_expert_changed(te_ref, t))
    def _():
        wd_sc[...] = wd_ref[...].astype(BF16)

    @pl.when(tv_ref[t] > 0)
    def _():
        o_ref[...] = _dot(h_ref[...], wd_sc[...])

    @pl.when(tv_ref[t] == 0)
    def _():
        o_ref[...] = jnp.zeros(o_ref.shape, o_ref.dtype)


def _expert_mlps(tile_expert, tile_valid, xs, layer, w_gate, w_up, w_down, tm):
    n_rows, d = xs.shape
    f = w_gate.shape[3]
    tf = min(512, f)
    td = min(2048, d)
    nt = n_rows // tm
    hidden = pl.pallas_call(
        _expert_up_kernel,
        grid_spec=pltpu.PrefetchScalarGridSpec(
            num_scalar_prefetch=2,
            grid=(f // tf, nt),
            in_specs=[pl.BlockSpec((tm, d), lambda j, t, te, tv: (t, 0)),
                      pl.BlockSpec((None, None, d, tf), lambda j, t, te, tv: (layer, te[t], 0, j)),
                      pl.BlockSpec((None, None, d, tf), lambda j, t, te, tv: (layer, te[t], 0, j))],
            out_specs=pl.BlockSpec((tm, tf), lambda j, t, te, tv: (t, j)),
            scratch_shapes=[pltpu.VMEM((d, tf), BF16), pltpu.VMEM((d, tf), BF16)]),
        out_shape=jax.ShapeDtypeStruct((n_rows, f), BF16),
        compiler_params=_cparams(("arbitrary", "arbitrary"),
                                 2 * (tm * d * 2 + 2 * d * tf * 4 + tm * tf * 2) + 2 * d * tf * 2 + 4 * tm * tf * 4),
        name="moe_gate_up",
    )(tile_expert, tile_valid, xs, w_gate, w_up)
    return pl.pallas_call(
        _expert_down_kernel,
        grid_spec=pltpu.PrefetchScalarGridSpec(
            num_scalar_prefetch=2,
            grid=(d // td, nt),
            in_specs=[pl.BlockSpec((tm, f), lambda j, t, te, tv: (t, 0)),
                      pl.BlockSpec((None, None, f, td), lambda j, t, te, tv: (layer, te[t], 0, j))],
            out_specs=pl.BlockSpec((tm, td), lambda j, t, te, tv: (t, j)),
            scratch_shapes=[pltpu.VMEM((f, td), BF16)]),
        out_shape=jax.ShapeDtypeStruct((n_rows, d), F32),
        compiler_params=_cparams(("arbitrary", "arbitrary"),
                                 2 * (tm * f * 2 + f * td * 4 + tm * td * 4) + f * td * 2 + tm * td * 4),
        name="moe_down",
    )(tile_expert, tile_valid, hidden, w_down)


def _combine_kernel(pos_ref, y_ref, x_ref, g_ref, route_ref, o_ref, buf0, buf1, sem,
                    *, tc, tiles_per_batch):
    i = pl.program_id(0)
    base = i * tc

    def issue(r, _):
        _row_copy(y_ref, pos_ref[2 * (base + r)], buf0, r, sem).start(priority=0)
        _row_copy(y_ref, pos_ref[2 * (base + r) + 1], buf1, r, sem).start(priority=1)
        return 0

    lax.fori_loop(0, tc, issue, 0)

    def drain(r, _):
        _row_copy(y_ref, 0, buf0, r, sem).wait()
        _row_copy(y_ref, 0, buf1, r, sem).wait()
        return 0

    lax.fori_loop(0, tc, drain, 0)
    route = route_ref[...]
    moe = route[:, 2:3] * buf0[...] + route[:, 3:4] * buf1[...]
    o_ref[...] = x_ref[...] + g_ref[pl.ds(i // tiles_per_batch, 1), :] * moe


def _combine(pos, y, x2, mod_l, g_chunk, route, seq):
    t, d = x2.shape
    tc = 128
    return pl.pallas_call(
        functools.partial(_combine_kernel, tc=tc, tiles_per_batch=seq // tc),
        grid_spec=pltpu.PrefetchScalarGridSpec(
            num_scalar_prefetch=1,
            grid=(t // tc,),
            in_specs=[pl.BlockSpec(memory_space=pl.ANY),
                      pl.BlockSpec((tc, d), lambda i, p: (i, 0)),
                      pl.BlockSpec((SUBLANES, d), lambda i, p: (0, g_chunk)),
                      pl.BlockSpec((tc, LANES), lambda i, p: (i, 0))],
            out_specs=pl.BlockSpec((tc, d), lambda i, p: (i, 0)),
            scratch_shapes=[pltpu.VMEM((tc, d), F32), pltpu.VMEM((tc, d), F32),
                            pltpu.SemaphoreType.DMA(())]),
        out_shape=jax.ShapeDtypeStruct((t, d), F32),
        compiler_params=_cparams(("arbitrary",), 2 * tc * d * 4 + 4 * tc * d * 4 + 4 * tc * d * 4),
        name="moe_combine",
    )(pos, y, x2, mod_l, route)


def _routing_tables(route, tm):
    t = route.shape[0]
    experts = route[:, :2].astype(jnp.int32).reshape(-1)
    onehot = (experts[:, None] == jnp.arange(N_EXPERTS)[None, :]).astype(jnp.int32)
    csum = jnp.cumsum(onehot, axis=0)
    rank = jnp.sum((csum - onehot) * onehot, axis=1)
    counts = csum[-1]
    padded = ((counts + tm - 1) // tm) * tm
    ends = jnp.cumsum(padded)
    starts = ends - padded
    dest = starts[experts] + rank
    n_rows = 2 * t + N_EXPERTS * tm
    row_token = jnp.zeros((n_rows,), jnp.int32).at[dest].set(jnp.arange(2 * t, dtype=jnp.int32) // 2)
    tile_start = jnp.arange(n_rows // tm, dtype=jnp.int32) * tm
    tile_expert = jnp.minimum(jnp.sum((tile_start[:, None] >= ends[None, :]).astype(jnp.int32), axis=1),
                              N_EXPERTS - 1)
    tile_valid = (tile_start < ends[-1]).astype(jnp.int32)
    real_end = (starts + counts)[tile_expert]
    tile_rows = jnp.clip(real_end - tile_start, 0, tm).astype(jnp.int32) * tile_valid
    return row_token, dest.astype(jnp.int32), tile_expert, tile_valid, tile_rows, n_rows


def _tiled(vec, reps):
    return jnp.tile(vec.astype(F32), reps)


def _mixer(h, seq, batch, layer, w_in_t, w_branch, w_out, rel_bias, cmp_pos, cmp_w1, cmp_w2,
           nsa_q_g, nsa_k_g, diff_q_g, diff_k_g, diff_lam, diff_out_g, lam_init, x2, mod_l, d):
    scale = HEAD_DIM ** -0.5
    ones = lambda n: jnp.ones((n,), F32)
    zeros = lambda n: jnp.zeros((n,), F32)
    proj = functools.partial(_matmul_w32, h, w_in_t, (layer,), tm=1024, tn=512, transposed=True)

    cs = jnp.concatenate([jnp.full((BRANCH_W,), scale, F32), ones(2 * BRANCH_W)]).reshape(1, -1)
    p_sb = proj(_C_SB, 3 * BRANCH_W, [(cs, "col")], _ep_scale, BF16, name="proj_sb")
    nf = ones(BRANCH_W).reshape(1, -1)
    cs = (_tiled(nsa_q_g, N_HEADS) * scale).reshape(1, -1)
    p_nsq = proj(_C_NSQ, BRANCH_W, [(nf, "col"), (cs, "col")],
                 functools.partial(_ep_norm, width=HEAD_DIM), BF16, name="proj_nsq")
    nf = jnp.concatenate([zeros(2 * KV_W), ones(KV_W), zeros(KV_W), ones(KV_W), zeros(KV_W)]).reshape(1, -1)
    cs = jnp.concatenate([ones(2 * KV_W), _tiled(nsa_k_g[1], KV_HEADS), ones(KV_W),
                          _tiled(nsa_k_g[2], KV_HEADS), ones(KV_W)]).reshape(1, -1)
    p_nskv = proj(_C_NSKV, 6 * KV_W, [(nf, "col"), (cs, "col")],
                  functools.partial(_ep_norm, width=HEAD_DIM), BF16, name="proj_nskv")
    ns_gates = proj(_C_GNS, LANES, [], _ep_sigmoid, F32, name="proj_nsgate")
    nf = jnp.concatenate([ones(2 * BRANCH_W), zeros(BRANCH_W)]).reshape(1, -1)
    cs = jnp.concatenate([_tiled(diff_q_g, 2 * N_HEADS) * DIFF_QK ** -0.5, _tiled(diff_k_g, 2 * N_HEADS),
                          ones(BRANCH_W)]).reshape(1, -1)
    p_df = proj(_C_DF, 3 * BRANCH_W, [(nf, "col"), (cs, "col")],
                functools.partial(_ep_norm, width=DIFF_QK), BF16, name="proj_diff")
    m_gates = proj(_C_MERGE, 3 * d, [], _ep_sigmoid, BF16, name="proj_merge_gate")

    o_sb = _sb_attention(p_sb, batch, seq)
    kvcm = _compress(p_nskv, cmp_pos, cmp_w1, cmp_w2, nsa_k_g[0], batch, seq)
    o_ns = _nsa_attention(p_nsq, kvcm, p_nskv, ns_gates, rel_bias[:, :N_HEADS], nsa_q_g, nsa_k_g, batch, seq)
    o_df = _diff_attention(p_df, rel_bias[:, N_HEADS:], diff_lam, diff_out_g, diff_q_g, diff_k_g, lam_init,
                           batch, seq)
    merged = _branch_merge(o_sb, o_ns, o_df, w_branch, layer, m_gates)
    g1 = lax.slice_in_dim(mod_l, 2 * d, 3 * d, axis=1)
    return _matmul_w32(merged, w_out, (layer,), 0, d, [(x2, "tile"), (g1, "batch")], _ep_residual, F32,
                       tm=1024, tn=512, name="proj_out", seq=seq)


def _moe(x2, seq, mod_l, norm_gain, w_router, router_bias, layer, w_gate, w_up, w_down):
    tm = 256
    h2, route = _norm_router(x2, norm_gain, mod_l, 3, 4, seq, w_router, router_bias)
    row_token, pos, tile_expert, tile_valid, tile_rows, n_rows = _routing_tables(route, tm)
    xs = _gather_rows(row_token, tile_rows, h2, n_rows, tm)
    y = _expert_mlps(tile_expert, tile_valid, xs, layer, w_gate, w_up, w_down, tm)
    return _combine(pos, y, x2, mod_l, 5, route, seq)


def kernel(x, c, rel_bias, w_router, router_bias, w_ada, b_ada, norm_mix, norm_ffn, w_in, nsa_cmp_pos,
           nsa_cmp_w1, nsa_cmp_w2, nsa_q_norm, nsa_k_norm, diff_q_norm, diff_k_norm, diff_lambda,
           diff_out_norm, w_branch, w_out, w_exp_gate, w_exp_up, w_exp_down):
    batch, seq, d = x.shape
    depth = w_ada.shape[0]
    mod = _adaln(c, w_ada, b_ada)
    x2 = x.reshape(batch * seq, d)
    w_in_t = jnp.swapaxes(w_in, 1, 2)
    for layer in range(depth):
        lam_init = 0.8 - 0.6 * math.exp(-0.3 * layer)
        mod_l = mod[layer]
        h = _norm_mod(x2, norm_mix[layer], mod_l, 0, 1, seq)
        x2 = _mixer(h, seq, batch, layer, w_in_t, w_branch, w_out, rel_bias,
                    nsa_cmp_pos[layer], nsa_cmp_w1[layer], nsa_cmp_w2[layer], nsa_q_norm[layer],
                    nsa_k_norm[layer], diff_q_norm[layer], diff_k_norm[layer], diff_lambda[layer],
                    diff_out_norm[layer], lam_init, x2, mod_l, d)
        x2 = _moe(x2, seq, mod_l, norm_ffn[layer], w_router, router_bias, layer, w_exp_gate,
                  w_exp_up, w_exp_down)
    return x2.reshape(batch, seq, d)
```
